```python
import jax, jax.numpy as jnp
from jax import lax
import numpy as np

D_MODEL = 1024
BATCH = 16
SEQ = 4096
DEPTH = 2

N_META = 16

HEAD_DIM = 64
RWKV_HEADS = 8
RWKV_WIDTH = RWKV_HEADS * HEAD_DIM
DECAY_LORA = 64
ICLR_LORA = 64
GATE_LORA = 160
GN_EPS = 64e-5

MLA_HEADS = 8
QK_NOPE_DIM = 64
QK_ROPE_DIM = 32
V_HEAD_DIM = 64
Q_LORA_RANK = 256
KV_LORA_RANK = 256
ROPE_THETA = 10000.0
ATTN_BLOCK = 128
MLA_WIDTH = MLA_HEADS * V_HEAD_DIM

D_FF = 2816
CONV_WIDTH = 3

N_BRANCHES = 2
RWKV_COLS = 3 * RWKV_WIDTH + DECAY_LORA + ICLR_LORA + GATE_LORA
MLA_COLS = Q_LORA_RANK + KV_LORA_RANK + QK_ROPE_DIM
GATE_COLS = N_BRANCHES * D_MODEL
IN_COLS = RWKV_COLS + MLA_COLS + GATE_COLS

DEEPNORM_ALPHA = float((2 * DEPTH) ** 0.25)
DEEPNORM_BETA = float((8 * DEPTH) ** -0.25)
LN_EPS = 1e-5
RMS_EPS = 1e-6

kernel_name = "hybrid_rwkv7_mla_convffn_deepnorm"


def _layer_norm(x, g, b):
    xf = x.astype(jnp.float32)
    mu = jnp.mean(xf, axis=-1, keepdims=True)
    var = jnp.mean(jnp.square(xf - mu), axis=-1, keepdims=True)
    return ((xf - mu) * lax.rsqrt(var + LN_EPS) * g + b).astype(x.dtype)


def _rms_norm(x, g):
    xf = x.astype(jnp.float32)
    return (xf * lax.rsqrt(jnp.mean(jnp.square(xf), axis=-1, keepdims=True) + RMS_EPS) * g).astype(x.dtype)


def _shift(x, n):
    return jnp.pad(x, ((0, 0), (n, 0), (0, 0)))[:, : x.shape[1]]


def _rope(x, pos):
    half = x.shape[-1] // 2
    inv_freq = ROPE_THETA ** (-jnp.arange(half, dtype=jnp.float32) / half)
    ang = pos.astype(jnp.float32)[:, None] * inv_freq[None, :]
    cos = jnp.cos(ang)[None, :, None, :]
    sin = jnp.sin(ang)[None, :, None, :]
    x1 = x[..., :half].astype(jnp.float32)
    x2 = x[..., half:].astype(jnp.float32)
    return jnp.concatenate([x1 * cos - x2 * sin, x1 * sin + x2 * cos], axis=-1).astype(x.dtype)


def _wkv7_scan(r, decay, k, v, kk, a):
    B, T, H, N = r.shape

    def step(S, inp):
        r_t, w_t, k_t, v_t, kk_t, a_t = inp
        s_kk = jnp.einsum('bhvk,bhk->bhv', S, kk_t)
        S = (S * w_t[:, :, None, :]
             - s_kk[..., :, None] * (kk_t * a_t)[:, :, None, :]
             + v_t[..., :, None] * k_t[:, :, None, :])
        y_t = jnp.einsum('bhvk,bhk->bhv', S, r_t)
        return S, y_t

    xs = tuple(jnp.moveaxis(t, 1, 0) for t in (r, decay, k, v, kk, a))
    S0 = jnp.zeros((B, H, N, N), dtype=r.dtype)
    _, y = lax.scan(step, S0, xs)
    return jnp.moveaxis(y, 0, 1)


def _rwkv7_mix(p, mu, w0, w_lora_up, a0, a_lora_up, g_lora_up, k_k, k_a, r_k, lnx_g, lnx_b):
    B, T, _ = p.shape
    p = p + (_shift(p, 1) - p) * mu
    splits = np.cumsum([RWKV_WIDTH, RWKV_WIDTH, RWKV_WIDTH, DECAY_LORA, ICLR_LORA]).tolist()
    r, k, v, wd, ad, gd = jnp.split(p, splits, axis=-1)
    w_raw = -jax.nn.softplus(-(w0 + jnp.tanh(wd) @ w_lora_up)) - 0.5
    decay = jnp.exp(-jnp.exp(w_raw.astype(jnp.float32))).astype(p.dtype)
    a = jax.nn.sigmoid(a0 + ad @ a_lora_up)
    g = jax.nn.sigmoid(gd) @ g_lora_up

    def heads(t):
        return t.reshape(B, T, RWKV_HEADS, HEAD_DIM)

    kk = heads(k * k_k).astype(jnp.float32)
    kk = (kk / jnp.maximum(jnp.sqrt(jnp.sum(kk * kk, axis=-1, keepdims=True)), 1e-12)).astype(p.dtype)
    k = k * (1.0 + (a - 1.0) * k_a)
    r_h, k_h, v_h, a_h, w_h = heads(r), heads(k), heads(v), heads(a), heads(decay)
    y = _wkv7_scan(r_h, w_h, k_h, v_h, kk, a_h)
    yf = y.astype(jnp.float32)
    ym = jnp.mean(yf, axis=-1, keepdims=True)
    yv = jnp.mean(jnp.square(yf - ym), axis=-1, keepdims=True)
    yn = ((yf - ym) * lax.rsqrt(yv + GN_EPS)).reshape(B, T, RWKV_WIDTH) * lnx_g + lnx_b
    yn = yn.astype(p.dtype)
    bonus = (jnp.sum(r_h * k_h * r_k, axis=-1, keepdims=True) * v_h).reshape(B, T, RWKV_WIDTH)
    return (yn + bonus) * g


def _causal_attention(q, k, v):
    T = q.shape[1]
    scale = 1.0 / float(np.sqrt(q.shape[-1]))
    bounds = [(0, N_META)] + [(s, min(s + ATTN_BLOCK, T)) for s in range(N_META, T, ATTN_BLOCK)]
    outs = []
    for s, e in bounds:
        sc = jnp.einsum('bqhd,bkhd->bhqk', q[:, s:e], k[:, :e],
                        preferred_element_type=jnp.float32) * scale
        causal = jnp.arange(e)[None, :] <= jnp.arange(s, e)[:, None]
        sc = jnp.where(causal[None, None], sc, -jnp.inf)
        probs = jax.nn.softmax(sc, axis=-1).astype(v.dtype)
        outs.append(jnp.einsum('bhqk,bkhd->bqhd', probs, v[:, :e]))
    return jnp.concatenate(outs, axis=1)


def _mla(p, pos, q_norm_g, w_uq, kv_norm_g, w_uk, w_uv):
    B, T, _ = p.shape
    cq, ckv, kr = jnp.split(p, [Q_LORA_RANK, Q_LORA_RANK + KV_LORA_RANK], axis=-1)
    q = (_rms_norm(cq, q_norm_g) @ w_uq).reshape(B, T, MLA_HEADS, QK_NOPE_DIM + QK_ROPE_DIM)
    q = jnp.concatenate([q[..., :QK_NOPE_DIM], _rope(q[..., QK_NOPE_DIM:], pos)], axis=-1)
    ckv = _rms_norm(ckv, kv_norm_g)
    k_nope = (ckv @ w_uk).reshape(B, T, MLA_HEADS, QK_NOPE_DIM)
    v = (ckv @ w_uv).reshape(B, T, MLA_HEADS, V_HEAD_DIM)
    k_rope = _rope(kr[:, :, None, :], pos)
    k = jnp.concatenate([k_nope, jnp.broadcast_to(k_rope, (B, T, MLA_HEADS, QK_ROPE_DIM))], axis=-1)
    return _causal_attention(q, k, v).reshape(B, T, MLA_WIDTH)


def _conv_ffn(x, w_up, conv_w, conv_b, w_down):
    u = x @ w_up
    u = sum(conv_w[CONV_WIDTH - 1 - j] * _shift(u, j) for j in range(CONV_WIDTH)) + conv_b
    gate, val = jnp.split(u, 2, axis=-1)
    return (jax.nn.silu(gate) * val) @ w_down


def setup_inputs(seed: int = 0) -> dict:
    key = jax.random.key(seed)
    ks = iter(list(jax.random.split(key, 48)))
    L = DEPTH
    beta = DEEPNORM_BETA

    def nrm(shape, scale):
        return scale * jax.random.normal(next(ks), shape, jnp.float32)

    def gain(shape):
        return 1.0 + nrm(shape, 0.02)

    def unif(shape, lo, hi):
        return jax.random.uniform(next(ks), shape, jnp.float32, lo, hi)

    return {
        'x': nrm((BATCH, SEQ, D_MODEL), 1.0),
        'meta_tokens': nrm((N_META, D_MODEL), 1.0),
        'ln_in_g': gain((D_MODEL,)),
        'ln_in_b': nrm((D_MODEL,), 0.02),
        'w_in': nrm((L, D_MODEL, IN_COLS), D_MODEL ** -0.5),
        'mu_shift': unif((L, RWKV_COLS), 0.0, 1.0),
        'w0': unif((L, RWKV_WIDTH), -5.0, 1.0),
        'w_lora_up': nrm((L, DECAY_LORA, RWKV_WIDTH), 0.1),
        'a0': nrm((L, RWKV_WIDTH), 0.5),
        'a_lora_up': nrm((L, ICLR_LORA, RWKV_WIDTH), 0.1),
        'g_lora_up': nrm((L, GATE_LORA, RWKV_WIDTH), GATE_LORA ** -0.5),
        'k_k': 0.85 + nrm((L, RWKV_WIDTH), 0.02),
        'k_a': gain((L, RWKV_WIDTH)),
        'r_k': nrm((L, RWKV_HEADS, HEAD_DIM), 0.1),
        'lnx_g': gain((L, RWKV_WIDTH)),
        'lnx_b': nrm((L, RWKV_WIDTH), 0.02),
        'q_norm_g': gain((L, Q_LORA_RANK)),
        'w_uq': nrm((L, Q_LORA_RANK, MLA_HEADS * (QK_NOPE_DIM + QK_ROPE_DIM)), Q_LORA_RANK ** -0.5),
        'kv_norm_g': gain((L, KV_LORA_RANK)),
        'w_uk': nrm((L, KV_LORA_RANK, MLA_HEADS * QK_NOPE_DIM), KV_LORA_RANK ** -0.5),
        'w_uv': nrm((L, KV_LORA_RANK, MLA_HEADS * V_HEAD_DIM), KV_LORA_RANK ** -0.5),
        'w_proj_rwkv': nrm((L, RWKV_WIDTH, D_MODEL), beta * RWKV_WIDTH ** -0.5),
        'w_proj_mla': nrm((L, MLA_WIDTH, D_MODEL), beta * MLA_WIDTH ** -0.5),
        'w_out': nrm((L, D_MODEL, D_MODEL), beta * D_MODEL ** -0.5),
        'ln1_g': gain((L, D_MODEL)),
        'ln1_b': nrm((L, D_MODEL), 0.02),
        'w_up': nrm((L, D_MODEL, 2 * D_FF), D_MODEL ** -0.5),
        'conv_w': nrm((L, CONV_WIDTH, 2 * D_FF), CONV_WIDTH ** -0.5),
        'conv_b': nrm((L, 2 * D_FF), 0.02),
        'w_down': nrm((L, D_FF, D_MODEL), beta * D_FF ** -0.5),
        'ln2_g': gain((L, D_MODEL)),
        'ln2_b': nrm((L, D_MODEL), 0.02),
    }


def reference(x, meta_tokens, ln_in_g, ln_in_b, w_in, mu_shift, w0, w_lora_up, a0, a_lora_up,
              g_lora_up, k_k, k_a, r_k, lnx_g, lnx_b, q_norm_g, w_uq, kv_norm_g, w_uk, w_uv,
              w_proj_rwkv, w_proj_mla, w_out, ln1_g, ln1_b, w_up, conv_w, conv_b, w_down,
              ln2_g, ln2_b):
    B = x.shape[0]
    meta = jnp.broadcast_to(meta_tokens[None].astype(x.dtype), (B, N_META, D_MODEL))
    h = jnp.concatenate([meta, x], axis=1)
    T = h.shape[1]
    pos = jnp.arange(T, dtype=jnp.int32)
    h = _layer_norm(h, ln_in_g, ln_in_b)
    for l in range(DEPTH):
        p = h @ w_in[l]
        p_rwkv, p_mla, p_gate = jnp.split(p, [RWKV_COLS, RWKV_COLS + MLA_COLS], axis=-1)
        y_rwkv = _rwkv7_mix(p_rwkv, mu_shift[l], w0[l], w_lora_up[l], a0[l], a_lora_up[l],
                            g_lora_up[l], k_k[l], k_a[l], r_k[l], lnx_g[l], lnx_b[l])
        y_mla = _mla(p_mla, pos, q_norm_g[l], w_uq[l], kv_norm_g[l], w_uk[l], w_uv[l])
        gate_rwkv, gate_mla = jnp.split(jax.nn.sigmoid(p_gate), N_BRANCHES, axis=-1)
        mixed = gate_rwkv * (y_rwkv @ w_proj_rwkv[l]) + gate_mla * (y_mla @ w_proj_mla[l])
        h = _layer_norm(DEEPNORM_ALPHA * h + mixed @ w_out[l], ln1_g[l], ln1_b[l])
        f = _conv_ffn(h, w_up[l], conv_w[l], conv_b[l], w_down[l])
        h = _layer_norm(DEEPNORM_ALPHA * h + f, ln2_g[l], ln2_b[l])
    return h[:, N_META:]
```

```python
import functools

import numpy as np
import jax
import jax.numpy as jnp
from jax import lax
from jax.experimental import pallas as pl
from jax.experimental.pallas import tpu as pltpu

N_META = 16
HEAD_DIM = 64
RWKV_HEADS = 8
RWKV_WIDTH = RWKV_HEADS * HEAD_DIM
DECAY_LORA = 64
ICLR_LORA = 64
GATE_LORA = 160
GN_EPS = 64e-5
MLA_HEADS = 8
QK_NOPE_DIM = 64
QK_ROPE_DIM = 32
V_HEAD_DIM = 64
Q_LORA_RANK = 256
KV_LORA_RANK = 256
ROPE_THETA = 10000.0
MLA_WIDTH = MLA_HEADS * V_HEAD_DIM
CONV_WIDTH = 3
LN_EPS = 1e-5
RMS_EPS = 1e-6

LANES = 128
MXU_DIM = 256
VMEM_LIMIT = 56 * 1024 * 1024

FRONT = LANES
PADF = FRONT - N_META
C_R, C_K, C_V = 0, RWKV_WIDTH, 2 * RWKV_WIDTH
C_WA = 3 * RWKV_WIDTH
GATE_LORA_PAD = 2 * LANES
C_G = C_WA + LANES
RW_END = C_G + GATE_LORA_PAD
C_CQ = RW_END
C_CKV = C_CQ + Q_LORA_RANK
C_KR = C_CKV + KV_LORA_RANK
C_GATE = C_KR + LANES
QK_PAD = LANES

WKV_CHUNK = 64
ATT_TILE = 384
PREP_TM = 256
ROW_TM = 512
FF_CHUNK = 256
NEG_BIG = -1e30

_F32 = jnp.float32
_BF16 = jnp.bfloat16


def _cparams(*sem):
    return pltpu.CompilerParams(dimension_semantics=sem, vmem_limit_bytes=VMEM_LIMIT)


def _dot(a, b):
    return jnp.dot(a, b, preferred_element_type=_F32)


def _dot_nt(a, b):
    return lax.dot_general(a, b, (((1,), (1,)), ((), ())), preferred_element_type=_F32)


def _dot_tn(a, b):
    return lax.dot_general(a, b, (((0,), (0,)), ((), ())), preferred_element_type=_F32)


def _split_dot(x, w_bf16, parts, left=False):
    acc = None
    rem = x
    for _ in range(parts):
        hi = rem.astype(_BF16)
        term = _dot(w_bf16, hi) if left else _dot(hi, w_bf16)
        acc = term if acc is None else acc + term
        rem = rem - hi.astype(_F32)
    return acc


def _layer_norm(x, g, b):
    mu = jnp.mean(x, axis=-1, keepdims=True)
    xc = x - mu
    var = jnp.mean(xc * xc, axis=-1, keepdims=True)
    return xc * lax.rsqrt(var + LN_EPS) * g + b


def _sigmoid(x):
    return 1.0 / (1.0 + jnp.exp(-x))


def _valid_rows(tile_idx, tm, tp):
    row = tile_idx * tm + lax.broadcasted_iota(jnp.int32, (tm, 1), 0)
    bidx = jnp.floor((row.astype(_F32) + 0.5) * (1.0 / tp)).astype(jnp.int32)
    return (row - bidx * tp) >= PADF


def _embed_kernel(x_ref, meta_ref, g_ref, b_ref, o_ref):
    t = pl.program_id(1)
    src = jnp.where(t == 0, meta_ref[...], x_ref[0])
    y = _layer_norm(src, g_ref[...], b_ref[...])
    row = lax.broadcasted_iota(jnp.int32, (FRONT, 1), 0)
    o_ref[0] = jnp.where((t > 0) | (row >= PADF), y, 0.0)


def _embed(x, meta_pad, g, b):
    bsz, seq, d = x.shape
    nt = seq // FRONT + 1
    return pl.pallas_call(
        _embed_kernel,
        out_shape=jax.ShapeDtypeStruct((bsz, nt * FRONT, d), _F32),
        grid=(bsz, nt),
        in_specs=[
            pl.BlockSpec((1, FRONT, d), lambda bi, t: (bi, jnp.maximum(t - 1, 0), 0)),
            pl.BlockSpec((FRONT, d), lambda bi, t: (0, 0)),
            pl.BlockSpec((1, d), lambda bi, t: (0, 0)),
            pl.BlockSpec((1, d), lambda bi, t: (0, 0)),
        ],
        out_specs=pl.BlockSpec((1, FRONT, d), lambda bi, t: (bi, t, 0)),
        compiler_params=_cparams("parallel", "arbitrary"),
        name="embed_ln",
    )(x, meta_pad, g, b)


def _inproj_kernel(h_ref, w_ref, mix_ref, gate_ref, *, n_chunk):
    hb = h_ref[...].astype(_BF16)
    for o_ref, base in ((mix_ref, 0), (gate_ref, C_GATE)):
        for c0 in range(0, o_ref.shape[1], n_chunk):
            o_ref[:, c0:c0 + n_chunk] = _dot(
                hb, w_ref[:, base + c0:base + c0 + n_chunk]).astype(o_ref.dtype)


def _inproj(h, w):
    r, d = h.shape
    n = w.shape[1]
    return pl.pallas_call(
        functools.partial(_inproj_kernel, n_chunk=2 * MXU_DIM),
        out_shape=[jax.ShapeDtypeStruct((r, C_GATE), _BF16),
                   jax.ShapeDtypeStruct((r, n - C_GATE), _BF16)],
        grid=(r // ROW_TM,),
        in_specs=[
            pl.BlockSpec((ROW_TM, d), lambda i: (i, 0)),
            pl.BlockSpec((d, n), lambda i: (0, 0)),
        ],
        out_specs=[pl.BlockSpec((ROW_TM, C_GATE), lambda i: (i, 0)),
                   pl.BlockSpec((ROW_TM, n - C_GATE), lambda i: (i, 0))],
        compiler_params=_cparams("parallel"),
        name="in_proj",
    )(h, w)


def _prep_kernel(p_ref, prev_ref, rope_ref, mu_ref, vec_ref, rk_ref, wa_up_ref, g_up_ref,
                 head_sum_ref, tri_ref, qg_ref, kvg_ref, wuq_ref, wuk_ref, wuv_ref,
                 kt_ref, rt_ref, kd_ref, bd_ref, v_ref, pc_ref, bonus_ref, g_ref,
                 q_ref, kk_ref, vv_ref):
    tm = p_ref.shape[0]
    w = RWKV_WIDTH

    pr = p_ref[:, :RW_END].astype(_F32)
    shifted = pltpu.roll(pr, 1, axis=0)
    row = lax.broadcasted_iota(jnp.int32, (tm, 1), 0)
    prev_last = prev_ref[7:8, :].astype(_F32)
    shifted = jnp.where(row == 0, prev_last, shifted)
    ps = pr + (shifted - pr) * mu_ref[...]

    r = ps[:, C_R:C_R + w]
    k = ps[:, C_K:C_K + w]
    v = ps[:, C_V:C_V + w]
    wa_in = ps[:, C_WA:C_WA + LANES]
    lane = lax.broadcasted_iota(jnp.int32, (1, LANES), 1)
    wa_in = jnp.where(lane < DECAY_LORA, jnp.tanh(wa_in), wa_in)
    wa = _dot(wa_in.astype(_BF16), wa_up_ref[...])
    w0, a0, k_k, k_a = (vec_ref[0:1, :], vec_ref[1:2, :], vec_ref[2:3, :], vec_ref[3:4, :])

    z = -(w0 + wa[:, :w])
    softplus = jnp.maximum(z, 0.0) + jnp.log1p(jnp.exp(-jnp.abs(z)))
    logw = -jnp.exp(-softplus - 0.5)
    a = _sigmoid(a0 + wa[:, w:])
    gate = _dot(_sigmoid(ps[:, C_G:C_G + GATE_LORA_PAD]).astype(_BF16), g_up_ref[...])

    head_sum = head_sum_ref[...]
    kk = k * k_k
    kk_ss = _split_dot(kk * kk, head_sum, 2)
    kk = kk / jnp.maximum(jnp.sqrt(kk_ss), 1e-12)
    k = k * (1.0 + (a - 1.0) * k_a)
    b = kk * a
    bonus = _split_dot(r * k * rk_ref[...], head_sum, 2) * v

    cum = _split_dot(logw, tri_ref[...], 3, left=True)
    e_neg = jnp.exp(-cum)
    kt_ref[...] = (kk * jnp.exp(cum - logw)).astype(_BF16)
    rt_ref[...] = (r * jnp.exp(cum)).astype(_BF16)
    kd_ref[...] = (k * e_neg).astype(_BF16)
    bd_ref[...] = (b * e_neg).astype(_BF16)
    v_ref[...] = v.astype(_BF16)
    bonus_ref[...] = bonus.astype(_BF16)
    g_ref[...] = gate.astype(_BF16)
    for ci in range(tm // WKV_CHUNK):
        last = (ci + 1) * WKV_CHUNK - 1
        pc_ref[ci] = jnp.exp(cum[last:last + 1, :])

    cos_t, sin_a, sin_b = rope_ref[0], rope_ref[1], rope_ref[2]

    def rope(x):
        nrep = x.shape[1] // LANES
        n = x.shape[1]
        c, sa, sb = (jnp.tile(t, (1, nrep)) if nrep > 1 else t for t in (cos_t, sin_a, sin_b))
        half = QK_ROPE_DIM // 2
        return x * c + pltpu.roll(x, n - half, axis=1) * sa + pltpu.roll(x, half, axis=1) * sb

    def rms(x, gain):
        return x * lax.rsqrt(jnp.mean(x * x, axis=-1, keepdims=True) + RMS_EPS) * gain

    cq = p_ref[:, C_CQ:C_CQ + Q_LORA_RANK].astype(_F32)
    ckv = p_ref[:, C_CKV:C_CKV + KV_LORA_RANK].astype(_F32)
    kr = p_ref[:, C_KR:C_KR + LANES].astype(_F32)
    qn = rms(cq, qg_ref[...]).astype(_BF16)
    scale = 1.0 / float(np.sqrt(QK_NOPE_DIM + QK_ROPE_DIM))
    q_ref[...] = (rope(_dot(qn, wuq_ref[...])) * scale).astype(_BF16)
    kvn = rms(ckv, kvg_ref[...]).astype(_BF16)
    kk_ref[...] = (_dot(kvn, wuk_ref[...]) + jnp.tile(rope(kr), (1, MLA_HEADS))).astype(_BF16)
    vv_ref[...] = _dot(kvn, wuv_ref[...]).astype(_BF16)


def _prep(p, rope_tab, mu, vec, rk, wa_up, g_up, head_sum, tri, qg, kvg, wuq, wuk, wuv):
    r = p.shape[0]
    tm = PREP_TM
    nt = r // tm
    nrope = rope_tab.shape[1] // tm
    w = RWKV_WIDTH
    nchunk = tm // WKV_CHUNK
    full = lambda a: pl.BlockSpec(a.shape, lambda i: (0,) * a.ndim)
    row_out = lambda n: pl.BlockSpec((tm, n), lambda i: (i, 0))
    out_shape = (
        [jax.ShapeDtypeStruct((r, w), _BF16)] * 5
        + [jax.ShapeDtypeStruct((nt * nchunk, 1, w), _F32)]
        + [jax.ShapeDtypeStruct((r, w), _BF16)] * 2
        + [jax.ShapeDtypeStruct((r, MLA_HEADS * QK_PAD), _BF16)] * 2
        + [jax.ShapeDtypeStruct((r, MLA_WIDTH), _BF16)]
    )
    out_specs = (
        [row_out(w)] * 5
        + [pl.BlockSpec((nchunk, 1, w), lambda i: (i, 0, 0))]
        + [row_out(w)] * 2
        + [row_out(MLA_HEADS * QK_PAD)] * 2
        + [row_out(MLA_WIDTH)]
    )
    return pl.pallas_call(
        _prep_kernel,
        out_shape=out_shape,
        grid=(nt,),
        in_specs=[
            pl.BlockSpec((tm, C_GATE), lambda i: (i, 0)),
            pl.BlockSpec((8, RW_END), lambda i: (jnp.maximum(i * (tm // 8) - 1, 0), 0)),
            pl.BlockSpec((3, tm, LANES), lambda i: (0, i % nrope, 0)),
            full(mu), full(vec), full(rk), full(wa_up), full(g_up), full(head_sum), full(tri),
            full(qg), full(kvg), full(wuq), full(wuk), full(wuv),
        ],
        out_specs=out_specs,
        compiler_params=_cparams("parallel"),
        name="prep",
    )(p, p, rope_tab, mu, vec, rk, wa_up, g_up, head_sum, tri, qg, kvg, wuq, wuk, wuv)


def _wkv_kernel(kt_ref, rt_ref, kd_ref, bd_ref, v_ref, pc_ref, y_ref, h_scr):
    c = pl.program_id(1)
    npair = RWKV_HEADS // 2
    cl = WKV_CHUNK
    n2 = 2 * cl

    @pl.when(c == 0)
    def _():
        h_scr[...] = jnp.zeros_like(h_scr)

    lane = lax.broadcasted_iota(jnp.int32, (1, LANES), 1)
    first = lane < HEAD_DIM
    row = lax.broadcasted_iota(jnp.int32, (n2, n2), 0)
    col = lax.broadcasted_iota(jnp.int32, (n2, n2), 1)
    strict = row > col
    incl = row >= col
    eye = (row == col).astype(_F32)

    def stack(x):
        zero = jnp.zeros_like(x)
        return jnp.concatenate([jnp.where(first, x, zero), jnp.where(first, zero, x)], axis=0)

    def same_block(size):
        sh = size.bit_length() - 1
        return jnp.right_shift(row, sh) == jnp.right_shift(col, sh)

    for j in range(npair):
        sl = slice(j * LANES, (j + 1) * LANES)
        kt2, rt2 = stack(kt_ref[:, sl]), stack(rt_ref[:, sl])
        kd2, bd2, v2 = stack(kd_ref[:, sl]), stack(bd_ref[:, sl]), stack(v_ref[:, sl])
        lhs = jnp.concatenate([kt2, rt2], axis=0)
        rhs = jnp.concatenate([kd2, bd2], axis=0)
        gram = _dot_nt(lhs, rhs)
        a_kk = jnp.where(strict, gram[:n2, :n2], 0.0)
        a_kb = jnp.where(strict, gram[:n2, n2:], 0.0)
        a_rk = jnp.where(incl, gram[n2:, :n2], 0.0)
        a_rb = jnp.where(incl, gram[n2:, n2:], 0.0)

        blk = same_block(8)
        nd = jnp.where(blk, a_kb, 0.0).astype(_BF16)
        s2 = _dot(nd, nd)
        i_minus = eye - nd.astype(_F32)
        s2b = s2.astype(_BF16)
        p1 = i_minus + _dot(i_minus.astype(_BF16), s2b)
        s4 = _dot(s2b, s2b)
        t_inv = p1 + _dot(p1.astype(_BF16), s4.astype(_BF16))
        size = 8
        while size < cl:
            outer = same_block(2 * size)
            off = jnp.where(outer & jnp.logical_not(blk), a_kb, 0.0).astype(_BF16)
            tb = t_inv.astype(_BF16)
            m = _dot(tb, off)
            t_inv = t_inv - _dot(m.astype(_BF16), tb)
            blk = outer
            size *= 2

        hbd = h_scr[j]
        x0 = _dot(lhs, hbd.astype(_BF16))
        wmat = x0[:n2] + _dot(a_kk.astype(_BF16), v2)
        u2 = _dot(t_inv.astype(_BF16), wmat.astype(_BF16)).astype(_BF16)
        vu = jnp.concatenate([v2, u2], axis=0)
        a_out = jnp.concatenate([a_rk, -a_rb], axis=1).astype(_BF16)
        y2 = x0[n2:] + _dot(a_out, vu)
        y_ref[:, sl] = (y2[:cl] + y2[cl:]).astype(y_ref.dtype)

        upd = _dot_tn(jnp.concatenate([kd2, -bd2], axis=0), vu)
        pc_row = pc_ref[0, :, sl]
        pc_col = jnp.transpose(jnp.broadcast_to(pc_row, (LANES, LANES)))
        h_scr[j] = (hbd + upd) * pc_col


def _wkv(kt, rt, kd, bd, v, pc, bsz):
    r, w = kt.shape
    nc = r // bsz // WKV_CHUNK
    blk = pl.BlockSpec((WKV_CHUNK, w), lambda bi, c: (bi * nc + c, 0))
    return pl.pallas_call(
        _wkv_kernel,
        out_shape=jax.ShapeDtypeStruct((r, w), _F32),
        grid=(bsz, nc),
        in_specs=[blk] * 5 + [pl.BlockSpec((1, 1, w), lambda bi, c: (bi * nc + c, 0, 0))],
        out_specs=blk,
        scratch_shapes=[pltpu.VMEM((RWKV_HEADS // 2, LANES, LANES), _F32)],
        compiler_params=_cparams("parallel", "arbitrary"),
        name="wkv7",
    )(kt, rt, kd, bd, v, pc)


def _attn_kernel(q_ref, k_ref, v_ref, o_ref):
    i = pl.program_id(2)
    tq = q_ref.shape[0]
    tk = tq
    q_pos = i * tq + lax.broadcasted_iota(jnp.int32, (tq, 1), 0)
    lane = lax.broadcasted_iota(jnp.int32, (1, LANES), 1)
    qs = (q_ref[:, :QK_PAD], q_ref[:, QK_PAD:])

    def body(jb, carry):
        start = pl.multiple_of(jb * tk, tk)
        kb = k_ref[pl.ds(start, tk), :]
        vb = v_ref[pl.ds(start, tk), :]
        k_pos = start + lax.broadcasted_iota(jnp.int32, (1, tk), 1)
        mask = (k_pos <= q_pos) & (k_pos >= PADF)
        out = []
        for h in range(2):
            m_prev, l_prev, acc = carry[3 * h:3 * h + 3]
            s = _dot_nt(qs[h], kb[:, h * QK_PAD:(h + 1) * QK_PAD])
            s = jnp.where(mask, s, NEG_BIG)
            m_new = jnp.maximum(m_prev, jnp.max(s, axis=-1, keepdims=True))
            alpha = jnp.exp(m_prev - m_new)
            prob = jnp.exp(s - m_new)
            l_new = alpha * l_prev + jnp.sum(prob, axis=-1, keepdims=True)
            acc = alpha * acc + _dot(prob.astype(_BF16), vb)
            out += [m_new, l_new, acc]
        return tuple(out)

    init = (jnp.full((tq, 1), NEG_BIG, _F32), jnp.zeros((tq, 1), _F32),
            jnp.zeros((tq, LANES), _F32)) * 2
    res = lax.fori_loop(0, i + 1, body, init)
    o1 = res[2] / res[1]
    o2 = res[5] / res[4]
    o_ref[...] = jnp.where(lane < V_HEAD_DIM, o1, o2).astype(o_ref.dtype)


def _attention(q, k, v, bsz):
    r = q.shape[0]
    tp = r // bsz
    nq = tp // ATT_TILE
    npair = MLA_HEADS // 2
    return pl.pallas_call(
        _attn_kernel,
        out_shape=jax.ShapeDtypeStruct((r, MLA_WIDTH), _BF16),
        grid=(bsz, npair, nq),
        in_specs=[
            pl.BlockSpec((ATT_TILE, 2 * QK_PAD), lambda bi, pj, i: (bi * nq + i, pj)),
            pl.BlockSpec((tp, 2 * QK_PAD), lambda bi, pj, i: (bi, pj)),
            pl.BlockSpec((tp, 2 * V_HEAD_DIM), lambda bi, pj, i: (bi, pj)),
        ],
        out_specs=pl.BlockSpec((ATT_TILE, 2 * V_HEAD_DIM), lambda bi, pj, i: (bi * nq + i, pj)),
        compiler_params=_cparams("parallel", "parallel", "arbitrary"),
        name="mla_attention",
    )(q, k, v)


def _merge_kernel(y_ref, bonus_ref, g_ref, ymla_ref, gate_ref, h_ref, lnx_ref, head_mean_ref,
                  wpr_ref, wpm_ref, wout_ref, ln_ref, o_ref, *, tp, alpha):
    tm = y_ref.shape[0]
    d = h_ref.shape[1]
    y = y_ref[...].astype(_F32)
    head_mean = head_mean_ref[...]
    ym = _split_dot(y, head_mean, 2)
    yc = y - ym
    yv = _split_dot(yc * yc, head_mean, 2)
    yn = yc * lax.rsqrt(yv + GN_EPS) * lnx_ref[0:1, :] + lnx_ref[1:2, :]
    y_rwkv = (yn.astype(_F32) + bonus_ref[...].astype(_F32)) * g_ref[...].astype(_F32)
    gates = _sigmoid(gate_ref[...].astype(_F32))
    mixed = (gates[:, :d] * _dot(y_rwkv.astype(_BF16), wpr_ref[...])
             + gates[:, d:] * _dot(ymla_ref[...], wpm_ref[...]))
    out = _dot(mixed.astype(_BF16), wout_ref[...])
    hn = _layer_norm(alpha * h_ref[...] + out, ln_ref[0:1, :], ln_ref[1:2, :])
    o_ref[...] = jnp.where(_valid_rows(pl.program_id(0), tm, tp), hn, 0.0)


def _merge(y, bonus, g, ymla, pgate, h, lnx, head_mean, wpr, wpm, wout, ln, tp, alpha):
    r, d = h.shape
    tm = ROW_TM
    w = RWKV_WIDTH
    full = lambda a: pl.BlockSpec(a.shape, lambda i: (0,) * a.ndim)
    rows = lambda n: pl.BlockSpec((tm, n), lambda i: (i, 0))
    return pl.pallas_call(
        functools.partial(_merge_kernel, tp=tp, alpha=alpha),
        out_shape=jax.ShapeDtypeStruct((r, d), _F32),
        grid=(r // tm,),
        in_specs=[rows(w), rows(w), rows(w), rows(MLA_WIDTH),
                  rows(2 * d),
                  rows(d), full(lnx), full(head_mean), full(wpr), full(wpm), full(wout), full(ln)],
        out_specs=rows(d),
        compiler_params=_cparams("parallel"),
        name="merge",
    )(y, bonus, g, ymla, pgate, h, lnx, head_mean, wpr, wpm, wout, ln)


def _ffn_kernel(h_ref, wup_ref, cw_ref, cb_ref, wdown_ref, ln_ref, o_ref, carry_ref, *, tp, alpha):
    i = pl.program_id(0)
    tm, d = h_ref.shape
    dff = wdown_ref.shape[0]

    @pl.when(i == 0)
    def _():
        carry_ref[...] = jnp.zeros_like(carry_ref)

    h = h_ref[...]
    hb = h.astype(_BF16)
    row = lax.broadcasted_iota(jnp.int32, (tm, 1), 0)

    def conv(u, c0):
        cs = slice(c0, c0 + FF_CHUNK)
        prev1 = carry_ref[7:8, cs]
        prev2 = carry_ref[6:7, cs]
        u1 = jnp.where(row == 0, prev1, pltpu.roll(u, 1, axis=0))
        u2 = jnp.where(row == 0, prev2, jnp.where(row == 1, prev1, pltpu.roll(u, 2, axis=0)))
        carry_ref[:, cs] = u[tm - 8:, :]
        return (cw_ref[2:3, cs] * u + cw_ref[1:2, cs] * u1 + cw_ref[0:1, cs] * u2 + cb_ref[:, cs])

    acc = jnp.zeros((tm, d), _F32)
    for c0 in range(0, dff, FF_CHUNK):
        gate = conv(_dot(hb, wup_ref[:, c0:c0 + FF_CHUNK]), c0)
        val = conv(_dot(hb, wup_ref[:, dff + c0:dff + c0 + FF_CHUNK]), dff + c0)
        act = gate * _sigmoid(gate) * val
        acc = acc + _dot(act.astype(_BF16), wdown_ref[c0:c0 + FF_CHUNK, :])
    hn = _layer_norm(alpha * h + acc, ln_ref[0:1, :], ln_ref[1:2, :])
    o_ref[...] = jnp.where(_valid_rows(i, tm, tp), hn, 0.0)


def _ffn(h, wup, cw, cb, wdown, ln, tp, alpha):
    r, d = h.shape
    tm = ROW_TM
    full = lambda a: pl.BlockSpec(a.shape, lambda i: (0,) * a.ndim)
    return pl.pallas_call(
        functools.partial(_ffn_kernel, tp=tp, alpha=alpha),
        out_shape=jax.ShapeDtypeStruct((r, d), _F32),
        grid=(r // tm,),
        in_specs=[pl.BlockSpec((tm, d), lambda i: (i, 0)),
                  full(wup), full(cw), full(cb), full(wdown), full(ln)],
        out_specs=pl.BlockSpec((tm, d), lambda i: (i, 0)),
        scratch_shapes=[pltpu.VMEM((8, wup.shape[1]), _F32)],
        compiler_params=_cparams("arbitrary"),
        name="conv_ffn",
    )(h, wup, cw, cb, wdown, ln)


def _pad_cols(a, n):
    return jnp.pad(a, ((0, 0), (0, n - a.shape[1])))


def _rearrange_in_cols(a):
    w = RWKV_WIDTH
    o_mla = 3 * w + DECAY_LORA + ICLR_LORA + GATE_LORA
    o_gate = o_mla + Q_LORA_RANK + KV_LORA_RANK + QK_ROPE_DIM
    rows = a.shape[0]
    z = lambda n: jnp.zeros((rows, n), a.dtype)
    kr = a[:, o_mla + Q_LORA_RANK + KV_LORA_RANK:o_gate]
    parts = [
        a[:, :3 * w + DECAY_LORA + ICLR_LORA],
        a[:, 3 * w + DECAY_LORA + ICLR_LORA:o_mla], z(GATE_LORA_PAD - GATE_LORA),
        a[:, o_mla:o_mla + Q_LORA_RANK + KV_LORA_RANK],
        z(QK_NOPE_DIM), kr, z(LANES - QK_NOPE_DIM - QK_ROPE_DIM),
        a[:, o_gate:],
    ]
    return jnp.concatenate(parts, axis=1)


def _pad_heads(wmat, per_head, lead):
    kdim = wmat.shape[0]
    wh = wmat.reshape(kdim, MLA_HEADS, per_head)
    wh = jnp.pad(wh, ((0, 0), (0, 0), (lead, QK_PAD - per_head - lead)))
    return wh.reshape(kdim, MLA_HEADS * QK_PAD)


def _rope_tables(tp, reps):
    half = QK_ROPE_DIM // 2
    pos = (jnp.arange(tp, dtype=jnp.int32) - PADF).astype(_F32)
    inv_freq = ROPE_THETA ** (-jnp.arange(half, dtype=_F32) / half)
    ang = pos[:, None] * inv_freq[None, :]
    cos, sin = jnp.cos(ang), jnp.sin(ang)
    zeros = lambda n: jnp.zeros((tp, n), _F32)
    tail = LANES - QK_NOPE_DIM - QK_ROPE_DIM
    cos_t = jnp.concatenate([jnp.ones((tp, QK_NOPE_DIM), _F32), cos, cos, zeros(tail)], axis=1)
    sin_a = jnp.concatenate([zeros(QK_NOPE_DIM), -sin, zeros(half + tail)], axis=1)
    sin_b = jnp.concatenate([zeros(QK_NOPE_DIM + half), sin, zeros(tail)], axis=1)
    tab = jnp.stack([cos_t, sin_a, sin_b])
    return jnp.tile(tab, (1, reps, 1))


def kernel(x, meta_tokens, ln_in_g, ln_in_b, w_in, mu_shift, w0, w_lora_up, a0, a_lora_up,
           g_lora_up, k_k, k_a, r_k, lnx_g, lnx_b, q_norm_g, w_uq, kv_norm_g, w_uk, w_uv,
           w_proj_rwkv, w_proj_mla, w_out, ln1_g, ln1_b, w_up, conv_w, conv_b, w_down,
           ln2_g, ln2_b):
    bsz, seq, d = x.shape
    depth = w_in.shape[0]
    tp = seq + FRONT
    r = bsz * tp
    assert tp % ATT_TILE == 0 and r % ROW_TM == 0 and r % PREP_TM == 0
    alpha = float((2 * depth) ** 0.25)
    w = RWKV_WIDTH

    meta_pad = jnp.pad(meta_tokens.astype(x.dtype), ((PADF, 0), (0, 0)))
    h = _embed(x, meta_pad, ln_in_g[None], ln_in_b[None]).reshape(r, d)

    hid = jnp.arange(w) // HEAD_DIM
    head_sum = (hid[:, None] == hid[None, :]).astype(_BF16)
    head_mean = head_sum * (1.0 / HEAD_DIM)
    tt = jnp.arange(PREP_TM)
    tri = ((tt[:, None] >= tt[None, :])
           & (tt[:, None] // WKV_CHUNK == tt[None, :] // WKV_CHUNK)).astype(_BF16)
    reps = 1
    while (reps * tp) % PREP_TM:
        reps += 1
    rope_tab = _rope_tables(tp, reps)

    for l in range(depth):
        w_in_l = _rearrange_in_cols(w_in[l]).astype(_BF16)
        mu = _rearrange_in_cols(jnp.pad(mu_shift[l][None], ((0, 0), (0, w_in.shape[2] - mu_shift.shape[1]))))
        mu = mu[:, :RW_END]
        vec = jnp.stack([w0[l], a0[l], k_k[l], k_a[l]])
        wa_up = jnp.concatenate([
            jnp.concatenate([w_lora_up[l], jnp.zeros((DECAY_LORA, w), _F32)], axis=1),
            jnp.concatenate([jnp.zeros((ICLR_LORA, w), _F32), a_lora_up[l]], axis=1)], axis=0)
        g_up = jnp.pad(g_lora_up[l], ((0, GATE_LORA_PAD - GATE_LORA), (0, 0)))
        wuq = _pad_heads(w_uq[l], QK_NOPE_DIM + QK_ROPE_DIM, 0)
        wuk = _pad_heads(w_uk[l], QK_NOPE_DIM, 0)

        p, pgate = _inproj(h, w_in_l)
        kt, rt, kd, bd, v, pc, bonus, g, q, kk, vv = _prep(
            p, rope_tab, mu, vec, r_k[l].reshape(1, w), wa_up.astype(_BF16), g_up.astype(_BF16),
            head_sum, tri, q_norm_g[l][None], kv_norm_g[l][None],
            wuq.astype(_BF16), wuk.astype(_BF16), w_uv[l].astype(_BF16))
        y = _wkv(kt, rt, kd, bd, v, pc, bsz)
        ymla = _attention(q, kk, vv, bsz)
        h = _merge(y, bonus, g, ymla, pgate, h, jnp.stack([lnx_g[l], lnx_b[l]]), head_mean,
                   w_proj_rwkv[l].astype(_BF16), w_proj_mla[l].astype(_BF16),
                   w_out[l].astype(_BF16), jnp.stack([ln1_g[l], ln1_b[l]]), tp, alpha)
        h = _ffn(h, w_up[l].astype(_BF16), conv_w[l], conv_b[l][None], w_down[l].astype(_BF16),
                 jnp.stack([ln2_g[l], ln2_b[l]]), tp, alpha)
    return h.reshape(bsz, tp, d)[:, FRONT:]
```

```python
import functools

import numpy as np
import jax
import jax.numpy as jnp
from jax import lax
from jax.experimental import pallas as pl
from jax.experimental.pallas import tpu as pltpu

N_META = 16
HEAD_DIM = 64
RWKV_HEADS = 8
RWKV_WIDTH = RWKV_HEADS * HEAD_DIM
DECAY_LORA = 64
ICLR_LORA = 64
GATE_LORA = 160
GN_EPS = 64e-5
MLA_HEADS = 8
QK_NOPE_DIM = 64
QK_ROPE_DIM = 32
V_HEAD_DIM = 64
Q_LORA_RANK = 256
KV_LORA_RANK = 256
ROPE_THETA = 10000.0
MLA_WIDTH = MLA_HEADS * V_HEAD_DIM
CONV_WIDTH = 3
LN_EPS = 1e-5
RMS_EPS = 1e-6

LANES = 128
MXU_DIM = 256
VMEM_LIMIT = 56 * 1024 * 1024

FRONT = MXU_DIM
PADF = FRONT - N_META
C_R, C_K, C_V = 0, RWKV_WIDTH, 2 * RWKV_WIDTH
C_WA = 3 * RWKV_WIDTH
GATE_LORA_PAD = 2 * LANES
C_G = C_WA + LANES
RW_END = C_G + GATE_LORA_PAD
C_CQ = RW_END
C_CKV = C_CQ + Q_LORA_RANK
C_KR = C_CKV + KV_LORA_RANK
C_GATE = C_KR + LANES
QK_PAD = LANES

WKV_CHUNK = 64
WKV_BATCH = 2
ATT_TILE = MXU_DIM
ATT_PAIRS = 4
PREP_TM = 256
ROW_TM = 512
FF_CHUNK = 256
NEG_BIG = -1e30

_F32 = jnp.float32
_BF16 = jnp.bfloat16


def _cparams(*sem):
    return pltpu.CompilerParams(dimension_semantics=sem, vmem_limit_bytes=VMEM_LIMIT)


def _dot(a, b):
    return jnp.dot(a, b, preferred_element_type=_F32)


def _dot_nt(a, b):
    return lax.dot_general(a, b, (((1,), (1,)), ((), ())), preferred_element_type=_F32)


def _dot_tn(a, b):
    return lax.dot_general(a, b, (((0,), (0,)), ((), ())), preferred_element_type=_F32)


def _split_dot(x, w_bf16, parts, left=False):
    acc = None
    rem = x
    for _ in range(parts):
        hi = rem.astype(_BF16)
        term = _dot(w_bf16, hi) if left else _dot(hi, w_bf16)
        acc = term if acc is None else acc + term
        rem = rem - hi.astype(_F32)
    return acc


def _layer_norm(x, g, b):
    mu = jnp.mean(x, axis=-1, keepdims=True)
    xc = x - mu
    var = jnp.mean(xc * xc, axis=-1, keepdims=True)
    return xc * lax.rsqrt(var + LN_EPS) * g + b


def _sigmoid(x):
    return 1.0 / (1.0 + jnp.exp(-x))


def _valid_rows(tile_idx, tm, tp):
    row = tile_idx * tm + lax.broadcasted_iota(jnp.int32, (tm, 1), 0)
    bidx = jnp.floor((row.astype(_F32) + 0.5) * (1.0 / tp)).astype(jnp.int32)
    return (row - bidx * tp) >= PADF


def _embed_kernel(x_ref, meta_ref, g_ref, b_ref, o_ref):
    t = pl.program_id(1)
    src = jnp.where(t == 0, meta_ref[...], x_ref[0])
    y = _layer_norm(src, g_ref[...], b_ref[...])
    row = lax.broadcasted_iota(jnp.int32, (FRONT, 1), 0)
    o_ref[0] = jnp.where((t > 0) | (row >= PADF), y, 0.0)


def _embed(x, meta_pad, g, b):
    bsz, seq, d = x.shape
    nt = seq // FRONT + 1
    return pl.pallas_call(
        _embed_kernel,
        out_shape=jax.ShapeDtypeStruct((bsz, nt * FRONT, d), _F32),
        grid=(bsz, nt),
        in_specs=[
            pl.BlockSpec((1, FRONT, d), lambda bi, t: (bi, jnp.maximum(t - 1, 0), 0)),
            pl.BlockSpec((FRONT, d), lambda bi, t: (0, 0)),
            pl.BlockSpec((1, d), lambda bi, t: (0, 0)),
            pl.BlockSpec((1, d), lambda bi, t: (0, 0)),
        ],
        out_specs=pl.BlockSpec((1, FRONT, d), lambda bi, t: (bi, t, 0)),
        compiler_params=_cparams("parallel", "arbitrary"),
        name="embed_ln",
    )(x, meta_pad, g, b)


def _inproj_kernel(h_ref, w_ref, mix_ref, gate_ref, *, n_chunk):
    hb = h_ref[...].astype(_BF16)
    for o_ref, base in ((mix_ref, 0), (gate_ref, C_GATE)):
        for c0 in range(0, o_ref.shape[1], n_chunk):
            o_ref[:, c0:c0 + n_chunk] = _dot(
                hb, w_ref[:, base + c0:base + c0 + n_chunk]).astype(o_ref.dtype)


def _inproj(h, w):
    r, d = h.shape
    n = w.shape[1]
    return pl.pallas_call(
        functools.partial(_inproj_kernel, n_chunk=2 * MXU_DIM),
        out_shape=[jax.ShapeDtypeStruct((r, C_GATE), _BF16),
                   jax.ShapeDtypeStruct((r, n - C_GATE), _BF16)],
        grid=(r // ROW_TM,),
        in_specs=[
            pl.BlockSpec((ROW_TM, d), lambda i: (i, 0)),
            pl.BlockSpec((d, n), lambda i: (0, 0)),
        ],
        out_specs=[pl.BlockSpec((ROW_TM, C_GATE), lambda i: (i, 0)),
                   pl.BlockSpec((ROW_TM, n - C_GATE), lambda i: (i, 0))],
        compiler_params=_cparams("parallel"),
        name="in_proj",
    )(h, w)


def _prep_kernel(p_ref, prev_ref, rope_ref, mu_ref, vec_ref, rk_ref, wa_up_ref, g_up_ref,
                 head_sum_ref, tri_ref, qg_ref, kvg_ref, wuq_ref, wuk_ref, wuv_ref,
                 kt_ref, rt_ref, kd_ref, bd_ref, v_ref, pc_ref, bonus_ref, g_ref,
                 q_ref, kk_ref, vv_ref):
    tm = p_ref.shape[0]
    w = RWKV_WIDTH

    pr = p_ref[:, :RW_END].astype(_F32)
    shifted = pltpu.roll(pr, 1, axis=0)
    row = lax.broadcasted_iota(jnp.int32, (tm, 1), 0)
    prev_last = prev_ref[7:8, :].astype(_F32)
    shifted = jnp.where(row == 0, prev_last, shifted)
    ps = pr + (shifted - pr) * mu_ref[...]

    r = ps[:, C_R:C_R + w]
    k = ps[:, C_K:C_K + w]
    v = ps[:, C_V:C_V + w]
    wa_in = ps[:, C_WA:C_WA + LANES]
    lane = lax.broadcasted_iota(jnp.int32, (1, LANES), 1)
    wa_in = jnp.where(lane < DECAY_LORA, jnp.tanh(wa_in), wa_in)
    wa = _dot(wa_in.astype(_BF16), wa_up_ref[...])
    w0, a0, k_k, k_a = (vec_ref[0:1, :], vec_ref[1:2, :], vec_ref[2:3, :], vec_ref[3:4, :])

    z = -(w0 + wa[:, :w])
    softplus = jnp.maximum(z, 0.0) + jnp.log1p(jnp.exp(-jnp.abs(z)))
    logw = -jnp.exp(-softplus - 0.5)
    a = _sigmoid(a0 + wa[:, w:])
    gate = _dot(_sigmoid(ps[:, C_G:C_G + GATE_LORA_PAD]).astype(_BF16), g_up_ref[...])

    head_sum = head_sum_ref[...]
    kk = k * k_k
    kk_ss = _split_dot(kk * kk, head_sum, 2)
    kk = kk / jnp.maximum(jnp.sqrt(kk_ss), 1e-12)
    k = k * (1.0 + (a - 1.0) * k_a)
    b = kk * a
    bonus = _split_dot(r * k * rk_ref[...], head_sum, 2) * v

    cum = _split_dot(logw, tri_ref[...], 3, left=True)
    e_neg = jnp.exp(-cum)
    kt_ref[...] = (kk * jnp.exp(cum - logw)).astype(_BF16)
    rt_ref[...] = (r * jnp.exp(cum)).astype(_BF16)
    kd_ref[...] = (k * e_neg).astype(_BF16)
    bd_ref[...] = (b * e_neg).astype(_BF16)
    v_ref[...] = v.astype(_BF16)
    bonus_ref[...] = bonus.astype(_BF16)
    g_ref[...] = gate.astype(_BF16)
    for ci in range(tm // WKV_CHUNK):
        last = (ci + 1) * WKV_CHUNK - 1
        pc_ref[ci] = jnp.exp(cum[last:last + 1, :])

    cos_t, sin_a, sin_b = rope_ref[0], rope_ref[1], rope_ref[2]

    def rope(x):
        nrep = x.shape[1] // LANES
        n = x.shape[1]
        c, sa, sb = (jnp.tile(t, (1, nrep)) if nrep > 1 else t for t in (cos_t, sin_a, sin_b))
        half = QK_ROPE_DIM // 2
        return x * c + pltpu.roll(x, n - half, axis=1) * sa + pltpu.roll(x, half, axis=1) * sb

    def rms(x, gain):
        return x * lax.rsqrt(jnp.mean(x * x, axis=-1, keepdims=True) + RMS_EPS) * gain

    cq = p_ref[:, C_CQ:C_CQ + Q_LORA_RANK].astype(_F32)
    ckv = p_ref[:, C_CKV:C_CKV + KV_LORA_RANK].astype(_F32)
    kr = p_ref[:, C_KR:C_KR + LANES].astype(_F32)
    qn = rms(cq, qg_ref[...]).astype(_BF16)
    scale = float(np.log2(np.e) / np.sqrt(QK_NOPE_DIM + QK_ROPE_DIM))
    q_ref[...] = (rope(_dot(qn, wuq_ref[...])) * scale).astype(_BF16)
    kvn = rms(ckv, kvg_ref[...]).astype(_BF16)
    kk_ref[...] = (_dot(kvn, wuk_ref[...]) + jnp.tile(rope(kr), (1, MLA_HEADS))).astype(_BF16)
    vv_ref[...] = _dot_nt(wuv_ref[...], kvn).astype(_BF16)


def _prep(p, rope_tab, mu, vec, rk, wa_up, g_up, head_sum, tri, qg, kvg, wuq, wuk, wuv):
    r = p.shape[0]
    tm = PREP_TM
    nt = r // tm
    nrope = rope_tab.shape[1] // tm
    w = RWKV_WIDTH
    nchunk = tm // WKV_CHUNK
    full = lambda a: pl.BlockSpec(a.shape, lambda i: (0,) * a.ndim)
    row_out = lambda n: pl.BlockSpec((tm, n), lambda i: (i, 0))
    out_shape = (
        [jax.ShapeDtypeStruct((r, w), _BF16)] * 5
        + [jax.ShapeDtypeStruct((nt * nchunk, 1, w), _F32)]
        + [jax.ShapeDtypeStruct((r, w), _BF16)] * 2
        + [jax.ShapeDtypeStruct((r, MLA_HEADS * QK_PAD), _BF16)] * 2
        + [jax.ShapeDtypeStruct((MLA_WIDTH, r), _BF16)]
    )
    out_specs = (
        [row_out(w)] * 5
        + [pl.BlockSpec((nchunk, 1, w), lambda i: (i, 0, 0))]
        + [row_out(w)] * 2
        + [row_out(MLA_HEADS * QK_PAD)] * 2
        + [pl.BlockSpec((MLA_WIDTH, tm), lambda i: (0, i))]
    )
    return pl.pallas_call(
        _prep_kernel,
        out_shape=out_shape,
        grid=(nt,),
        in_specs=[
            pl.BlockSpec((tm, C_GATE), lambda i: (i, 0)),
            pl.BlockSpec((8, RW_END), lambda i: (jnp.maximum(i * (tm // 8) - 1, 0), 0)),
            pl.BlockSpec((3, tm, LANES), lambda i: (0, i % nrope, 0)),
            full(mu), full(vec), full(rk), full(wa_up), full(g_up), full(head_sum), full(tri),
            full(qg), full(kvg), full(wuq), full(wuk), full(wuv),
        ],
        out_specs=out_specs,
        compiler_params=_cparams("parallel"),
        name="prep",
    )(p, p, rope_tab, mu, vec, rk, wa_up, g_up, head_sum, tri, qg, kvg, wuq, wuk, wuv)


def _wkv_kernel(kt_ref, rt_ref, kd_ref, bd_ref, v_ref, pc_ref, y_ref, h_scr):
    c = pl.program_id(1)

    @pl.when(c < PADF // WKV_CHUNK)
    def _():
        h_scr[...] = jnp.zeros_like(h_scr)
        y_ref[...] = jnp.zeros_like(y_ref)

    @pl.when(c >= PADF // WKV_CHUNK)
    def _():
        _wkv_chunk(kt_ref, rt_ref, kd_ref, bd_ref, v_ref, pc_ref, y_ref, h_scr)


def _wkv_chunk(kt_ref, rt_ref, kd_ref, bd_ref, v_ref, pc_ref, y_ref, h_scr):
    npair = RWKV_HEADS // 2
    cl = WKV_CHUNK
    n2 = 2 * cl
    lane = lax.broadcasted_iota(jnp.int32, (1, LANES), 1)
    first = lane < HEAD_DIM
    row = lax.broadcasted_iota(jnp.int32, (n2, n2), 0)
    col = lax.broadcasted_iota(jnp.int32, (n2, n2), 1)
    strict = row > col
    incl = row >= col
    eye = (row == col).astype(_F32)

    def stack(x):
        zero = jnp.zeros_like(x)
        return jnp.concatenate([jnp.where(first, x, zero), jnp.where(first, zero, x)], axis=0)

    def same_block(size):
        sh = size.bit_length() - 1
        return jnp.right_shift(row, sh) == jnp.right_shift(col, sh)

    units = [(bl, j) for bl in range(kt_ref.shape[0]) for j in range(npair)]
    each = lambda fn, *lists: [fn(*args) for args in zip(*lists)]
    bf = lambda x: x.astype(_BF16)

    def load(ref):
        return [stack(ref[bl, :, j * LANES:(j + 1) * LANES]) for bl, j in units]

    kt2, rt2, kd2, bd2, v2 = load(kt_ref), load(rt_ref), load(kd_ref), load(bd_ref), load(v_ref)
    lhs = each(lambda a, b: jnp.concatenate([a, b], axis=0), kt2, rt2)
    rhs = each(lambda a, b: jnp.concatenate([a, b], axis=0), kd2, bd2)
    gram = each(_dot_nt, lhs, rhs)
    a_kk = [bf(jnp.where(strict, g[:n2, :n2], 0.0)) for g in gram]
    a_kb = [jnp.where(strict, g[:n2, n2:], 0.0) for g in gram]
    a_out = [bf(jnp.concatenate([jnp.where(incl, g[n2:, :n2], 0.0),
                                 jnp.where(incl, -g[n2:, n2:], 0.0)], axis=1)) for g in gram]

    blk = same_block(8)
    nd = [bf(jnp.where(blk, a, 0.0)) for a in a_kb]
    s2 = [bf(x) for x in each(_dot, nd, nd)]
    i_minus = [eye - x.astype(_F32) for x in nd]
    p1 = each(lambda im, s: im + _dot(bf(im), s), i_minus, s2)
    s4 = [bf(x) for x in each(_dot, s2, s2)]
    t_inv = each(lambda p, s: p + _dot(bf(p), s), p1, s4)
    size = 8
    while size < cl:
        outer = same_block(2 * size)
        sel = outer & jnp.logical_not(blk)
        off = [bf(jnp.where(sel, a, 0.0)) for a in a_kb]
        tb = [bf(t) for t in t_inv]
        m = [bf(x) for x in each(_dot, tb, off)]
        t_inv = each(lambda t, mm, tbb: t - _dot(mm, tbb), t_inv, m, tb)
        blk = outer
        size *= 2
    tb = [bf(t) for t in t_inv]
    akv = each(_dot, a_kk, v2)

    hbd = [h_scr[i] for i in range(len(units))]
    x0 = each(lambda l, h: _dot(l, bf(h)), lhs, hbd)
    u2 = each(lambda t, x, w: bf(_dot(t, bf(x[:n2] + w))), tb, x0, akv)
    vu = each(lambda a, b: jnp.concatenate([a, b], axis=0), v2, u2)
    y2 = each(lambda x, a, z: x[n2:] + _dot(a, z), x0, a_out, vu)
    upd = each(lambda kd, bd, z: _dot_tn(jnp.concatenate([kd, -bd], axis=0), z), kd2, bd2, vu)
    for i, (bl, j) in enumerate(units):
        sl = slice(j * LANES, (j + 1) * LANES)
        y_ref[bl, :, sl] = (y2[i][:cl] + y2[i][cl:]).astype(y_ref.dtype)
        pc_row = pc_ref[bl, 0, :, sl]
        pc_col = jnp.transpose(jnp.broadcast_to(pc_row, (LANES, LANES)))
        h_scr[i] = (hbd[i] + upd[i]) * pc_col


def _wkv(kt, rt, kd, bd, v, pc, bsz):
    r, w = kt.shape
    tp = r // bsz
    nc = tp // WKV_CHUNK
    nb = WKV_BATCH
    blk = pl.BlockSpec((nb, WKV_CHUNK, w), lambda bg, c: (bg, c, 0))
    args = [a.reshape(bsz, tp, w) for a in (kt, rt, kd, bd, v)] + [pc.reshape(bsz, nc, 1, w)]
    y = pl.pallas_call(
        _wkv_kernel,
        out_shape=jax.ShapeDtypeStruct((bsz, tp, w), _F32),
        grid=(bsz // nb, nc),
        in_specs=[blk] * 5 + [pl.BlockSpec((nb, 1, 1, w), lambda bg, c: (bg, c, 0, 0))],
        out_specs=blk,
        scratch_shapes=[pltpu.VMEM((nb * (RWKV_HEADS // 2), LANES, LANES), _F32)],
        compiler_params=_cparams("parallel", "arbitrary"),
        name="wkv7",
    )(*args)
    return y.reshape(r, w)


def _attn_kernel(q_ref, k_ref, v_ref, o_ref):
    i = pl.program_id(2)
    tq = q_ref.shape[0]
    tk = tq
    npair = q_ref.shape[1] // (2 * QK_PAD)
    q_pos = i * tq + lax.broadcasted_iota(jnp.int32, (1, tq), 1)
    lane2 = lax.broadcasted_iota(jnp.int32, (1, 2 * QK_PAD), 1)
    first_k = lane2 < QK_PAD
    first_v = lax.broadcasted_iota(jnp.int32, (LANES, 1), 0) < V_HEAD_DIM
    qs = [q_ref[:, g * 2 * QK_PAD:(g + 1) * 2 * QK_PAD] for g in range(npair)]

    def pair_rows(a, b):
        return jnp.where(first_v, a, b)

    def body(jb, carry, masked):
        start = pl.multiple_of(jb * tk, tk)
        if masked:
            k_pos = start + lax.broadcasted_iota(jnp.int32, (tk, 1), 0)
            mask = (k_pos <= q_pos) & (k_pos >= PADF)
        scores, vbd = [], []
        for g in range(npair):
            kb = k_ref[pl.ds(start, tk), g * 2 * QK_PAD:(g + 1) * 2 * QK_PAD]
            vt = v_ref[g * LANES:(g + 1) * LANES, pl.ds(start, tk)]
            zk, zv = jnp.zeros_like(kb), jnp.zeros_like(vt)
            kbd = jnp.concatenate([jnp.where(first_k, kb, zk), jnp.where(first_k, zk, kb)], axis=0)
            vbd.append(jnp.concatenate([jnp.where(first_v, vt, zv), jnp.where(first_v, zv, vt)],
                                       axis=1))
            scores.append(_dot_nt(kbd, qs[g]))
        out = []
        for g in range(npair):
            m_prev, l_prev, acc = carry[5 * g:5 * g + 2], carry[5 * g + 2:5 * g + 4], carry[5 * g + 4]
            probs, m_news, l_news, alphas = [], [], [], []
            for h in range(2):
                s = scores[g][h * tk:(h + 1) * tk, :]
                if masked:
                    s = jnp.where(mask, s, NEG_BIG)
                m_new = jnp.maximum(m_prev[h], jnp.max(s, axis=0, keepdims=True))
                alpha = jnp.exp2(m_prev[h] - m_new)
                prob = jnp.exp2(s - m_new)
                l_news.append(alpha * l_prev[h] + jnp.sum(prob, axis=0, keepdims=True))
                probs.append(prob.astype(_BF16))
                m_news.append(m_new)
                alphas.append(alpha)
            pv = _dot(vbd[g], jnp.concatenate(probs, axis=0))
            acc = pair_rows(alphas[0], alphas[1]) * acc + pv
            out += m_news + l_news + [acc]
        return tuple(out)

    init = ((jnp.full((1, tq), NEG_BIG, _F32),) * 2 + (jnp.zeros((1, tq), _F32),) * 2
            + (jnp.zeros((LANES, tq), _F32),)) * npair
    carry = body(0, init, True)
    carry = lax.fori_loop(1, i, lambda jb, c: body(jb, c, False), carry)
    carry = lax.fori_loop(jnp.maximum(i, 1), i + 1, lambda jb, c: body(jb, c, True), carry)
    for g in range(npair):
        o = carry[5 * g + 4] / pair_rows(carry[5 * g + 2], carry[5 * g + 3])
        o_ref[:, g * LANES:(g + 1) * LANES] = jnp.transpose(o).astype(o_ref.dtype)


def _attention(q, k, v, bsz):
    r = q.shape[0]
    tp = r // bsz
    nq = tp // ATT_TILE
    ng = MLA_HEADS // 2 // ATT_PAIRS
    qw, vw = ATT_PAIRS * 2 * QK_PAD, ATT_PAIRS * 2 * V_HEAD_DIM
    return pl.pallas_call(
        _attn_kernel,
        out_shape=jax.ShapeDtypeStruct((r, MLA_WIDTH), _BF16),
        grid=(bsz, ng, nq),
        in_specs=[
            pl.BlockSpec((ATT_TILE, qw), lambda bi, pj, i: (bi * nq + i, pj)),
            pl.BlockSpec((tp, qw), lambda bi, pj, i: (bi, pj)),
            pl.BlockSpec((vw, tp), lambda bi, pj, i: (pj, bi)),
        ],
        out_specs=pl.BlockSpec((ATT_TILE, vw), lambda bi, pj, i: (bi * nq + i, pj)),
        compiler_params=_cparams("parallel", "parallel", "arbitrary"),
        name="mla_attention",
    )(q, k, v)


def _merge_kernel(y_ref, bonus_ref, g_ref, ymla_ref, gate_ref, h_ref, lnx_ref, head_mean_ref,
                  wpr_ref, wpm_ref, wout_ref, ln_ref, o_ref, *, tp, alpha):
    tm = y_ref.shape[0]
    d = h_ref.shape[1]
    y = y_ref[...].astype(_F32)
    head_mean = head_mean_ref[...]
    ym = _split_dot(y, head_mean, 2)
    yc = y - ym
    yv = _split_dot(yc * yc, head_mean, 2)
    yn = yc * lax.rsqrt(yv + GN_EPS) * lnx_ref[0:1, :] + lnx_ref[1:2, :]
    y_rwkv = (yn.astype(_F32) + bonus_ref[...].astype(_F32)) * g_ref[...].astype(_F32)
    gates = _sigmoid(gate_ref[...].astype(_F32))
    mixed = (gates[:, :d] * _dot(y_rwkv.astype(_BF16), wpr_ref[...])
             + gates[:, d:] * _dot(ymla_ref[...], wpm_ref[...]))
    out = _dot(mixed.astype(_BF16), wout_ref[...])
    hn = _layer_norm(alpha * h_ref[...] + out, ln_ref[0:1, :], ln_ref[1:2, :])
    o_ref[...] = jnp.where(_valid_rows(pl.program_id(0), tm, tp), hn, 0.0)


def _merge(y, bonus, g, ymla, pgate, h, lnx, head_mean, wpr, wpm, wout, ln, tp, alpha):
    r, d = h.shape
    tm = ROW_TM
    w = RWKV_WIDTH
    full = lambda a: pl.BlockSpec(a.shape, lambda i: (0,) * a.ndim)
    rows = lambda n: pl.BlockSpec((tm, n), lambda i: (i, 0))
    return pl.pallas_call(
        functools.partial(_merge_kernel, tp=tp, alpha=alpha),
        out_shape=jax.ShapeDtypeStruct((r, d), _F32),
        grid=(r // tm,),
        in_specs=[rows(w), rows(w), rows(w), rows(MLA_WIDTH),
                  rows(2 * d),
                  rows(d), full(lnx), full(head_mean), full(wpr), full(wpm), full(wout), full(ln)],
        out_specs=rows(d),
        compiler_params=_cparams("parallel"),
        name="merge",
    )(y, bonus, g, ymla, pgate, h, lnx, head_mean, wpr, wpm, wout, ln)


def _ffn_kernel(h_ref, wup_ref, cw_ref, cb_ref, wdown_ref, ln_ref, o_ref, carry_ref, *, tp, alpha):
    i = pl.program_id(0)
    tm, d = h_ref.shape
    dff = wdown_ref.shape[0]

    @pl.when(i == 0)
    def _():
        carry_ref[...] = jnp.zeros_like(carry_ref)

    h = h_ref[...]
    hb = h.astype(_BF16)
    row = lax.broadcasted_iota(jnp.int32, (tm, 1), 0)

    def conv(u, c0):
        cs = slice(c0, c0 + FF_CHUNK)
        prev1 = carry_ref[7:8, cs]
        prev2 = carry_ref[6:7, cs]
        u1 = jnp.where(row == 0, prev1, pltpu.roll(u, 1, axis=0))
        u2 = jnp.where(row == 0, prev2, jnp.where(row == 1, prev1, pltpu.roll(u, 2, axis=0)))
        carry_ref[:, cs] = u[tm - 8:, :]
        return (cw_ref[2:3, cs] * u + cw_ref[1:2, cs] * u1 + cw_ref[0:1, cs] * u2 + cb_ref[:, cs])

    acc = jnp.zeros((tm, d), _F32)
    for c0 in range(0, dff, FF_CHUNK):
        gate = conv(_dot(hb, wup_ref[:, c0:c0 + FF_CHUNK]), c0)
        val = conv(_dot(hb, wup_ref[:, dff + c0:dff + c0 + FF_CHUNK]), dff + c0)
        act = gate * _sigmoid(gate) * val
        acc = acc + _dot(act.astype(_BF16), wdown_ref[c0:c0 + FF_CHUNK, :])
    hn = _layer_norm(alpha * h + acc, ln_ref[0:1, :], ln_ref[1:2, :])
    o_ref[...] = jnp.where(_valid_rows(i, tm, tp), hn, 0.0)


def _ffn(h, wup, cw, cb, wdown, ln, tp, alpha):
    r, d = h.shape
    tm = ROW_TM
    full = lambda a: pl.BlockSpec(a.shape, lambda i: (0,) * a.ndim)
    return pl.pallas_call(
        functools.partial(_ffn_kernel, tp=tp, alpha=alpha),
        out_shape=jax.ShapeDtypeStruct((r, d), _F32),
        grid=(r // tm,),
        in_specs=[pl.BlockSpec((tm, d), lambda i: (i, 0)),
                  full(wup), full(cw), full(cb), full(wdown), full(ln)],
        out_specs=pl.BlockSpec((tm, d), lambda i: (i, 0)),
        scratch_shapes=[pltpu.VMEM((8, wup.shape[1]), _F32)],
        compiler_params=_cparams("arbitrary"),
        name="conv_ffn",
    )(h, wup, cw, cb, wdown, ln)


def _pad_cols(a, n):
    return jnp.pad(a, ((0, 0), (0, n - a.shape[1])))


def _rearrange_in_cols(a):
    w = RWKV_WIDTH
    o_mla = 3 * w + DECAY_LORA + ICLR_LORA + GATE_LORA
    o_gate = o_mla + Q_LORA_RANK + KV_LORA_RANK + QK_ROPE_DIM
    rows = a.shape[0]
    z = lambda n: jnp.zeros((rows, n), a.dtype)
    kr = a[:, o_mla + Q_LORA_RANK + KV_LORA_RANK:o_gate]
    parts = [
        a[:, :3 * w + DECAY_LORA + ICLR_LORA],
        a[:, 3 * w + DECAY_LORA + ICLR_LORA:o_mla], z(GATE_LORA_PAD - GATE_LORA),
        a[:, o_mla:o_mla + Q_LORA_RANK + KV_LORA_RANK],
        z(QK_NOPE_DIM), kr, z(LANES - QK_NOPE_DIM - QK_ROPE_DIM),
        a[:, o_gate:],
    ]
    return jnp.concatenate(parts, axis=1)


def _pad_heads(wmat, per_head, lead):
    kdim = wmat.shape[0]
    wh = wmat.reshape(kdim, MLA_HEADS, per_head)
    wh = jnp.pad(wh, ((0, 0), (0, 0), (lead, QK_PAD - per_head - lead)))
    return wh.reshape(kdim, MLA_HEADS * QK_PAD)


def _rope_tables(tp, reps):
    half = QK_ROPE_DIM // 2
    pos = (jnp.arange(tp, dtype=jnp.int32) - PADF).astype(_F32)
    inv_freq = ROPE_THETA ** (-jnp.arange(half, dtype=_F32) / half)
    ang = pos[:, None] * inv_freq[None, :]
    cos, sin = jnp.cos(ang), jnp.sin(ang)
    zeros = lambda n: jnp.zeros((tp, n), _F32)
    tail = LANES - QK_NOPE_DIM - QK_ROPE_DIM
    cos_t = jnp.concatenate([jnp.ones((tp, QK_NOPE_DIM), _F32), cos, cos, zeros(tail)], axis=1)
    sin_a = jnp.concatenate([zeros(QK_NOPE_DIM), -sin, zeros(half + tail)], axis=1)
    sin_b = jnp.concatenate([zeros(QK_NOPE_DIM + half), sin, zeros(tail)], axis=1)
    tab = jnp.stack([cos_t, sin_a, sin_b])
    return jnp.tile(tab, (1, reps, 1))


def kernel(x, meta_tokens, ln_in_g, ln_in_b, w_in, mu_shift, w0, w_lora_up, a0, a_lora_up,
           g_lora_up, k_k, k_a, r_k, lnx_g, lnx_b, q_norm_g, w_uq, kv_norm_g, w_uk, w_uv,
           w_proj_rwkv, w_proj_mla, w_out, ln1_g, ln1_b, w_up, conv_w, conv_b, w_down,
           ln2_g, ln2_b):
    bsz, seq, d = x.shape
    depth = w_in.shape[0]
    tp = seq + FRONT
    r = bsz * tp
    assert tp % ATT_TILE == 0 and r % ROW_TM == 0 and r % PREP_TM == 0
    alpha = float((2 * depth) ** 0.25)
    w = RWKV_WIDTH

    meta_pad = jnp.pad(meta_tokens.astype(x.dtype), ((PADF, 0), (0, 0)))
    h = _embed(x, meta_pad, ln_in_g[None], ln_in_b[None]).reshape(r, d)

    hid = jnp.arange(w) // HEAD_DIM
    head_sum = (hid[:, None] == hid[None, :]).astype(_BF16)
    head_mean = head_sum * (1.0 / HEAD_DIM)
    tt = jnp.arange(PREP_TM)
    tri = ((tt[:, None] >= tt[None, :])
           & (tt[:, None] // WKV_CHUNK == tt[None, :] // WKV_CHUNK)).astype(_BF16)
    reps = 1
    while (reps * tp) % PREP_TM:
        reps += 1
    rope_tab = _rope_tables(tp, reps)

    for l in range(depth):
        w_in_l = _rearrange_in_cols(w_in[l]).astype(_BF16)
        mu = _rearrange_in_cols(jnp.pad(mu_shift[l][None], ((0, 0), (0, w_in.shape[2] - mu_shift.shape[1]))))
        mu = mu[:, :RW_END]
        vec = jnp.stack([w0[l], a0[l], k_k[l], k_a[l]])
        wa_up = jnp.concatenate([
            jnp.concatenate([w_lora_up[l], jnp.zeros((DECAY_LORA, w), _F32)], axis=1),
            jnp.concatenate([jnp.zeros((ICLR_LORA, w), _F32), a_lora_up[l]], axis=1)], axis=0)
        g_up = jnp.pad(g_lora_up[l], ((0, GATE_LORA_PAD - GATE_LORA), (0, 0)))
        wuq = _pad_heads(w_uq[l], QK_NOPE_DIM + QK_ROPE_DIM, 0)
        wuk = _pad_heads(w_uk[l], QK_NOPE_DIM, 0)

        p, pgate = _inproj(h, w_in_l)
        kt, rt, kd, bd, v, pc, bonus, g, q, kk, vv = _prep(
            p, rope_tab, mu, vec, r_k[l].reshape(1, w), wa_up.astype(_BF16), g_up.astype(_BF16),
            head_sum, tri, q_norm_g[l][None], kv_norm_g[l][None],
            wuq.astype(_BF16), wuk.astype(_BF16), w_uv[l].T.astype(_BF16))
        y = _wkv(kt, rt, kd, bd, v, pc, bsz)
        ymla = _attention(q, kk, vv, bsz)
        h = _merge(y, bonus, g, ymla, pgate, h, jnp.stack([lnx_g[l], lnx_b[l]]), head_mean,
                   w_proj_rwkv[l].astype(_BF16), w_proj_mla[l].astype(_BF16),
                   w_out[l].astype(_BF16), jnp.stack([ln1_g[l], ln1_b[l]]), tp, alpha)
        h = _ffn(h, w_up[l].astype(_BF16), conv_w[l], conv_b[l][None], w_down[l].astype(_BF16),
                 jnp.stack([ln2_g[l], ln2_b[l]]), tp, alpha)
    return h.reshape(bsz, tp, d)[:, FRONT:]
```

```python
import functools

import numpy as np
import jax
import jax.numpy as jnp
from jax import lax
from jax.experimental import pallas as pl
from jax.experimental.pallas import tpu as pltpu

N_META = 16
HEAD_DIM = 64
RWKV_HEADS = 8
RWKV_WIDTH = RWKV_HEADS * HEAD_DIM
DECAY_LORA = 64
ICLR_LORA = 64
GATE_LORA = 160
GN_EPS = 64e-5
MLA_HEADS = 8
QK_NOPE_DIM = 64
QK_ROPE_DIM = 32
V_HEAD_DIM = 64
Q_LORA_RANK = 256
KV_LORA_RANK = 256
ROPE_THETA = 10000.0
MLA_WIDTH = MLA_HEADS * V_HEAD_DIM
CONV_WIDTH = 3
LN_EPS = 1e-5
RMS_EPS = 1e-6

LANES = 128
MXU_DIM = 256
VMEM_LIMIT = 56 * 1024 * 1024

FRONT = MXU_DIM
PADF = FRONT - N_META
C_R, C_K, C_V = 0, RWKV_WIDTH, 2 * RWKV_WIDTH
C_WA = 3 * RWKV_WIDTH
GATE_LORA_PAD = 2 * LANES
C_G = C_WA + LANES
RW_END = C_G + GATE_LORA_PAD
C_CQ = RW_END
C_CKV = C_CQ + Q_LORA_RANK
C_KR = C_CKV + KV_LORA_RANK
C_GATE = C_KR + LANES
QK_PAD = LANES

WKV_CHUNK = 64
WKV_BATCH = 2
ATT_TILE = MXU_DIM
ATT_PAIRS = 4
PREP_TM = 256
ROW_TM = 512
FFN_TM = 256
FF_CHUNK = 256
NEG_BIG = -1e30
BF16_ROWS = 16
SUM_ROW = (V_HEAD_DIM, 0)

_F32 = jnp.float32
_BF16 = jnp.bfloat16


def _cparams(*sem):
    return pltpu.CompilerParams(dimension_semantics=sem, vmem_limit_bytes=VMEM_LIMIT)


def _dot(a, b):
    return jnp.dot(a, b, preferred_element_type=_F32)


def _dot_nt(a, b):
    return lax.dot_general(a, b, (((1,), (1,)), ((), ())), preferred_element_type=_F32)


def _dot_tn(a, b):
    return lax.dot_general(a, b, (((0,), (0,)), ((), ())), preferred_element_type=_F32)


def _split_dot(x, w_bf16, parts, left=False):
    acc = None
    rem = x
    for _ in range(parts):
        hi = rem.astype(_BF16)
        term = _dot(w_bf16, hi) if left else _dot(hi, w_bf16)
        acc = term if acc is None else acc + term
        rem = rem - hi.astype(_F32)
    return acc


def _layer_norm(x, g, b):
    mu = jnp.mean(x, axis=-1, keepdims=True)
    xc = x - mu
    var = jnp.mean(xc * xc, axis=-1, keepdims=True)
    return xc * lax.rsqrt(var + LN_EPS) * g + b


def _sigmoid(x):
    return 1.0 / (1.0 + jnp.exp(-x))


def _valid_rows(tile_idx, tm, tp):
    row = tile_idx * tm + lax.broadcasted_iota(jnp.int32, (tm, 1), 0)
    bidx = jnp.floor((row.astype(_F32) + 0.5) * (1.0 / tp)).astype(jnp.int32)
    return (row - bidx * tp) >= PADF


def _embed_kernel(x_ref, meta_ref, g_ref, b_ref, o_ref):
    t = pl.program_id(1)
    src = jnp.where(t == 0, meta_ref[...], x_ref[0])
    y = _layer_norm(src, g_ref[...], b_ref[...])
    row = lax.broadcasted_iota(jnp.int32, (FRONT, 1), 0)
    o_ref[0] = jnp.where((t > 0) | (row >= PADF), y, 0.0)


def _embed(x, meta_pad, g, b):
    bsz, seq, d = x.shape
    nt = seq // FRONT + 1
    return pl.pallas_call(
        _embed_kernel,
        out_shape=jax.ShapeDtypeStruct((bsz, nt * FRONT, d), _F32),
        grid=(bsz, nt),
        in_specs=[
            pl.BlockSpec((1, FRONT, d), lambda bi, t: (bi, jnp.maximum(t - 1, 0), 0)),
            pl.BlockSpec((FRONT, d), lambda bi, t: (0, 0)),
            pl.BlockSpec((1, d), lambda bi, t: (0, 0)),
            pl.BlockSpec((1, d), lambda bi, t: (0, 0)),
        ],
        out_specs=pl.BlockSpec((1, FRONT, d), lambda bi, t: (bi, t, 0)),
        compiler_params=_cparams("parallel", "arbitrary"),
        name="embed_ln",
    )(x, meta_pad, g, b)


def _inproj_kernel(h_ref, w_ref, mix_ref, gate_ref, *, n_chunk):
    hb = h_ref[...].astype(_BF16)
    for o_ref, base in ((mix_ref, 0), (gate_ref, C_GATE)):
        for c0 in range(0, o_ref.shape[1], n_chunk):
            o_ref[:, c0:c0 + n_chunk] = _dot(
                hb, w_ref[:, base + c0:base + c0 + n_chunk]).astype(o_ref.dtype)


def _inproj(h, w):
    r, d = h.shape
    n = w.shape[1]
    return pl.pallas_call(
        functools.partial(_inproj_kernel, n_chunk=2 * MXU_DIM),
        out_shape=[jax.ShapeDtypeStruct((r, C_GATE), _BF16),
                   jax.ShapeDtypeStruct((r, n - C_GATE), _BF16)],
        grid=(r // ROW_TM,),
        in_specs=[
            pl.BlockSpec((ROW_TM, d), lambda i: (i, 0)),
            pl.BlockSpec((d, n), lambda i: (0, 0)),
        ],
        out_specs=[pl.BlockSpec((ROW_TM, C_GATE), lambda i: (i, 0)),
                   pl.BlockSpec((ROW_TM, n - C_GATE), lambda i: (i, 0))],
        compiler_params=_cparams("parallel"),
        name="in_proj",
    )(h, w)


def _prep_kernel(p_ref, prev_ref, rope_ref, mu_ref, vec_ref, rk_ref, wa_up_ref, g_up_ref,
                 head_sum_ref, tri_ref, qg_ref, kvg_ref, wuq_ref, wuk_ref, wuv_ref,
                 kt_ref, rt_ref, kd_ref, bd_ref, v_ref, pc_ref, bonus_ref, g_ref,
                 q_ref, kk_ref, vv_ref):
    tm = p_ref.shape[0]
    w = RWKV_WIDTH

    pr = p_ref[:, :RW_END].astype(_F32)
    shifted = pltpu.roll(pr, 1, axis=0)
    row = lax.broadcasted_iota(jnp.int32, (tm, 1), 0)
    prev_last = prev_ref[7:8, :].astype(_F32)
    shifted = jnp.where(row == 0, prev_last, shifted)
    ps = pr + (shifted - pr) * mu_ref[...]

    r = ps[:, C_R:C_R + w]
    k = ps[:, C_K:C_K + w]
    v = ps[:, C_V:C_V + w]
    wa_in = ps[:, C_WA:C_WA + LANES]
    lane = lax.broadcasted_iota(jnp.int32, (1, LANES), 1)
    wa_in = jnp.where(lane < DECAY_LORA, jnp.tanh(wa_in), wa_in)
    wa = _dot(wa_in.astype(_BF16), wa_up_ref[...])
    w0, a0, k_k, k_a = (vec_ref[0:1, :], vec_ref[1:2, :], vec_ref[2:3, :], vec_ref[3:4, :])

    z = -(w0 + wa[:, :w])
    softplus = jnp.maximum(z, 0.0) + jnp.log1p(jnp.exp(-jnp.abs(z)))
    logw = -jnp.exp(-softplus - 0.5)
    a = _sigmoid(a0 + wa[:, w:])
    gate = _dot(_sigmoid(ps[:, C_G:C_G + GATE_LORA_PAD]).astype(_BF16), g_up_ref[...])

    head_sum = head_sum_ref[...]
    kk = k * k_k
    kk_ss = _split_dot(kk * kk, head_sum, 2)
    kk = kk / jnp.maximum(jnp.sqrt(kk_ss), 1e-12)
    k = k * (1.0 + (a - 1.0) * k_a)
    b = kk * a
    bonus = _split_dot(r * k * rk_ref[...], head_sum, 2) * v

    cum = _split_dot(logw, tri_ref[...], 3, left=True)
    e_neg = jnp.exp(-cum)
    kt_ref[...] = (kk * jnp.exp(cum - logw)).astype(_BF16)
    rt_ref[...] = (r * jnp.exp(cum)).astype(_BF16)
    kd_ref[...] = (k * e_neg).astype(_BF16)
    bd_ref[...] = (b * e_neg).astype(_BF16)
    v_ref[...] = v.astype(_BF16)
    bonus_ref[...] = bonus.astype(_BF16)
    g_ref[...] = gate.astype(_BF16)
    for ci in range(tm // WKV_CHUNK):
        last = (ci + 1) * WKV_CHUNK - 1
        pc_ref[ci] = jnp.exp(cum[last:last + 1, :])

    cos_t, sin_a, sin_b = rope_ref[0], rope_ref[1], rope_ref[2]

    def rope(x):
        nrep = x.shape[1] // LANES
        n = x.shape[1]
        c, sa, sb = (jnp.tile(t, (1, nrep)) if nrep > 1 else t for t in (cos_t, sin_a, sin_b))
        half = QK_ROPE_DIM // 2
        return x * c + pltpu.roll(x, n - half, axis=1) * sa + pltpu.roll(x, half, axis=1) * sb

    def rms(x, gain):
        return x * lax.rsqrt(jnp.mean(x * x, axis=-1, keepdims=True) + RMS_EPS) * gain

    cq = p_ref[:, C_CQ:C_CQ + Q_LORA_RANK].astype(_F32)
    ckv = p_ref[:, C_CKV:C_CKV + KV_LORA_RANK].astype(_F32)
    kr = p_ref[:, C_KR:C_KR + LANES].astype(_F32)
    qn = rms(cq, qg_ref[...]).astype(_BF16)
    scale = float(np.log2(np.e) / np.sqrt(QK_NOPE_DIM + QK_ROPE_DIM))
    q_ref[...] = (rope(_dot(qn, wuq_ref[...])) * scale).astype(_BF16)
    kvn = rms(ckv, kvg_ref[...]).astype(_BF16)
    kk_ref[...] = (_dot(kvn, wuk_ref[...]) + jnp.tile(rope(kr), (1, MLA_HEADS))).astype(_BF16)
    vv_ref[...] = _dot_nt(wuv_ref[...], kvn).astype(_BF16)


def _prep(p, rope_tab, mu, vec, rk, wa_up, g_up, head_sum, tri, qg, kvg, wuq, wuk, wuv):
    r = p.shape[0]
    tm = PREP_TM
    nt = r // tm
    nrope = rope_tab.shape[1] // tm
    w = RWKV_WIDTH
    nchunk = tm // WKV_CHUNK
    full = lambda a: pl.BlockSpec(a.shape, lambda i: (0,) * a.ndim)
    row_out = lambda n: pl.BlockSpec((tm, n), lambda i: (i, 0))
    out_shape = (
        [jax.ShapeDtypeStruct((r, w), _BF16)] * 5
        + [jax.ShapeDtypeStruct((nt * nchunk, 1, w), _F32)]
        + [jax.ShapeDtypeStruct((r, w), _BF16)] * 2
        + [jax.ShapeDtypeStruct((r, MLA_HEADS * QK_PAD), _BF16)] * 2
        + [jax.ShapeDtypeStruct((MLA_WIDTH, r), _BF16)]
    )
    out_specs = (
        [row_out(w)] * 5
        + [pl.BlockSpec((nchunk, 1, w), lambda i: (i, 0, 0))]
        + [row_out(w)] * 2
        + [row_out(MLA_HEADS * QK_PAD)] * 2
        + [pl.BlockSpec((MLA_WIDTH, tm), lambda i: (0, i))]
    )
    return pl.pallas_call(
        _prep_kernel,
        out_shape=out_shape,
        grid=(nt,),
        in_specs=[
            pl.BlockSpec((tm, C_GATE), lambda i: (i, 0)),
            pl.BlockSpec((8, RW_END), lambda i: (jnp.maximum(i * (tm // 8) - 1, 0), 0)),
            pl.BlockSpec((3, tm, LANES), lambda i: (0, i % nrope, 0)),
            full(mu), full(vec), full(rk), full(wa_up), full(g_up), full(head_sum), full(tri),
            full(qg), full(kvg), full(wuq), full(wuk), full(wuv),
        ],
        out_specs=out_specs,
        compiler_params=_cparams("parallel"),
        name="prep",
    )(p, p, rope_tab, mu, vec, rk, wa_up, g_up, head_sum, tri, qg, kvg, wuq, wuk, wuv)


def _wkv_kernel(kt_ref, rt_ref, kd_ref, bd_ref, v_ref, pc_ref, y_ref, h_scr):
    c = pl.program_id(1)

    @pl.when(c < PADF // WKV_CHUNK)
    def _():
        h_scr[...] = jnp.zeros_like(h_scr)
        y_ref[...] = jnp.zeros_like(y_ref)

    @pl.when(c >= PADF // WKV_CHUNK)
    def _():
        _wkv_chunk(kt_ref, rt_ref, kd_ref, bd_ref, v_ref, pc_ref, y_ref, h_scr)


def _wkv_chunk(kt_ref, rt_ref, kd_ref, bd_ref, v_ref, pc_ref, y_ref, h_scr):
    npair = RWKV_HEADS // 2
    cl = WKV_CHUNK
    n2 = 2 * cl
    lane = lax.broadcasted_iota(jnp.int32, (1, LANES), 1)
    first = lane < HEAD_DIM
    row = lax.broadcasted_iota(jnp.int32, (n2, n2), 0)
    col = lax.broadcasted_iota(jnp.int32, (n2, n2), 1)
    strict = row > col
    incl = row >= col
    eye = (row == col).astype(_F32)

    def stack(x):
        zero = jnp.zeros_like(x)
        return jnp.concatenate([jnp.where(first, x, zero), jnp.where(first, zero, x)], axis=0)

    def same_block(size):
        sh = size.bit_length() - 1
        return jnp.right_shift(row, sh) == jnp.right_shift(col, sh)

    units = [(bl, j) for bl in range(kt_ref.shape[0]) for j in range(npair)]
    each = lambda fn, *lists: [fn(*args) for args in zip(*lists)]
    bf = lambda x: x.astype(_BF16)

    def load(ref):
        return [stack(ref[bl, :, j * LANES:(j + 1) * LANES]) for bl, j in units]

    kt2, rt2, kd2, bd2, v2 = load(kt_ref), load(rt_ref), load(kd_ref), load(bd_ref), load(v_ref)
    lhs = each(lambda a, b: jnp.concatenate([a, b], axis=0), kt2, rt2)
    rhs = each(lambda a, b: jnp.concatenate([a, b], axis=0), kd2, bd2)
    gram = each(_dot_nt, lhs, rhs)
    a_kk = [bf(jnp.where(strict, g[:n2, :n2], 0.0)) for g in gram]
    a_kb = [jnp.where(strict, g[:n2, n2:], 0.0) for g in gram]
    a_out = [bf(jnp.concatenate([jnp.where(incl, g[n2:, :n2], 0.0),
                                 jnp.where(incl, -g[n2:, n2:], 0.0)], axis=1)) for g in gram]

    blk = same_block(8)
    nd = [bf(jnp.where(blk, a, 0.0)) for a in a_kb]
    s2 = [bf(x) for x in each(_dot, nd, nd)]
    i_minus = [eye - x.astype(_F32) for x in nd]
    p1 = each(lambda im, s: im + _dot(bf(im), s), i_minus, s2)
    s4 = [bf(x) for x in each(_dot, s2, s2)]
    t_inv = each(lambda p, s: p + _dot(bf(p), s), p1, s4)
    size = 8
    while size < cl:
        outer = same_block(2 * size)
        sel = outer & jnp.logical_not(blk)
        off = [bf(jnp.where(sel, a, 0.0)) for a in a_kb]
        tb = [bf(t) for t in t_inv]
        m = [bf(x) for x in each(_dot, tb, off)]
        t_inv = each(lambda t, mm, tbb: t - _dot(mm, tbb), t_inv, m, tb)
        blk = outer
        size *= 2
    tb = [bf(t) for t in t_inv]
    akv = each(_dot, a_kk, v2)

    hbd = [h_scr[i] for i in range(len(units))]
    x0 = each(lambda l, h: _dot(l, bf(h)), lhs, hbd)
    u2 = each(lambda t, x, w: bf(_dot(t, bf(x[:n2] + w))), tb, x0, akv)
    vu = each(lambda a, b: jnp.concatenate([a, b], axis=0), v2, u2)
    y2 = each(lambda x, a, z: x[n2:] + _dot(a, z), x0, a_out, vu)
    upd = each(lambda kd, bd, z: _dot_tn(jnp.concatenate([kd, -bd], axis=0), z), kd2, bd2, vu)
    for i, (bl, j) in enumerate(units):
        sl = slice(j * LANES, (j + 1) * LANES)
        y_ref[bl, :, sl] = (y2[i][:cl] + y2[i][cl:]).astype(y_ref.dtype)
        pc_row = pc_ref[bl, 0, :, sl]
        pc_col = jnp.transpose(jnp.broadcast_to(pc_row, (LANES, LANES)))
        h_scr[i] = (hbd[i] + upd[i]) * pc_col


def _wkv(kt, rt, kd, bd, v, pc, bsz):
    r, w = kt.shape
    tp = r // bsz
    nc = tp // WKV_CHUNK
    nb = WKV_BATCH
    blk = pl.BlockSpec((nb, WKV_CHUNK, w), lambda bg, c: (bg, c, 0))
    args = [a.reshape(bsz, tp, w) for a in (kt, rt, kd, bd, v)] + [pc.reshape(bsz, nc, 1, w)]
    y = pl.pallas_call(
        _wkv_kernel,
        out_shape=jax.ShapeDtypeStruct((bsz, tp, w), _F32),
        grid=(bsz // nb, nc),
        in_specs=[blk] * 5 + [pl.BlockSpec((nb, 1, 1, w), lambda bg, c: (bg, c, 0, 0))],
        out_specs=blk,
        scratch_shapes=[pltpu.VMEM((nb * (RWKV_HEADS // 2), LANES, LANES), _F32)],
        compiler_params=_cparams("parallel", "arbitrary"),
        name="wkv7",
    )(*args)
    return y.reshape(r, w)


def _attn_kernel(q_ref, k_ref, v_ref, o_ref, s0_scr, s1_scr):
    i = pl.program_id(2)
    tq = q_ref.shape[0]
    tk = tq
    npair = q_ref.shape[1] // (2 * QK_PAD)
    q_pos = i * tq + lax.broadcasted_iota(jnp.int32, (1, tq), 1)
    first_k = lax.broadcasted_iota(jnp.int32, (1, 2 * QK_PAD), 1) < QK_PAD
    first_v = lax.broadcasted_iota(jnp.int32, (LANES, 1), 0) < V_HEAD_DIM
    qh = []
    for g in range(npair):
        qp = q_ref[:, g * 2 * QK_PAD:(g + 1) * 2 * QK_PAD]
        zq = jnp.zeros_like(qp)
        qh.append((jnp.where(first_k, qp, zq), jnp.where(first_k, zq, qp)))

    s_slots = (s0_scr, s1_scr)

    def scores(jb, slot):
        start = pl.multiple_of(jb * tk, tk)
        for g in range(npair):
            kb = k_ref[pl.ds(start, tk), g * 2 * QK_PAD:(g + 1) * 2 * QK_PAD]
            for h in range(2):
                s_slots[slot][2 * g + h] = _dot_nt(kb, qh[g][h])

    def update(jb, slot, stats, masked):
        start = pl.multiple_of(jb * tk, tk)
        if masked:
            k_pos = start + lax.broadcasted_iota(jnp.int32, (tk, 1), 0)
            mask = (k_pos <= q_pos) & (k_pos >= PADF)
        out = []
        for g in range(npair):
            vt = v_ref[g * LANES:(g + 1) * LANES, pl.ds(start, tk)]
            for h in range(2):
                m_prev, acc = stats[4 * g + 2 * h], stats[4 * g + 2 * h + 1]
                s = s_slots[slot][2 * g + h]
                if masked:
                    s = jnp.where(mask, s, NEG_BIG)
                m_new = jnp.maximum(m_prev, jnp.max(s, axis=0, keepdims=True))
                prob = jnp.exp2(s - m_new).astype(_BF16)
                lo = SUM_ROW[h]
                pieces = [vt[:lo]] if lo else []
                pieces += [jnp.ones((BF16_ROWS, tk), _BF16), vt[lo + BF16_ROWS:]]
                vt_h = jnp.concatenate(pieces, axis=0)
                acc = jnp.exp2(m_prev - m_new) * acc + _dot(vt_h, prob)
                out += [m_new, acc]
        return tuple(out)

    init = (jnp.full((1, tq), NEG_BIG, _F32), jnp.zeros((LANES, tq), _F32)) * (2 * npair)

    scores(0, 0)

    @pl.when(i > 0)
    def _():
        scores(1, 1)

    carry = update(0, 0, init, True)

    def two_blocks(pidx, stats):
        jb = 2 * pidx + 1
        scores(jb + 1, 0)
        stats = update(jb, 1, stats, False)
        scores(jb + 2, 1)
        return update(jb + 1, 0, stats, False)

    def tail_even_i(_, stats):
        scores(i, 0)
        stats = update(i - 1, 1, stats, False)
        return update(i, 0, stats, True)

    def tail_odd_i(_, stats):
        return update(i, 1, stats, True)

    is_even = (i % 2 == 0)
    carry = lax.fori_loop(0, (i - 1) // 2, two_blocks, carry)
    carry = lax.fori_loop(0, jnp.where((i > 0) & is_even, 1, 0), tail_even_i, carry)
    carry = lax.fori_loop(0, jnp.where(is_even, 0, 1), tail_odd_i, carry)
    for g in range(npair):
        acc_a, acc_b = carry[4 * g + 1], carry[4 * g + 3]
        o = jnp.where(first_v, acc_a / acc_a[SUM_ROW[0]:SUM_ROW[0] + 1],
                      acc_b / acc_b[SUM_ROW[1]:SUM_ROW[1] + 1])
        o_ref[:, g * LANES:(g + 1) * LANES] = jnp.transpose(o).astype(o_ref.dtype)


def _attention(q, k, v, bsz):
    r = q.shape[0]
    tp = r // bsz
    nq = tp // ATT_TILE
    ng = MLA_HEADS // 2 // ATT_PAIRS
    qw, vw = ATT_PAIRS * 2 * QK_PAD, ATT_PAIRS * 2 * V_HEAD_DIM
    return pl.pallas_call(
        _attn_kernel,
        out_shape=jax.ShapeDtypeStruct((r, MLA_WIDTH), _BF16),
        grid=(bsz, ng, nq),
        in_specs=[
            pl.BlockSpec((ATT_TILE, qw), lambda bi, pj, i: (bi * nq + i, pj)),
            pl.BlockSpec((tp, qw), lambda bi, pj, i: (bi, pj)),
            pl.BlockSpec((vw, tp), lambda bi, pj, i: (pj, bi)),
        ],
        out_specs=pl.BlockSpec((ATT_TILE, vw), lambda bi, pj, i: (bi * nq + i, pj)),
        scratch_shapes=[pltpu.VMEM((2 * ATT_PAIRS, ATT_TILE, ATT_TILE), _F32)] * 2,
        compiler_params=_cparams("parallel", "parallel", "arbitrary"),
        name="mla_attention",
    )(q, k, v)


def _merge_kernel(y_ref, bonus_ref, g_ref, ymla_ref, gate_ref, h_ref, lnx_ref, head_mean_ref,
                  wpr_ref, wpm_ref, wout_ref, ln_ref, o_ref, *, tp, alpha):
    tm = y_ref.shape[0]
    d = h_ref.shape[1]
    y = y_ref[...].astype(_F32)
    head_mean = head_mean_ref[...]
    ym = _split_dot(y, head_mean, 2)
    yc = y - ym
    yv = _split_dot(yc * yc, head_mean, 2)
    yn = yc * lax.rsqrt(yv + GN_EPS) * lnx_ref[0:1, :] + lnx_ref[1:2, :]
    y_rwkv = (yn.astype(_F32) + bonus_ref[...].astype(_F32)) * g_ref[...].astype(_F32)
    gates = _sigmoid(gate_ref[...].astype(_F32))
    mixed = (gates[:, :d] * _dot(y_rwkv.astype(_BF16), wpr_ref[...])
             + gates[:, d:] * _dot(ymla_ref[...], wpm_ref[...]))
    out = _dot(mixed.astype(_BF16), wout_ref[...])
    hn = _layer_norm(alpha * h_ref[...] + out, ln_ref[0:1, :], ln_ref[1:2, :])
    o_ref[...] = jnp.where(_valid_rows(pl.program_id(0), tm, tp), hn, 0.0)


def _merge(y, bonus, g, ymla, pgate, h, lnx, head_mean, wpr, wpm, wout, ln, tp, alpha):
    r, d = h.shape
    tm = ROW_TM
    w = RWKV_WIDTH
    full = lambda a: pl.BlockSpec(a.shape, lambda i: (0,) * a.ndim)
    rows = lambda n: pl.BlockSpec((tm, n), lambda i: (i, 0))
    return pl.pallas_call(
        functools.partial(_merge_kernel, tp=tp, alpha=alpha),
        out_shape=jax.ShapeDtypeStruct((r, d), _F32),
        grid=(r // tm,),
        in_specs=[rows(w), rows(w), rows(w), rows(MLA_WIDTH),
                  rows(2 * d),
                  rows(d), full(lnx), full(head_mean), full(wpr), full(wpm), full(wout), full(ln)],
        out_specs=rows(d),
        compiler_params=_cparams("parallel"),
        name="merge",
    )(y, bonus, g, ymla, pgate, h, lnx, head_mean, wpr, wpm, wout, ln)


def _ffn_kernel(h_ref, wup_ref, cw_ref, cb_ref, wdown_ref, ln_ref, o_ref, carry_ref, *, tp, alpha):
    i = pl.program_id(0)
    tm, d = h_ref.shape
    dff = wdown_ref.shape[0]

    @pl.when(i == 0)
    def _():
        carry_ref[0:8, :] = jnp.zeros((8, carry_ref.shape[1]), _F32)

    h = h_ref[...]
    hb = h.astype(_BF16)
    def conv(u, c0):
        cs = slice(c0, c0 + FF_CHUNK)
        carry_ref[8:, cs] = u
        u1 = carry_ref[7:7 + tm, cs]
        u2 = carry_ref[6:6 + tm, cs]
        out = (cw_ref[2:3, cs] * u + cw_ref[1:2, cs] * u1 + cw_ref[0:1, cs] * u2 + cb_ref[:, cs])
        carry_ref[0:8, cs] = u[tm - 8:, :]
        return out

    def up(c0):
        return (_dot(hb, wup_ref[:, c0:c0 + FF_CHUNK]),
                _dot(hb, wup_ref[:, dff + c0:dff + c0 + FF_CHUNK]))

    acc = jnp.zeros((tm, d), _F32)
    chunks = list(range(0, dff, FF_CHUNK))
    ahead = up(chunks[0])
    for n, c0 in enumerate(chunks):
        gate_raw, val_raw = ahead
        if n + 1 < len(chunks):
            ahead = up(chunks[n + 1])
        gate = conv(gate_raw, c0)
        val = conv(val_raw, dff + c0)
        act = gate * _sigmoid(gate) * val
        acc = acc + _dot(act.astype(_BF16), wdown_ref[c0:c0 + FF_CHUNK, :])
    hn = _layer_norm(alpha * h + acc, ln_ref[0:1, :], ln_ref[1:2, :])
    o_ref[...] = jnp.where(_valid_rows(i, tm, tp), hn, 0.0).reshape(o_ref.shape)


def _ffn(h, wup, cw, cb, wdown, ln, tp, alpha, bsz, final):
    r, d = h.shape
    tm = FFN_TM
    full = lambda a: pl.BlockSpec(a.shape, lambda i: (0,) * a.ndim)
    if final:
        nt = tp // tm
        assert FRONT == tm
        out_shape = jax.ShapeDtypeStruct((bsz, tp - FRONT, d), _F32)
        out_spec = pl.BlockSpec((1, tm, d), lambda i: (i // nt, jnp.maximum(i % nt - 1, 0), 0))
    else:
        out_shape = jax.ShapeDtypeStruct((r, d), _F32)
        out_spec = pl.BlockSpec((tm, d), lambda i: (i, 0))
    return pl.pallas_call(
        functools.partial(_ffn_kernel, tp=tp, alpha=alpha),
        out_shape=out_shape,
        grid=(r // tm,),
        in_specs=[pl.BlockSpec((tm, d), lambda i: (i, 0)),
                  full(wup), full(cw), full(cb), full(wdown), full(ln)],
        out_specs=out_spec,
        scratch_shapes=[pltpu.VMEM((8 + tm, wup.shape[1]), _F32)],
        compiler_params=_cparams("arbitrary"),
        name="conv_ffn",
    )(h, wup, cw, cb, wdown, ln)


def _pad_cols(a, n):
    return jnp.pad(a, ((0, 0), (0, n - a.shape[1])))


def _rearrange_in_cols(a):
    w = RWKV_WIDTH
    o_mla = 3 * w + DECAY_LORA + ICLR_LORA + GATE_LORA
    o_gate = o_mla + Q_LORA_RANK + KV_LORA_RANK + QK_ROPE_DIM
    rows = a.shape[0]
    z = lambda n: jnp.zeros((rows, n), a.dtype)
    kr = a[:, o_mla + Q_LORA_RANK + KV_LORA_RANK:o_gate]
    parts = [
        a[:, :3 * w + DECAY_LORA + ICLR_LORA],
        a[:, 3 * w + DECAY_LORA + ICLR_LORA:o_mla], z(GATE_LORA_PAD - GATE_LORA),
        a[:, o_mla:o_mla + Q_LORA_RANK + KV_LORA_RANK],
        z(QK_NOPE_DIM), kr, z(LANES - QK_NOPE_DIM - QK_ROPE_DIM),
        a[:, o_gate:],
    ]
    return jnp.concatenate(parts, axis=1)


def _pad_heads(wmat, per_head, lead):
    kdim = wmat.shape[0]
    wh = wmat.reshape(kdim, MLA_HEADS, per_head)
    wh = jnp.pad(wh, ((0, 0), (0, 0), (lead, QK_PAD - per_head - lead)))
    return wh.reshape(kdim, MLA_HEADS * QK_PAD)


def _rope_tables(tp, reps):
    half = QK_ROPE_DIM // 2
    pos = (jnp.arange(tp, dtype=jnp.int32) - PADF).astype(_F32)
    inv_freq = ROPE_THETA ** (-jnp.arange(half, dtype=_F32) / half)
    ang = pos[:, None] * inv_freq[None, :]
    cos, sin = jnp.cos(ang), jnp.sin(ang)
    zeros = lambda n: jnp.zeros((tp, n), _F32)
    tail = LANES - QK_NOPE_DIM - QK_ROPE_DIM
    cos_t = jnp.concatenate([jnp.ones((tp, QK_NOPE_DIM), _F32), cos, cos, zeros(tail)], axis=1)
    sin_a = jnp.concatenate([zeros(QK_NOPE_DIM), -sin, zeros(half + tail)], axis=1)
    sin_b = jnp.concatenate([zeros(QK_NOPE_DIM + half), sin, zeros(tail)], axis=1)
    tab = jnp.stack([cos_t, sin_a, sin_b])
    return jnp.tile(tab, (1, reps, 1))


def kernel(x, meta_tokens, ln_in_g, ln_in_b, w_in, mu_shift, w0, w_lora_up, a0, a_lora_up,
           g_lora_up, k_k, k_a, r_k, lnx_g, lnx_b, q_norm_g, w_uq, kv_norm_g, w_uk, w_uv,
           w_proj_rwkv, w_proj_mla, w_out, ln1_g, ln1_b, w_up, conv_w, conv_b, w_down,
           ln2_g, ln2_b):
    bsz, seq, d = x.shape
    depth = w_in.shape[0]
    tp = seq + FRONT
    r = bsz * tp
    assert tp % ATT_TILE == 0 and r % ROW_TM == 0 and r % PREP_TM == 0
    alpha = float((2 * depth) ** 0.25)
    w = RWKV_WIDTH

    meta_pad = jnp.pad(meta_tokens.astype(x.dtype), ((PADF, 0), (0, 0)))
    h = _embed(x, meta_pad, ln_in_g[None], ln_in_b[None]).reshape(r, d)

    hid = jnp.arange(w) // HEAD_DIM
    head_sum = (hid[:, None] == hid[None, :]).astype(_BF16)
    head_mean = head_sum * (1.0 / HEAD_DIM)
    tt = jnp.arange(PREP_TM)
    tri = ((tt[:, None] >= tt[None, :])
           & (tt[:, None] // WKV_CHUNK == tt[None, :] // WKV_CHUNK)).astype(_BF16)
    reps = 1
    while (reps * tp) % PREP_TM:
        reps += 1
    rope_tab = _rope_tables(tp, reps)

    for l in range(depth):
        w_in_l = _rearrange_in_cols(w_in[l]).astype(_BF16)
        mu = _rearrange_in_cols(jnp.pad(mu_shift[l][None], ((0, 0), (0, w_in.shape[2] - mu_shift.shape[1]))))
        mu = mu[:, :RW_END]
        vec = jnp.stack([w0[l], a0[l], k_k[l], k_a[l]])
        wa_up = jnp.concatenate([
            jnp.concatenate([w_lora_up[l], jnp.zeros((DECAY_LORA, w), _F32)], axis=1),
            jnp.concatenate([jnp.zeros((ICLR_LORA, w), _F32), a_lora_up[l]], axis=1)], axis=0)
        g_up = jnp.pad(g_lora_up[l], ((0, GATE_LORA_PAD - GATE_LORA), (0, 0)))
        wuq = _pad_heads(w_uq[l], QK_NOPE_DIM + QK_ROPE_DIM, 0)
        wuk = _pad_heads(w_uk[l], QK_NOPE_DIM, 0)

        p, pgate = _inproj(h, w_in_l)
        kt, rt, kd, bd, v, pc, bonus, g, q, kk, vv = _prep(
            p, rope_tab, mu, vec, r_k[l].reshape(1, w), wa_up.astype(_BF16), g_up.astype(_BF16),
            head_sum, tri, q_norm_g[l][None], kv_norm_g[l][None],
            wuq.astype(_BF16), wuk.astype(_BF16), w_uv[l].T.astype(_BF16))
        y = _wkv(kt, rt, kd, bd, v, pc, bsz)
        ymla = _attention(q, kk, vv, bsz)
        h = _merge(y, bonus, g, ymla, pgate, h, jnp.stack([lnx_g[l], lnx_b[l]]), head_mean,
                   w_proj_rwkv[l].astype(_BF16), w_proj_mla[l].astype(_BF16),
                   w_out[l].astype(_BF16), jnp.stack([ln1_g[l], ln1_b[l]]), tp, alpha)
        h = _ffn(h, w_up[l].astype(_BF16), conv_w[l], conv_b[l][None], w_down[l].astype(_BF16),
                 jnp.stack([ln2_g[l], ln2_b[l]]), tp, alpha, bsz, final=(l == depth - 1))
    return h
```

```python
import functools

import numpy as np
import jax
import jax.numpy as jnp
from jax import lax
from jax.experimental import pallas as pl
from jax.experimental.pallas import tpu as pltpu

N_META = 16
HEAD_DIM = 64
RWKV_HEADS = 8
RWKV_WIDTH = RWKV_HEADS * HEAD_DIM
DECAY_LORA = 64
ICLR_LORA = 64
GATE_LORA = 160
GN_EPS = 64e-5
MLA_HEADS = 8
QK_NOPE_DIM = 64
QK_ROPE_DIM = 32
V_HEAD_DIM = 64
Q_LORA_RANK = 256
KV_LORA_RANK = 256
ROPE_THETA = 10000.0
MLA_WIDTH = MLA_HEADS * V_HEAD_DIM
CONV_WIDTH = 3
LN_EPS = 1e-5
RMS_EPS = 1e-6

LANES = 128
MXU_DIM = 256
VMEM_LIMIT = 56 * 1024 * 1024

FRONT = MXU_DIM
PADF = FRONT - N_META
C_R, C_K, C_V = 0, RWKV_WIDTH, 2 * RWKV_WIDTH
C_WA = 3 * RWKV_WIDTH
GATE_LORA_PAD = 2 * LANES
C_G = C_WA + LANES
RW_END = C_G + GATE_LORA_PAD
C_CQ = RW_END
C_CKV = C_CQ + Q_LORA_RANK
C_KR = C_CKV + KV_LORA_RANK
C_GATE = C_KR + LANES
QK_PAD = LANES

WKV_CHUNK = 64
WKV_BATCH = 2
ATT_TILE = MXU_DIM
ATT_PAIRS = 4
PREP_TM = 256
ROW_TM = 512
FFN_TM = 256
FF_CHUNK = 256
NEG_BIG = -1e30
BF16_ROWS = 16
SUM_ROW = (V_HEAD_DIM, 0)

_F32 = jnp.float32
_BF16 = jnp.bfloat16


def _cparams(*sem):
    return pltpu.CompilerParams(dimension_semantics=sem, vmem_limit_bytes=VMEM_LIMIT)


def _dot(a, b):
    return jnp.dot(a, b, preferred_element_type=_F32)


def _dot_nt(a, b):
    return lax.dot_general(a, b, (((1,), (1,)), ((), ())), preferred_element_type=_F32)


def _dot_tn(a, b):
    return lax.dot_general(a, b, (((0,), (0,)), ((), ())), preferred_element_type=_F32)


def _split_dot(x, w_bf16, parts, left=False):
    acc = None
    rem = x
    for _ in range(parts):
        hi = rem.astype(_BF16)
        term = _dot(w_bf16, hi) if left else _dot(hi, w_bf16)
        acc = term if acc is None else acc + term
        rem = rem - hi.astype(_F32)
    return acc


def _head_sum(x, ones_bd):
    n = ones_bd.shape[0]
    xb = x.astype(_BF16)
    return jnp.concatenate([_dot(xb[:, c0:c0 + n], ones_bd) for c0 in range(0, x.shape[1], n)],
                           axis=1)


def _layer_norm(x, g, b):
    mu = jnp.mean(x, axis=-1, keepdims=True)
    xc = x - mu
    var = jnp.mean(xc * xc, axis=-1, keepdims=True)
    return xc * lax.rsqrt(var + LN_EPS) * g + b


def _sigmoid(x):
    return 1.0 / (1.0 + jnp.exp(-x))


def _valid_rows(tile_idx, tm, tp):
    row = tile_idx * tm + lax.broadcasted_iota(jnp.int32, (tm, 1), 0)
    bidx = jnp.floor((row.astype(_F32) + 0.5) * (1.0 / tp)).astype(jnp.int32)
    return (row - bidx * tp) >= PADF


def _mixin_kernel(*refs, embed, tp):
    if embed:
        x_ref, meta_ref, lng_ref, lnb_ref = refs[:4]
        refs = refs[4:]
    else:
        h_ref = refs[0]
        refs = refs[1:]
    (w_ref, rope_ref, mu_ref, vec_ref, rk_ref, wa_up_ref, g_up_ref, head_sum_ref, tri_ref,
     qg_ref, kvg_ref, wuq_ref, wuk_ref, wuv_ref) = refs[:14]
    refs = refs[14:]
    if embed:
        h_out_ref = refs[0]
        refs = refs[1:]
    (gate_ref, kt_ref, rt_ref, kd_ref, bd_ref, v_ref, pc_ref, bonus_ref, g_ref,
     q_ref, kk_ref, vv_ref, carry_ref) = refs
    i = pl.program_id(0)
    tm = kt_ref.shape[0]
    w = RWKV_WIDTH

    if embed:
        src = jnp.where(i % (tp // tm) == 0, meta_ref[...], x_ref[0])
        h = _layer_norm(src, lng_ref[...], lnb_ref[...])
        h = jnp.where(_valid_rows(i, tm, tp), h, 0.0)
        h_out_ref[...] = h
    else:
        h = h_ref[...]
    hb = h.astype(_BF16)

    @pl.when(i == 0)
    def _():
        carry_ref[...] = jnp.zeros_like(carry_ref)

    def gate_cols(c0, c1):
        gate_ref[:, c0:c1] = _dot(hb, w_ref[:, C_GATE + c0:C_GATE + c1]).astype(gate_ref.dtype)

    ngate = gate_ref.shape[1]
    gq = ngate // 4

    pr = _dot(hb, w_ref[:, :RW_END])
    gate_cols(0, gq)
    shifted = pltpu.roll(pr, 1, axis=0)
    row = lax.broadcasted_iota(jnp.int32, (tm, 1), 0)
    shifted = jnp.where(row == 0, carry_ref[7:8, :], shifted)
    carry_ref[...] = pr[tm - 8:, :]
    ps = pr + (shifted - pr) * mu_ref[...]

    r = ps[:, C_R:C_R + w]
    k = ps[:, C_K:C_K + w]
    v = ps[:, C_V:C_V + w]
    wa_in = ps[:, C_WA:C_WA + LANES]
    lane = lax.broadcasted_iota(jnp.int32, (1, LANES), 1)
    wa_in = jnp.where(lane < DECAY_LORA, jnp.tanh(wa_in), wa_in)
    wa = _dot(wa_in.astype(_BF16), wa_up_ref[...])
    w0, a0, k_k, k_a = (vec_ref[0:1, :], vec_ref[1:2, :], vec_ref[2:3, :], vec_ref[3:4, :])

    z = -(w0 + wa[:, :w])
    softplus = jnp.maximum(z, 0.0) + jnp.log1p(jnp.exp(-jnp.abs(z)))
    logw = -jnp.exp(-softplus - 0.5)
    a = _sigmoid(a0 + wa[:, w:])
    gate = _dot(_sigmoid(ps[:, C_G:C_G + GATE_LORA_PAD]).astype(_BF16), g_up_ref[...])
    gate_cols(gq, 2 * gq)

    head_sum = head_sum_ref[...]
    kk = k * k_k
    kk_ss = _head_sum(kk * kk, head_sum)
    kk = kk / jnp.maximum(jnp.sqrt(kk_ss), 1e-12)
    k = k * (1.0 + (a - 1.0) * k_a)
    b = kk * a
    bonus = _head_sum(r * k * rk_ref[...], head_sum) * v
    gate_cols(2 * gq, 3 * gq)

    cum = _split_dot(logw, tri_ref[...], 3, left=True)
    pm = _dot(hb, w_ref[:, C_CQ:C_GATE])
    gate_cols(3 * gq, ngate)
    e_neg = jnp.exp(-cum)
    kt_ref[...] = (kk * jnp.exp(cum - logw)).astype(_BF16)
    rt_ref[...] = (r * jnp.exp(cum)).astype(_BF16)
    kd_ref[...] = (k * e_neg).astype(_BF16)
    bd_ref[...] = (b * e_neg).astype(_BF16)
    v_ref[...] = v.astype(_BF16)
    bonus_ref[...] = bonus.astype(_BF16)
    g_ref[...] = gate.astype(_BF16)
    for ci in range(tm // WKV_CHUNK):
        last = (ci + 1) * WKV_CHUNK - 1
        pc_ref[ci] = jnp.exp(cum[last:last + 1, :])

    cos_t, sin_a, sin_b = rope_ref[0], rope_ref[1], rope_ref[2]

    def rope(x):
        nrep = x.shape[1] // LANES
        n = x.shape[1]
        c, sa, sb = (jnp.tile(t, (1, nrep)) if nrep > 1 else t for t in (cos_t, sin_a, sin_b))
        half = QK_ROPE_DIM // 2
        return x * c + pltpu.roll(x, n - half, axis=1) * sa + pltpu.roll(x, half, axis=1) * sb

    def rms(x, gain):
        return x * lax.rsqrt(jnp.mean(x * x, axis=-1, keepdims=True) + RMS_EPS) * gain

    cq = pm[:, :Q_LORA_RANK]
    ckv = pm[:, Q_LORA_RANK:Q_LORA_RANK + KV_LORA_RANK]
    kr = pm[:, C_KR - C_CQ:]
    qn = rms(cq, qg_ref[...]).astype(_BF16)
    scale = float(np.log2(np.e) / np.sqrt(QK_NOPE_DIM + QK_ROPE_DIM))
    q_ref[...] = (rope(_dot(qn, wuq_ref[...])) * scale).astype(_BF16)
    kvn = rms(ckv, kvg_ref[...]).astype(_BF16)
    kk_ref[...] = (_dot(kvn, wuk_ref[...]) + jnp.tile(rope(kr), (1, MLA_HEADS))).astype(_BF16)
    vv_ref[...] = _dot_nt(wuv_ref[...], kvn).astype(_BF16)


def _mixin(src, w_in, rope_tab, params, tp, bsz):
    embed = isinstance(src, tuple)
    tm = PREP_TM
    d = w_in.shape[0]
    r = bsz * tp
    nt = r // tm
    per = tp // tm
    nrope = rope_tab.shape[1] // tm
    w = RWKV_WIDTH
    nchunk = tm // WKV_CHUNK
    ngate = w_in.shape[1] - C_GATE
    full = lambda a: pl.BlockSpec(a.shape, lambda i: (0,) * a.ndim)
    row_out = lambda n: pl.BlockSpec((tm, n), lambda i: (i, 0))
    if embed:
        assert FRONT == tm
        x, meta_pad, ln_g, ln_b = src
        lead_args = [x, meta_pad, ln_g, ln_b]
        lead_specs = [pl.BlockSpec((1, tm, d), lambda i: (i // per, jnp.maximum(i % per - 1, 0), 0)),
                      full(meta_pad), full(ln_g), full(ln_b)]
        lead_out, lead_out_specs = [jax.ShapeDtypeStruct((r, d), _F32)], [row_out(d)]
    else:
        lead_args, lead_specs = [src], [row_out(d)]
        lead_out, lead_out_specs = [], []
    out_shape = (
        lead_out
        + [jax.ShapeDtypeStruct((r, ngate), _BF16)]
        + [jax.ShapeDtypeStruct((r, w), _BF16)] * 5
        + [jax.ShapeDtypeStruct((nt * nchunk, 1, w), _F32)]
        + [jax.ShapeDtypeStruct((r, w), _BF16)] * 2
        + [jax.ShapeDtypeStruct((r, MLA_HEADS * QK_PAD), _BF16)] * 2
        + [jax.ShapeDtypeStruct((MLA_WIDTH, r), _BF16)]
    )
    out_specs = (
        lead_out_specs
        + [row_out(ngate)]
        + [row_out(w)] * 5
        + [pl.BlockSpec((nchunk, 1, w), lambda i: (i, 0, 0))]
        + [row_out(w)] * 2
        + [row_out(MLA_HEADS * QK_PAD)] * 2
        + [pl.BlockSpec((MLA_WIDTH, tm), lambda i: (0, i))]
    )
    return pl.pallas_call(
        functools.partial(_mixin_kernel, embed=embed, tp=tp),
        out_shape=out_shape,
        grid=(nt,),
        in_specs=lead_specs + [full(w_in), pl.BlockSpec((3, tm, LANES), lambda i: (0, i % nrope, 0))]
        + [full(a) for a in params],
        out_specs=out_specs,
        scratch_shapes=[pltpu.VMEM((8, RW_END), _F32)],
        compiler_params=_cparams("arbitrary"),
        name="mix_in",
    )(*lead_args, w_in, rope_tab, *params)


def _wkv_kernel(kt_ref, rt_ref, kd_ref, bd_ref, v_ref, pc_ref, y_ref, h_scr):
    c = pl.program_id(1)

    @pl.when(c < PADF // WKV_CHUNK)
    def _():
        h_scr[...] = jnp.zeros_like(h_scr)
        y_ref[...] = jnp.zeros_like(y_ref)

    @pl.when(c >= PADF // WKV_CHUNK)
    def _():
        _wkv_chunk(kt_ref, rt_ref, kd_ref, bd_ref, v_ref, pc_ref, y_ref, h_scr)


def _wkv_chunk(kt_ref, rt_ref, kd_ref, bd_ref, v_ref, pc_ref, y_ref, h_scr):
    npair = RWKV_HEADS // 2
    cl = WKV_CHUNK
    n2 = 2 * cl
    lane = lax.broadcasted_iota(jnp.int32, (1, LANES), 1)
    first = lane < HEAD_DIM
    row = lax.broadcasted_iota(jnp.int32, (n2, n2), 0)
    col = lax.broadcasted_iota(jnp.int32, (n2, n2), 1)
    strict = row > col
    incl = row >= col
    eye = (row == col).astype(_F32)

    def stack(x):
        zero = jnp.zeros_like(x)
        return jnp.concatenate([jnp.where(first, x, zero), jnp.where(first, zero, x)], axis=0)

    def same_block(size):
        sh = size.bit_length() - 1
        return jnp.right_shift(row, sh) == jnp.right_shift(col, sh)

    units = [(bl, j) for bl in range(kt_ref.shape[0]) for j in range(npair)]
    each = lambda fn, *lists: [fn(*args) for args in zip(*lists)]
    bf = lambda x: x.astype(_BF16)

    def load(ref):
        return [stack(ref[bl, :, j * LANES:(j + 1) * LANES]) for bl, j in units]

    kt2, rt2, kd2, bd2, v2 = load(kt_ref), load(rt_ref), load(kd_ref), load(bd_ref), load(v_ref)
    lhs = each(lambda a, b: jnp.concatenate([a, b], axis=0), kt2, rt2)
    rhs = each(lambda a, b: jnp.concatenate([a, b], axis=0), kd2, bd2)
    gram = each(_dot_nt, lhs, rhs)
    a_kk = [bf(jnp.where(strict, g[:n2, :n2], 0.0)) for g in gram]
    a_kb = [jnp.where(strict, g[:n2, n2:], 0.0) for g in gram]
    a_out = [bf(jnp.concatenate([jnp.where(incl, g[n2:, :n2], 0.0),
                                 jnp.where(incl, -g[n2:, n2:], 0.0)], axis=1)) for g in gram]

    blk = same_block(8)
    nd = [bf(jnp.where(blk, a, 0.0)) for a in a_kb]
    s2 = [bf(x) for x in each(_dot, nd, nd)]
    i_minus = [eye - x.astype(_F32) for x in nd]
    p1 = each(lambda im, s: im + _dot(bf(im), s), i_minus, s2)
    s4 = [bf(x) for x in each(_dot, s2, s2)]
    t_inv = each(lambda p, s: p + _dot(bf(p), s), p1, s4)
    size = 8
    while size < cl:
        outer = same_block(2 * size)
        sel = outer & jnp.logical_not(blk)
        off = [bf(jnp.where(sel, a, 0.0)) for a in a_kb]
        tb = [bf(t) for t in t_inv]
        m = [bf(x) for x in each(_dot, tb, off)]
        t_inv = each(lambda t, mm, tbb: t - _dot(mm, tbb), t_inv, m, tb)
        blk = outer
        size *= 2
    tb = [bf(t) for t in t_inv]
    akv = each(_dot, a_kk, v2)

    hbd = [h_scr[i] for i in range(len(units))]
    x0 = each(lambda l, h: _dot(l, bf(h)), lhs, hbd)
    u2 = each(lambda t, x, w: bf(_dot(t, bf(x[:n2] + w))), tb, x0, akv)
    vu = each(lambda a, b: jnp.concatenate([a, b], axis=0), v2, u2)
    y2 = each(lambda x, a, z: x[n2:] + _dot(a, z), x0, a_out, vu)
    upd = each(lambda kd, bd, z: _dot_tn(jnp.concatenate([kd, -bd], axis=0), z), kd2, bd2, vu)
    for i, (bl, j) in enumerate(units):
        sl = slice(j * LANES, (j + 1) * LANES)
        y_ref[bl, :, sl] = (y2[i][:cl] + y2[i][cl:]).astype(y_ref.dtype)
        pc_row = pc_ref[bl, 0, :, sl]
        pc_col = jnp.transpose(jnp.broadcast_to(pc_row, (LANES, LANES)))
        h_scr[i] = (hbd[i] + upd[i]) * pc_col


def _wkv(kt, rt, kd, bd, v, pc, bsz):
    r, w = kt.shape
    tp = r // bsz
    nc = tp // WKV_CHUNK
    nb = WKV_BATCH
    blk = pl.BlockSpec((nb, WKV_CHUNK, w), lambda bg, c: (bg, c, 0))
    args = [a.reshape(bsz, tp, w) for a in (kt, rt, kd, bd, v)] + [pc.reshape(bsz, nc, 1, w)]
    y = pl.pallas_call(
        _wkv_kernel,
        out_shape=jax.ShapeDtypeStruct((bsz, tp, w), _F32),
        grid=(bsz // nb, nc),
        in_specs=[blk] * 5 + [pl.BlockSpec((nb, 1, 1, w), lambda bg, c: (bg, c, 0, 0))],
        out_specs=blk,
        scratch_shapes=[pltpu.VMEM((nb * (RWKV_HEADS // 2), LANES, LANES), _F32)],
        compiler_params=_cparams("parallel", "arbitrary"),
        name="wkv7",
    )(*args)
    return y.reshape(r, w)


def _attn_kernel(q_ref, k_ref, v_ref, o_ref, s0_scr, s1_scr):
    i = pl.program_id(2)
    tq = q_ref.shape[0]
    tk = tq
    npair = q_ref.shape[1] // (2 * QK_PAD)
    q_pos = i * tq + lax.broadcasted_iota(jnp.int32, (1, tq), 1)
    first_k = lax.broadcasted_iota(jnp.int32, (1, 2 * QK_PAD), 1) < QK_PAD
    first_v = lax.broadcasted_iota(jnp.int32, (LANES, 1), 0) < V_HEAD_DIM
    qh = []
    for g in range(npair):
        qp = q_ref[:, g * 2 * QK_PAD:(g + 1) * 2 * QK_PAD]
        zq = jnp.zeros_like(qp)
        qh.append((jnp.where(first_k, qp, zq), jnp.where(first_k, zq, qp)))

    s_slots = (s0_scr, s1_scr)

    def scores(jb, slot):
        start = pl.multiple_of(jb * tk, tk)
        for g in range(npair):
            kb = k_ref[pl.ds(start, tk), g * 2 * QK_PAD:(g + 1) * 2 * QK_PAD]
            for h in range(2):
                s_slots[slot][2 * g + h] = _dot_nt(kb, qh[g][h])

    def update(jb, slot, stats, masked):
        start = pl.multiple_of(jb * tk, tk)
        if masked:
            k_pos = start + lax.broadcasted_iota(jnp.int32, (tk, 1), 0)
            mask = (k_pos <= q_pos) & (k_pos >= PADF)
        out = []
        for g in range(npair):
            vt = v_ref[g * LANES:(g + 1) * LANES, pl.ds(start, tk)]
            for h in range(2):
                m_prev, acc = stats[4 * g + 2 * h], stats[4 * g + 2 * h + 1]
                s = s_slots[slot][2 * g + h]
                if masked:
                    s = jnp.where(mask, s, NEG_BIG)
                m_new = jnp.maximum(m_prev, jnp.max(s, axis=0, keepdims=True))
                prob = jnp.exp2(s - m_new).astype(_BF16)
                lo = SUM_ROW[h]
                pieces = [vt[:lo]] if lo else []
                pieces += [jnp.ones((BF16_ROWS, tk), _BF16), vt[lo + BF16_ROWS:]]
                vt_h = jnp.concatenate(pieces, axis=0)
                acc = jnp.exp2(m_prev - m_new) * acc + _dot(vt_h, prob)
                out += [m_new, acc]
        return tuple(out)

    init = (jnp.full((1, tq), NEG_BIG, _F32), jnp.zeros((LANES, tq), _F32)) * (2 * npair)

    scores(0, 0)

    @pl.when(i > 0)
    def _():
        scores(1, 1)

    carry = update(0, 0, init, True)

    def two_blocks(pidx, stats):
        jb = 2 * pidx + 1
        scores(jb + 1, 0)
        stats = update(jb, 1, stats, False)
        scores(jb + 2, 1)
        return update(jb + 1, 0, stats, False)

    def tail_even_i(_, stats):
        scores(i, 0)
        stats = update(i - 1, 1, stats, False)
        return update(i, 0, stats, True)

    def tail_odd_i(_, stats):
        return update(i, 1, stats, True)

    is_even = (i % 2 == 0)
    carry = lax.fori_loop(0, (i - 1) // 2, two_blocks, carry)
    carry = lax.fori_loop(0, jnp.where((i > 0) & is_even, 1, 0), tail_even_i, carry)
    carry = lax.fori_loop(0, jnp.where(is_even, 0, 1), tail_odd_i, carry)
    for g in range(npair):
        acc_a, acc_b = carry[4 * g + 1], carry[4 * g + 3]
        o = jnp.where(first_v, acc_a / acc_a[SUM_ROW[0]:SUM_ROW[0] + 1],
                      acc_b / acc_b[SUM_ROW[1]:SUM_ROW[1] + 1])
        o_ref[:, g * LANES:(g + 1) * LANES] = jnp.transpose(o).astype(o_ref.dtype)


def _attention(q, k, v, bsz):
    r = q.shape[0]
    tp = r // bsz
    nq = tp // ATT_TILE
    ng = MLA_HEADS // 2 // ATT_PAIRS
    qw, vw = ATT_PAIRS * 2 * QK_PAD, ATT_PAIRS * 2 * V_HEAD_DIM
    return pl.pallas_call(
        _attn_kernel,
        out_shape=jax.ShapeDtypeStruct((r, MLA_WIDTH), _BF16),
        grid=(bsz, ng, nq),
        in_specs=[
            pl.BlockSpec((ATT_TILE, qw), lambda bi, pj, i: (bi * nq + i, pj)),
            pl.BlockSpec((tp, qw), lambda bi, pj, i: (bi, pj)),
            pl.BlockSpec((vw, tp), lambda bi, pj, i: (pj, bi)),
        ],
        out_specs=pl.BlockSpec((ATT_TILE, vw), lambda bi, pj, i: (bi * nq + i, pj)),
        scratch_shapes=[pltpu.VMEM((2 * ATT_PAIRS, ATT_TILE, ATT_TILE), _F32)] * 2,
        compiler_params=_cparams("parallel", "parallel", "arbitrary"),
        name="mla_attention",
    )(q, k, v)


def _merge_kernel(y_ref, bonus_ref, g_ref, ymla_ref, gate_ref, h_ref, lnx_ref, head_sum_ref,
                  wpr_ref, wpm_ref, wout_ref, ln_ref, o_ref, *, tp, alpha):
    tm = y_ref.shape[0]
    d = h_ref.shape[1]
    y = y_ref[...].astype(_F32)
    head_sum = head_sum_ref[...]
    ym = _head_sum(y, head_sum) * (1.0 / HEAD_DIM)
    yc = y - ym
    yv = _head_sum(yc * yc, head_sum) * (1.0 / HEAD_DIM)
    yn = yc * lax.rsqrt(yv + GN_EPS) * lnx_ref[0:1, :] + lnx_ref[1:2, :]
    y_rwkv = (yn.astype(_F32) + bonus_ref[...].astype(_F32)) * g_ref[...].astype(_F32)
    gates = _sigmoid(gate_ref[...].astype(_F32))
    mixed = (gates[:, :d] * _dot(y_rwkv.astype(_BF16), wpr_ref[...])
             + gates[:, d:] * _dot(ymla_ref[...], wpm_ref[...]))
    out = _dot(mixed.astype(_BF16), wout_ref[...])
    hn = _layer_norm(alpha * h_ref[...] + out, ln_ref[0:1, :], ln_ref[1:2, :])
    o_ref[...] = jnp.where(_valid_rows(pl.program_id(0), tm, tp), hn, 0.0)


def _merge(y, bonus, g, ymla, pgate, h, lnx, head_mean, wpr, wpm, wout, ln, tp, alpha):
    r, d = h.shape
    tm = ROW_TM
    w = RWKV_WIDTH
    full = lambda a: pl.BlockSpec(a.shape, lambda i: (0,) * a.ndim)
    rows = lambda n: pl.BlockSpec((tm, n), lambda i: (i, 0))
    return pl.pallas_call(
        functools.partial(_merge_kernel, tp=tp, alpha=alpha),
        out_shape=jax.ShapeDtypeStruct((r, d), _F32),
        grid=(r // tm,),
        in_specs=[rows(w), rows(w), rows(w), rows(MLA_WIDTH),
                  rows(2 * d),
                  rows(d), full(lnx), full(head_mean), full(wpr), full(wpm), full(wout), full(ln)],
        out_specs=rows(d),
        compiler_params=_cparams("parallel"),
        name="merge",
    )(y, bonus, g, ymla, pgate, h, lnx, head_mean, wpr, wpm, wout, ln)


def _ffn_kernel(h_ref, wup_ref, cw_ref, cb_ref, wdown_ref, ln_ref, o_ref, carry_ref, act_ref, *,
                tp, alpha):
    i = pl.program_id(0)
    tm, d = h_ref.shape
    dff = wdown_ref.shape[0]

    @pl.when(i == 0)
    def _():
        carry_ref[0:8, :] = jnp.zeros((8, carry_ref.shape[1]), _F32)

    h = h_ref[...]
    hb = h.astype(_BF16)
    def conv(u, c0):
        cs = slice(c0, c0 + FF_CHUNK)
        carry_ref[8:, cs] = u
        u1 = carry_ref[7:7 + tm, cs]
        u2 = carry_ref[6:6 + tm, cs]
        out = (cw_ref[2:3, cs] * u + cw_ref[1:2, cs] * u1 + cw_ref[0:1, cs] * u2 + cb_ref[:, cs])
        carry_ref[0:8, cs] = u[tm - 8:, :]
        return out

    def up(c0):
        return (_dot(hb, wup_ref[:, c0:c0 + FF_CHUNK]),
                _dot(hb, wup_ref[:, dff + c0:dff + c0 + FF_CHUNK]))

    chunks = list(range(0, dff, FF_CHUNK))
    ahead = up(chunks[0])
    for n, c0 in enumerate(chunks):
        gate_raw, val_raw = ahead
        if n + 1 < len(chunks):
            ahead = up(chunks[n + 1])
        gate = conv(gate_raw, c0)
        val = conv(val_raw, dff + c0)
        act_ref[:, c0:c0 + FF_CHUNK] = (gate * _sigmoid(gate) * val).astype(_BF16)
    acc = _dot(act_ref[...], wdown_ref[...])
    hn = _layer_norm(alpha * h + acc, ln_ref[0:1, :], ln_ref[1:2, :])
    o_ref[...] = jnp.where(_valid_rows(i, tm, tp), hn, 0.0).reshape(o_ref.shape)


def _ffn(h, wup, cw, cb, wdown, ln, tp, alpha, bsz, final):
    r, d = h.shape
    tm = FFN_TM
    full = lambda a: pl.BlockSpec(a.shape, lambda i: (0,) * a.ndim)
    if final:
        nt = tp // tm
        assert FRONT == tm
        out_shape = jax.ShapeDtypeStruct((bsz, tp - FRONT, d), _F32)
        out_spec = pl.BlockSpec((1, tm, d), lambda i: (i // nt, jnp.maximum(i % nt - 1, 0), 0))
    else:
        out_shape = jax.ShapeDtypeStruct((r, d), _F32)
        out_spec = pl.BlockSpec((tm, d), lambda i: (i, 0))
    return pl.pallas_call(
        functools.partial(_ffn_kernel, tp=tp, alpha=alpha),
        out_shape=out_shape,
        grid=(r // tm,),
        in_specs=[pl.BlockSpec((tm, d), lambda i: (i, 0)),
                  full(wup), full(cw), full(cb), full(wdown), full(ln)],
        out_specs=out_spec,
        scratch_shapes=[pltpu.VMEM((8 + tm, wup.shape[1]), _F32),
                        pltpu.VMEM((tm, wdown.shape[0]), _BF16)],
        compiler_params=_cparams("arbitrary"),
        name="conv_ffn",
    )(h, wup, cw, cb, wdown, ln)


def _pad_cols(a, n):
    return jnp.pad(a, ((0, 0), (0, n - a.shape[1])))


def _rearrange_in_cols(a):
    w = RWKV_WIDTH
    o_mla = 3 * w + DECAY_LORA + ICLR_LORA + GATE_LORA
    o_gate = o_mla + Q_LORA_RANK + KV_LORA_RANK + QK_ROPE_DIM
    rows = a.shape[0]
    z = lambda n: jnp.zeros((rows, n), a.dtype)
    kr = a[:, o_mla + Q_LORA_RANK + KV_LORA_RANK:o_gate]
    parts = [
        a[:, :3 * w + DECAY_LORA + ICLR_LORA],
        a[:, 3 * w + DECAY_LORA + ICLR_LORA:o_mla], z(GATE_LORA_PAD - GATE_LORA),
        a[:, o_mla:o_mla + Q_LORA_RANK + KV_LORA_RANK],
        z(QK_NOPE_DIM), kr, z(LANES - QK_NOPE_DIM - QK_ROPE_DIM),
        a[:, o_gate:],
    ]
    return jnp.concatenate(parts, axis=1)


def _pad_heads(wmat, per_head, lead):
    kdim = wmat.shape[0]
    wh = wmat.reshape(kdim, MLA_HEADS, per_head)
    wh = jnp.pad(wh, ((0, 0), (0, 0), (lead, QK_PAD - per_head - lead)))
    return wh.reshape(kdim, MLA_HEADS * QK_PAD)


def _rope_tables(tp, reps):
    half = QK_ROPE_DIM // 2
    pos = (jnp.arange(tp, dtype=jnp.int32) - PADF).astype(_F32)
    inv_freq = ROPE_THETA ** (-jnp.arange(half, dtype=_F32) / half)
    ang = pos[:, None] * inv_freq[None, :]
    cos, sin = jnp.cos(ang), jnp.sin(ang)
    zeros = lambda n: jnp.zeros((tp, n), _F32)
    tail = LANES - QK_NOPE_DIM - QK_ROPE_DIM
    cos_t = jnp.concatenate([jnp.ones((tp, QK_NOPE_DIM), _F32), cos, cos, zeros(tail)], axis=1)
    sin_a = jnp.concatenate([zeros(QK_NOPE_DIM), -sin, zeros(half + tail)], axis=1)
    sin_b = jnp.concatenate([zeros(QK_NOPE_DIM + half), sin, zeros(tail)], axis=1)
    tab = jnp.stack([cos_t, sin_a, sin_b])
    return jnp.tile(tab, (1, reps, 1))


def kernel(x, meta_tokens, ln_in_g, ln_in_b, w_in, mu_shift, w0, w_lora_up, a0, a_lora_up,
           g_lora_up, k_k, k_a, r_k, lnx_g, lnx_b, q_norm_g, w_uq, kv_norm_g, w_uk, w_uv,
           w_proj_rwkv, w_proj_mla, w_out, ln1_g, ln1_b, w_up, conv_w, conv_b, w_down,
           ln2_g, ln2_b):
    bsz, seq, d = x.shape
    depth = w_in.shape[0]
    tp = seq + FRONT
    r = bsz * tp
    assert tp % ATT_TILE == 0 and r % ROW_TM == 0 and r % PREP_TM == 0
    alpha = float((2 * depth) ** 0.25)
    w = RWKV_WIDTH

    meta_pad = jnp.pad(meta_tokens.astype(x.dtype), ((PADF, 0), (0, 0)))
    h = (x, meta_pad, ln_in_g[None], ln_in_b[None])

    hid = jnp.arange(MXU_DIM) // HEAD_DIM
    head_sum = (hid[:, None] == hid[None, :]).astype(_BF16)
    tt = jnp.arange(PREP_TM)
    tri = ((tt[:, None] >= tt[None, :])
           & (tt[:, None] // WKV_CHUNK == tt[None, :] // WKV_CHUNK)).astype(_BF16)
    reps = 1
    while (reps * tp) % PREP_TM:
        reps += 1
    rope_tab = _rope_tables(tp, reps)

    for l in range(depth):
        w_in_l = _rearrange_in_cols(w_in[l]).astype(_BF16)
        mu = _rearrange_in_cols(jnp.pad(mu_shift[l][None], ((0, 0), (0, w_in.shape[2] - mu_shift.shape[1]))))
        mu = mu[:, :RW_END]
        vec = jnp.stack([w0[l], a0[l], k_k[l], k_a[l]])
        wa_up = jnp.concatenate([
            jnp.concatenate([w_lora_up[l], jnp.zeros((DECAY_LORA, w), _F32)], axis=1),
            jnp.concatenate([jnp.zeros((ICLR_LORA, w), _F32), a_lora_up[l]], axis=1)], axis=0)
        g_up = jnp.pad(g_lora_up[l], ((0, GATE_LORA_PAD - GATE_LORA), (0, 0)))
        wuq = _pad_heads(w_uq[l], QK_NOPE_DIM + QK_ROPE_DIM, 0)
        wuk = _pad_heads(w_uk[l], QK_NOPE_DIM, 0)

        params = (mu, vec, r_k[l].reshape(1, w), wa_up.astype(_BF16), g_up.astype(_BF16),
                  head_sum, tri, q_norm_g[l][None], kv_norm_g[l][None],
                  wuq.astype(_BF16), wuk.astype(_BF16), w_uv[l].T.astype(_BF16))
        outs = _mixin(h, w_in_l, rope_tab, params, tp, bsz)
        if l == 0:
            h, outs = outs[0], outs[1:]
        pgate, kt, rt, kd, bd, v, pc, bonus, g, q, kk, vv = outs
        y = _wkv(kt, rt, kd, bd, v, pc, bsz)
        ymla = _attention(q, kk, vv, bsz)
        h = _merge(y, bonus, g, ymla, pgate, h, jnp.stack([lnx_g[l], lnx_b[l]]), head_sum,
                   w_proj_rwkv[l].astype(_BF16), w_proj_mla[l].astype(_BF16),
                   w_out[l].astype(_BF16), jnp.stack([ln1_g[l], ln1_b[l]]), tp, alpha)
        h = _ffn(h, w_up[l].astype(_BF16), conv_w[l], conv_b[l][None], w_down[l].astype(_BF16),
                 jnp.stack([ln2_g[l], ln2_b[l]]), tp, alpha, bsz, final=(l == depth - 1))
    return h
```

```python
import functools

import numpy as np
import jax
import jax.numpy as jnp
from jax import lax
from jax.experimental import pallas as pl
from jax.experimental.pallas import tpu as pltpu

N_META = 16
HEAD_DIM = 64
RWKV_HEADS = 8
RWKV_WIDTH = RWKV_HEADS * HEAD_DIM
DECAY_LORA = 64
ICLR_LORA = 64
GATE_LORA = 160
GN_EPS = 64e-5
MLA_HEADS = 8
QK_NOPE_DIM = 64
QK_ROPE_DIM = 32
V_HEAD_DIM = 64
Q_LORA_RANK = 256
KV_LORA_RANK = 256
ROPE_THETA = 10000.0
MLA_WIDTH = MLA_HEADS * V_HEAD_DIM
CONV_WIDTH = 3
LN_EPS = 1e-5
RMS_EPS = 1e-6

LANES = 128
MXU_DIM = 256
VMEM_LIMIT = 56 * 1024 * 1024

FRONT = MXU_DIM
PADF = FRONT - N_META
C_R, C_K, C_V = 0, RWKV_WIDTH, 2 * RWKV_WIDTH
C_WA = 3 * RWKV_WIDTH
GATE_LORA_PAD = 2 * LANES
C_G = C_WA + LANES
RW_END = C_G + GATE_LORA_PAD
C_CQ = RW_END
C_CKV = C_CQ + Q_LORA_RANK
C_KR = C_CKV + KV_LORA_RANK
C_GATE = C_KR + LANES
QK_PAD = LANES

WKV_CHUNK = 64
WKV_BATCH = 4
ATT_TILE = MXU_DIM
ATT_PAIRS = 4
PREP_TM = 256
ROW_TM = 512
FFN_TM = 256
FF_CHUNK = 256
NEG_BIG = -1e30
BF16_ROWS = 16
SUM_ROW = (V_HEAD_DIM, 0)

_F32 = jnp.float32
_BF16 = jnp.bfloat16


def _cparams(*sem):
    return pltpu.CompilerParams(dimension_semantics=sem, vmem_limit_bytes=VMEM_LIMIT)


def _dot(a, b):
    return jnp.dot(a, b, preferred_element_type=_F32)


def _dot_nt(a, b):
    return lax.dot_general(a, b, (((1,), (1,)), ((), ())), preferred_element_type=_F32)


def _dot_tn(a, b):
    return lax.dot_general(a, b, (((0,), (0,)), ((), ())), preferred_element_type=_F32)


def _split_dot(x, w_bf16, parts, left=False):
    acc = None
    rem = x
    for _ in range(parts):
        hi = rem.astype(_BF16)
        term = _dot(w_bf16, hi) if left else _dot(hi, w_bf16)
        acc = term if acc is None else acc + term
        rem = rem - hi.astype(_F32)
    return acc


def _head_sum(x, ones_bd):
    n = ones_bd.shape[0]
    xb = x.astype(_BF16)
    return jnp.concatenate([_dot(xb[:, c0:c0 + n], ones_bd) for c0 in range(0, x.shape[1], n)],
                           axis=1)


def _layer_norm(x, g, b):
    mu = jnp.mean(x, axis=-1, keepdims=True)
    xc = x - mu
    var = jnp.mean(xc * xc, axis=-1, keepdims=True)
    return xc * lax.rsqrt(var + LN_EPS) * g + b


def _sigmoid(x):
    return 1.0 / (1.0 + jnp.exp(-x))


def _valid_rows(tile_idx, tm, tp):
    row = tile_idx * tm + lax.broadcasted_iota(jnp.int32, (tm, 1), 0)
    bidx = jnp.floor((row.astype(_F32) + 0.5) * (1.0 / tp)).astype(jnp.int32)
    return (row - bidx * tp) >= PADF


def _mixin_kernel(*refs, embed, tp):
    if embed:
        x_ref, meta_ref, lng_ref, lnb_ref = refs[:4]
        refs = refs[4:]
    else:
        h_ref = refs[0]
        refs = refs[1:]
    (w_ref, rope_ref, mu_ref, vec_ref, rk_ref, wa_up_ref, g_up_ref, head_sum_ref, tri_ref,
     qg_ref, kvg_ref, wuq_ref, wuk_ref, wuv_ref) = refs[:14]
    refs = refs[14:]
    if embed:
        h_out_ref = refs[0]
        refs = refs[1:]
    (gate_ref, kt_ref, rt_ref, kd_ref, bd_ref, v_ref, pc_ref, bonus_ref, g_ref,
     q_ref, kk_ref, vv_ref, carry_ref) = refs
    i = pl.program_id(0)
    tm = kt_ref.shape[0]
    w = RWKV_WIDTH

    if embed:
        src = jnp.where(i % (tp // tm) == 0, meta_ref[...], x_ref[0])
        h = _layer_norm(src, lng_ref[...], lnb_ref[...])
        h = jnp.where(_valid_rows(i, tm, tp), h, 0.0)
        h_out_ref[...] = h
    else:
        h = h_ref[...]
    hb = h.astype(_BF16)

    @pl.when(i == 0)
    def _():
        carry_ref[...] = jnp.zeros_like(carry_ref)

    def gate_cols(c0, c1):
        gate_ref[:, c0:c1] = _dot(hb, w_ref[:, C_GATE + c0:C_GATE + c1]).astype(gate_ref.dtype)

    ngate = gate_ref.shape[1]
    gq = ngate // 4

    pr = _dot(hb, w_ref[:, :RW_END])
    gate_cols(0, gq)
    shifted = pltpu.roll(pr, 1, axis=0)
    row = lax.broadcasted_iota(jnp.int32, (tm, 1), 0)
    shifted = jnp.where(row == 0, carry_ref[7:8, :], shifted)
    carry_ref[...] = pr[tm - 8:, :]
    ps = pr + (shifted - pr) * mu_ref[...]

    r = ps[:, C_R:C_R + w]
    k = ps[:, C_K:C_K + w]
    v = ps[:, C_V:C_V + w]
    wa_in = ps[:, C_WA:C_WA + LANES]
    lane = lax.broadcasted_iota(jnp.int32, (1, LANES), 1)
    wa_in = jnp.where(lane < DECAY_LORA, jnp.tanh(wa_in), wa_in)
    wa = _dot(wa_in.astype(_BF16), wa_up_ref[...])
    w0, a0, k_k, k_a = (vec_ref[0:1, :], vec_ref[1:2, :], vec_ref[2:3, :], vec_ref[3:4, :])

    z = -(w0 + wa[:, :w])
    softplus = jnp.maximum(z, 0.0) + jnp.log1p(jnp.exp(-jnp.abs(z)))
    logw = -jnp.exp(-softplus - 0.5)
    a = _sigmoid(a0 + wa[:, w:])
    gate = _dot(_sigmoid(ps[:, C_G:C_G + GATE_LORA_PAD]).astype(_BF16), g_up_ref[...])
    gate_cols(gq, 2 * gq)

    head_sum = head_sum_ref[...]
    kk = k * k_k
    kk_ss = _head_sum(kk * kk, head_sum)
    kk = kk / jnp.maximum(jnp.sqrt(kk_ss), 1e-12)
    k = k * (1.0 + (a - 1.0) * k_a)
    b = kk * a
    bonus = _head_sum(r * k * rk_ref[...], head_sum) * v
    gate_cols(2 * gq, 3 * gq)

    cum = _split_dot(logw, tri_ref[...], 3, left=True)
    pm = _dot(hb, w_ref[:, C_CQ:C_GATE])
    gate_cols(3 * gq, ngate)
    e_neg = jnp.exp(-cum)
    kt_ref[...] = (kk * jnp.exp(cum - logw)).astype(_BF16)
    rt_ref[...] = (r * jnp.exp(cum)).astype(_BF16)
    kd_ref[...] = (k * e_neg).astype(_BF16)
    bd_ref[...] = (b * e_neg).astype(_BF16)
    v_ref[...] = v.astype(_BF16)
    bonus_ref[...] = bonus.astype(_BF16)
    g_ref[...] = gate.astype(_BF16)
    for ci in range(tm // WKV_CHUNK):
        last = (ci + 1) * WKV_CHUNK - 1
        pc_ref[ci] = jnp.exp(cum[last:last + 1, :])

    cos_t, sin_a, sin_b = rope_ref[0], rope_ref[1], rope_ref[2]

    def rope(x):
        nrep = x.shape[1] // LANES
        n = x.shape[1]
        c, sa, sb = (jnp.tile(t, (1, nrep)) if nrep > 1 else t for t in (cos_t, sin_a, sin_b))
        half = QK_ROPE_DIM // 2
        return x * c + pltpu.roll(x, n - half, axis=1) * sa + pltpu.roll(x, half, axis=1) * sb

    def rms(x, gain):
        return x * lax.rsqrt(jnp.mean(x * x, axis=-1, keepdims=True) + RMS_EPS) * gain

    cq = pm[:, :Q_LORA_RANK]
    ckv = pm[:, Q_LORA_RANK:Q_LORA_RANK + KV_LORA_RANK]
    kr = pm[:, C_KR - C_CQ:]
    qn = rms(cq, qg_ref[...]).astype(_BF16)
    scale = float(np.log2(np.e) / np.sqrt(QK_NOPE_DIM + QK_ROPE_DIM))
    q_ref[...] = (rope(_dot(qn, wuq_ref[...])) * scale).astype(_BF16)
    kvn = rms(ckv, kvg_ref[...]).astype(_BF16)
    kk_ref[...] = (_dot(kvn, wuk_ref[...]) + jnp.tile(rope(kr), (1, MLA_HEADS))).astype(_BF16)
    vv_ref[...] = _dot_nt(wuv_ref[...], kvn).astype(_BF16)


def _mixin(src, w_in, rope_tab, params, tp, bsz):
    embed = isinstance(src, tuple)
    tm = PREP_TM
    d = w_in.shape[0]
    r = bsz * tp
    nt = r // tm
    per = tp // tm
    nrope = rope_tab.shape[1] // tm
    w = RWKV_WIDTH
    nchunk = tm // WKV_CHUNK
    ngate = w_in.shape[1] - C_GATE
    full = lambda a: pl.BlockSpec(a.shape, lambda i: (0,) * a.ndim)
    row_out = lambda n: pl.BlockSpec((tm, n), lambda i: (i, 0))
    if embed:
        assert FRONT == tm
        x, meta_pad, ln_g, ln_b = src
        lead_args = [x, meta_pad, ln_g, ln_b]
        lead_specs = [pl.BlockSpec((1, tm, d), lambda i: (i // per, jnp.maximum(i % per - 1, 0), 0)),
                      full(meta_pad), full(ln_g), full(ln_b)]
        lead_out, lead_out_specs = [jax.ShapeDtypeStruct((r, d), _F32)], [row_out(d)]
    else:
        lead_args, lead_specs = [src], [row_out(d)]
        lead_out, lead_out_specs = [], []
    out_shape = (
        lead_out
        + [jax.ShapeDtypeStruct((r, ngate), _BF16)]
        + [jax.ShapeDtypeStruct((r, w), _BF16)] * 5
        + [jax.ShapeDtypeStruct((nt * nchunk, 1, w), _F32)]
        + [jax.ShapeDtypeStruct((r, w), _BF16)] * 2
        + [jax.ShapeDtypeStruct((r, MLA_HEADS * QK_PAD), _BF16)] * 2
        + [jax.ShapeDtypeStruct((MLA_WIDTH, r), _BF16)]
    )
    out_specs = (
        lead_out_specs
        + [row_out(ngate)]
        + [row_out(w)] * 5
        + [pl.BlockSpec((nchunk, 1, w), lambda i: (i, 0, 0))]
        + [row_out(w)] * 2
        + [row_out(MLA_HEADS * QK_PAD)] * 2
        + [pl.BlockSpec((MLA_WIDTH, tm), lambda i: (0, i))]
    )
    return pl.pallas_call(
        functools.partial(_mixin_kernel, embed=embed, tp=tp),
        out_shape=out_shape,
        grid=(nt,),
        in_specs=lead_specs + [full(w_in), pl.BlockSpec((3, tm, LANES), lambda i: (0, i % nrope, 0))]
        + [full(a) for a in params],
        out_specs=out_specs,
        scratch_shapes=[pltpu.VMEM((8, RW_END), _F32)],
        compiler_params=_cparams("arbitrary"),
        name="mix_in",
    )(*lead_args, w_in, rope_tab, *params)


def _wkv_kernel(kt_ref, rt_ref, kd_ref, bd_ref, v_ref, pc_ref, y_ref, h_scr):
    c = pl.program_id(1)

    @pl.when(c < PADF // WKV_CHUNK)
    def _():
        h_scr[...] = jnp.zeros_like(h_scr)
        y_ref[...] = jnp.zeros_like(y_ref)

    @pl.when(c >= PADF // WKV_CHUNK)
    def _():
        _wkv_chunk(kt_ref, rt_ref, kd_ref, bd_ref, v_ref, pc_ref, y_ref, h_scr)


def _wkv_chunk(kt_ref, rt_ref, kd_ref, bd_ref, v_ref, pc_ref, y_ref, h_scr):
    npair = RWKV_HEADS // 2
    cl = WKV_CHUNK
    n2 = 2 * cl
    lane = lax.broadcasted_iota(jnp.int32, (1, LANES), 1)
    first = lane < HEAD_DIM
    row = lax.broadcasted_iota(jnp.int32, (n2, n2), 0)
    col = lax.broadcasted_iota(jnp.int32, (n2, n2), 1)
    strict = row > col
    incl = row >= col
    eye = (row == col).astype(_F32)

    def stack(x):
        zero = jnp.zeros_like(x)
        return jnp.concatenate([jnp.where(first, x, zero), jnp.where(first, zero, x)], axis=0)

    def same_block(size):
        sh = size.bit_length() - 1
        return jnp.right_shift(row, sh) == jnp.right_shift(col, sh)

    units = [(bl, j) for bl in range(kt_ref.shape[0]) for j in range(npair)]
    each = lambda fn, *lists: [fn(*args) for args in zip(*lists)]
    bf = lambda x: x.astype(_BF16)

    def load(ref):
        return [stack(ref[bl, :, j * LANES:(j + 1) * LANES]) for bl, j in units]

    kt2, rt2, kd2, bd2, v2 = load(kt_ref), load(rt_ref), load(kd_ref), load(bd_ref), load(v_ref)
    lhs = each(lambda a, b: jnp.concatenate([a, b], axis=0), kt2, rt2)
    rhs = each(lambda a, b: jnp.concatenate([a, b], axis=0), kd2, bd2)
    gram = each(_dot_nt, lhs, rhs)
    a_kk = [bf(jnp.where(strict, g[:n2, :n2], 0.0)) for g in gram]
    a_kb = [jnp.where(strict, g[:n2, n2:], 0.0) for g in gram]
    a_out = [bf(jnp.concatenate([jnp.where(incl, g[n2:, :n2], 0.0),
                                 jnp.where(incl, -g[n2:, n2:], 0.0)], axis=1)) for g in gram]

    blk = same_block(8)
    nd = [bf(jnp.where(blk, a, 0.0)) for a in a_kb]
    s2 = [bf(x) for x in each(_dot, nd, nd)]
    i_minus = [eye - x.astype(_F32) for x in nd]
    p1 = each(lambda im, s: im + _dot(bf(im), s), i_minus, s2)
    s4 = [bf(x) for x in each(_dot, s2, s2)]
    t_inv = each(lambda p, s: p + _dot(bf(p), s), p1, s4)
    size = 8
    while size < cl:
        outer = same_block(2 * size)
        sel = outer & jnp.logical_not(blk)
        off = [bf(jnp.where(sel, a, 0.0)) for a in a_kb]
        tb = [bf(t) for t in t_inv]
        m = [bf(x) for x in each(_dot, tb, off)]
        t_inv = each(lambda t, mm, tbb: t - _dot(mm, tbb), t_inv, m, tb)
        blk = outer
        size *= 2
    tb = [bf(t) for t in t_inv]
    akv = each(_dot, a_kk, v2)

    hbd = [h_scr[i] for i in range(len(units))]
    x0 = each(lambda l, h: _dot(l, bf(h)), lhs, hbd)
    u2 = each(lambda t, x, w: bf(_dot(t, bf(x[:n2] + w))), tb, x0, akv)
    vu = each(lambda a, b: jnp.concatenate([a, b], axis=0), v2, u2)
    y2 = each(lambda x, a, z: x[n2:] + _dot(a, z), x0, a_out, vu)
    upd = each(lambda kd, bd, z: _dot_tn(jnp.concatenate([kd, -bd], axis=0), z), kd2, bd2, vu)
    for i, (bl, j) in enumerate(units):
        sl = slice(j * LANES, (j + 1) * LANES)
        y_ref[bl, :, sl] = (y2[i][:cl] + y2[i][cl:]).astype(y_ref.dtype)
        pc_row = pc_ref[bl, 0, :, sl]
        pc_col = jnp.transpose(jnp.broadcast_to(pc_row, (LANES, LANES)))
        h_scr[i] = (hbd[i] + upd[i]) * pc_col


def _wkv(kt, rt, kd, bd, v, pc, bsz):
    r, w = kt.shape
    tp = r // bsz
    nc = tp // WKV_CHUNK
    nb = WKV_BATCH
    assert bsz % nb == 0
    blk = pl.BlockSpec((nb, WKV_CHUNK, w), lambda bg, c: (bg, c, 0))
    args = [a.reshape(bsz, tp, w) for a in (kt, rt, kd, bd, v)] + [pc.reshape(bsz, nc, 1, w)]
    y = pl.pallas_call(
        _wkv_kernel,
        out_shape=jax.ShapeDtypeStruct((bsz, tp, w), _F32),
        grid=(bsz // nb, nc),
        in_specs=[blk] * 5 + [pl.BlockSpec((nb, 1, 1, w), lambda bg, c: (bg, c, 0, 0))],
        out_specs=blk,
        scratch_shapes=[pltpu.VMEM((nb * (RWKV_HEADS // 2), LANES, LANES), _F32)],
        compiler_params=_cparams("parallel", "arbitrary"),
        name="wkv7",
    )(*args)
    return y.reshape(r, w)


def _attn_kernel(q_ref, k_ref, v_ref, o_ref, s0_scr, s1_scr):
    i = pl.program_id(2)
    tq = q_ref.shape[0]
    tk = tq
    nhead = q_ref.shape[1] // QK_PAD
    q_pos = i * tq + lax.broadcasted_iota(jnp.int32, (1, tq), 1)
    first_v = lax.broadcasted_iota(jnp.int32, (LANES, 1), 0) < V_HEAD_DIM
    head_cols = lambda n: slice(n * QK_PAD, (n + 1) * QK_PAD)
    qh = [q_ref[:, head_cols(n)] for n in range(nhead)]

    s_slots = (s0_scr, s1_scr)

    def scores(jb, slot):
        start = pl.multiple_of(jb * tk, tk)
        for n in range(nhead):
            s_slots[slot][n] = _dot_nt(k_ref[pl.ds(start, tk), head_cols(n)], qh[n])

    def accumulate(n, vt, s, m_prev, acc):
        m_new = jnp.maximum(m_prev, jnp.max(s, axis=0, keepdims=True))
        prob = jnp.exp2(s - m_new).astype(_BF16)
        if prob.shape[0] < vt.shape[1]:
            prob = jnp.concatenate(
                [jnp.zeros((vt.shape[1] - prob.shape[0], tq), _BF16), prob], axis=0)
        lo = SUM_ROW[n % 2]
        pieces = [vt[:lo]] if lo else []
        pieces += [jnp.ones((BF16_ROWS, vt.shape[1]), _BF16), vt[lo + BF16_ROWS:]]
        acc = jnp.exp2(m_prev - m_new) * acc + _dot(jnp.concatenate(pieces, axis=0), prob)
        return [m_new, acc]

    def update(jb, slot, stats, masked):
        start = pl.multiple_of(jb * tk, tk)
        if masked:
            mask = (start + lax.broadcasted_iota(jnp.int32, (tk, 1), 0)) <= q_pos
        out = []
        for n in range(nhead):
            vt = v_ref[(n // 2) * LANES:(n // 2 + 1) * LANES, pl.ds(start, tk)]
            s = s_slots[slot][n]
            if masked:
                s = jnp.where(mask, s, NEG_BIG)
            out += accumulate(n, vt, s, stats[2 * n], stats[2 * n + 1])
        return tuple(out)

    def front(stats):
        mask = (PADF + lax.broadcasted_iota(jnp.int32, (N_META, 1), 0)) <= q_pos
        out = []
        for n in range(nhead):
            s = _dot_nt(k_ref[PADF:FRONT, head_cols(n)], qh[n])
            vt = v_ref[(n // 2) * LANES:(n // 2 + 1) * LANES, 0:tk]
            out += accumulate(n, vt, jnp.where(mask, s, NEG_BIG), stats[2 * n], stats[2 * n + 1])
        return tuple(out)

    init = (jnp.full((1, tq), NEG_BIG, _F32), jnp.zeros((LANES, tq), _F32)) * nhead

    @pl.when(i > 0)
    def _():
        scores(1, 1)

    carry = front(init)

    def two_blocks(pidx, stats):
        jb = 2 * pidx + 1
        scores(jb + 1, 0)
        stats = update(jb, 1, stats, False)
        scores(jb + 2, 1)
        return update(jb + 1, 0, stats, False)

    def tail_even_i(_, stats):
        scores(i, 0)
        stats = update(i - 1, 1, stats, False)
        return update(i, 0, stats, True)

    def tail_odd_i(_, stats):
        return update(i, 1, stats, True)

    is_even = (i % 2 == 0)
    carry = lax.fori_loop(0, (i - 1) // 2, two_blocks, carry)
    carry = lax.fori_loop(0, jnp.where((i > 0) & is_even, 1, 0), tail_even_i, carry)
    carry = lax.fori_loop(0, jnp.where(is_even, 0, 1), tail_odd_i, carry)
    for g in range(nhead // 2):
        acc_a, acc_b = carry[4 * g + 1], carry[4 * g + 3]
        o = jnp.where(first_v, acc_a / acc_a[SUM_ROW[0]:SUM_ROW[0] + 1],
                      acc_b / acc_b[SUM_ROW[1]:SUM_ROW[1] + 1])
        o_ref[:, g * LANES:(g + 1) * LANES] = jnp.transpose(o).astype(o_ref.dtype)


def _attention(q, k, v, bsz):
    r = q.shape[0]
    tp = r // bsz
    nq = tp // ATT_TILE
    ng = MLA_HEADS // 2 // ATT_PAIRS
    qw, vw = ATT_PAIRS * 2 * QK_PAD, ATT_PAIRS * 2 * V_HEAD_DIM
    return pl.pallas_call(
        _attn_kernel,
        out_shape=jax.ShapeDtypeStruct((r, MLA_WIDTH), _BF16),
        grid=(bsz, ng, nq),
        in_specs=[
            pl.BlockSpec((ATT_TILE, qw), lambda bi, pj, i: (bi * nq + i, pj)),
            pl.BlockSpec((tp, qw), lambda bi, pj, i: (bi, pj)),
            pl.BlockSpec((vw, tp), lambda bi, pj, i: (pj, bi)),
        ],
        out_specs=pl.BlockSpec((ATT_TILE, vw), lambda bi, pj, i: (bi * nq + i, pj)),
        scratch_shapes=[pltpu.VMEM((2 * ATT_PAIRS, ATT_TILE, ATT_TILE), _F32)] * 2,
        compiler_params=_cparams("parallel", "parallel", "arbitrary"),
        name="mla_attention",
    )(q, k, v)


def _merge_kernel(y_ref, bonus_ref, g_ref, ymla_ref, gate_ref, h_ref, lnx_ref, head_sum_ref,
                  wpr_ref, wpm_ref, wout_ref, ln_ref, o_ref, *, tp, alpha):
    tm = y_ref.shape[0]
    d = h_ref.shape[1]
    y = y_ref[...].astype(_F32)
    head_sum = head_sum_ref[...]
    ym = _head_sum(y, head_sum) * (1.0 / HEAD_DIM)
    yc = y - ym
    yv = _head_sum(yc * yc, head_sum) * (1.0 / HEAD_DIM)
    yn = yc * lax.rsqrt(yv + GN_EPS) * lnx_ref[0:1, :] + lnx_ref[1:2, :]
    y_rwkv = (yn.astype(_F32) + bonus_ref[...].astype(_F32)) * g_ref[...].astype(_F32)
    gates = _sigmoid(gate_ref[...].astype(_F32))
    mixed = (gates[:, :d] * _dot(y_rwkv.astype(_BF16), wpr_ref[...])
             + gates[:, d:] * _dot(ymla_ref[...], wpm_ref[...]))
    out = _dot(mixed.astype(_BF16), wout_ref[...])
    hn = _layer_norm(alpha * h_ref[...] + out, ln_ref[0:1, :], ln_ref[1:2, :])
    o_ref[...] = jnp.where(_valid_rows(pl.program_id(0), tm, tp), hn, 0.0)


def _merge(y, bonus, g, ymla, pgate, h, lnx, head_mean, wpr, wpm, wout, ln, tp, alpha):
    r, d = h.shape
    tm = ROW_TM
    w = RWKV_WIDTH
    full = lambda a: pl.BlockSpec(a.shape, lambda i: (0,) * a.ndim)
    rows = lambda n: pl.BlockSpec((tm, n), lambda i: (i, 0))
    return pl.pallas_call(
        functools.partial(_merge_kernel, tp=tp, alpha=alpha),
        out_shape=jax.ShapeDtypeStruct((r, d), _F32),
        grid=(r // tm,),
        in_specs=[rows(w), rows(w), rows(w), rows(MLA_WIDTH),
                  rows(2 * d),
                  rows(d), full(lnx), full(head_mean), full(wpr), full(wpm), full(wout), full(ln)],
        out_specs=rows(d),
        compiler_params=_cparams("parallel"),
        name="merge",
    )(y, bonus, g, ymla, pgate, h, lnx, head_mean, wpr, wpm, wout, ln)


def _ffn_kernel(h_ref, wup_ref, cw_ref, cb_ref, wdown_ref, ln_ref, o_ref, carry_ref, act_ref, *,
                tp, alpha):
    i = pl.program_id(0)
    tm, d = h_ref.shape
    dff = wdown_ref.shape[0]

    @pl.when(i == 0)
    def _():
        carry_ref[0:8, :] = jnp.zeros((8, carry_ref.shape[1]), _F32)

    h = h_ref[...]
    hb = h.astype(_BF16)
    def conv(u, c0):
        cs = slice(c0, c0 + FF_CHUNK)
        carry_ref[8:, cs] = u
        u1 = carry_ref[7:7 + tm, cs]
        u2 = carry_ref[6:6 + tm, cs]
        out = (cw_ref[2:3, cs] * u + cw_ref[1:2, cs] * u1 + cw_ref[0:1, cs] * u2 + cb_ref[:, cs])
        carry_ref[0:8, cs] = u[tm - 8:, :]
        return out

    def up(c0):
        return (_dot(hb, wup_ref[:, c0:c0 + FF_CHUNK]),
                _dot(hb, wup_ref[:, dff + c0:dff + c0 + FF_CHUNK]))

    chunks = list(range(0, dff, FF_CHUNK))
    ahead = up(chunks[0])
    for n, c0 in enumerate(chunks):
        gate_raw, val_raw = ahead
        if n + 1 < len(chunks):
            ahead = up(chunks[n + 1])
        gate = conv(gate_raw, c0)
        val = conv(val_raw, dff + c0)
        act_ref[:, c0:c0 + FF_CHUNK] = (gate * _sigmoid(gate) * val).astype(_BF16)
    acc = _dot(act_ref[...], wdown_ref[...])
    hn = _layer_norm(alpha * h + acc, ln_ref[0:1, :], ln_ref[1:2, :])
    o_ref[...] = jnp.where(_valid_rows(i, tm, tp), hn, 0.0).reshape(o_ref.shape)


def _ffn(h, wup, cw, cb, wdown, ln, tp, alpha, bsz, final):
    r, d = h.shape
    tm = FFN_TM
    full = lambda a: pl.BlockSpec(a.shape, lambda i: (0,) * a.ndim)
    if final:
        nt = tp // tm
        assert FRONT == tm
        out_shape = jax.ShapeDtypeStruct((bsz, tp - FRONT, d), _F32)
        out_spec = pl.BlockSpec((1, tm, d), lambda i: (i // nt, jnp.maximum(i % nt - 1, 0), 0))
    else:
        out_shape = jax.ShapeDtypeStruct((r, d), _F32)
        out_spec = pl.BlockSpec((tm, d), lambda i: (i, 0))
    return pl.pallas_call(
        functools.partial(_ffn_kernel, tp=tp, alpha=alpha),
        out_shape=out_shape,
        grid=(r // tm,),
        in_specs=[pl.BlockSpec((tm, d), lambda i: (i, 0)),
                  full(wup), full(cw), full(cb), full(wdown), full(ln)],
        out_specs=out_spec,
        scratch_shapes=[pltpu.VMEM((8 + tm, wup.shape[1]), _F32),
                        pltpu.VMEM((tm, wdown.shape[0]), _BF16)],
        compiler_params=_cparams("arbitrary"),
        name="conv_ffn",
    )(h, wup, cw, cb, wdown, ln)


def _pad_cols(a, n):
    return jnp.pad(a, ((0, 0), (0, n - a.shape[1])))


def _rearrange_in_cols(a):
    w = RWKV_WIDTH
    o_mla = 3 * w + DECAY_LORA + ICLR_LORA + GATE_LORA
    o_gate = o_mla + Q_LORA_RANK + KV_LORA_RANK + QK_ROPE_DIM
    rows = a.shape[0]
    z = lambda n: jnp.zeros((rows, n), a.dtype)
    kr = a[:, o_mla + Q_LORA_RANK + KV_LORA_RANK:o_gate]
    parts = [
        a[:, :3 * w + DECAY_LORA + ICLR_LORA],
        a[:, 3 * w + DECAY_LORA + ICLR_LORA:o_mla], z(GATE_LORA_PAD - GATE_LORA),
        a[:, o_mla:o_mla + Q_LORA_RANK + KV_LORA_RANK],
        z(QK_NOPE_DIM), kr, z(LANES - QK_NOPE_DIM - QK_ROPE_DIM),
        a[:, o_gate:],
    ]
    return jnp.concatenate(parts, axis=1)


def _pad_heads(wmat, per_head, lead):
    kdim = wmat.shape[0]
    wh = wmat.reshape(kdim, MLA_HEADS, per_head)
    wh = jnp.pad(wh, ((0, 0), (0, 0), (lead, QK_PAD - per_head - lead)))
    return wh.reshape(kdim, MLA_HEADS * QK_PAD)


def _rope_tables(tp, reps):
    half = QK_ROPE_DIM // 2
    pos = (jnp.arange(tp, dtype=jnp.int32) - PADF).astype(_F32)
    inv_freq = ROPE_THETA ** (-jnp.arange(half, dtype=_F32) / half)
    ang = pos[:, None] * inv_freq[None, :]
    cos, sin = jnp.cos(ang), jnp.sin(ang)
    zeros = lambda n: jnp.zeros((tp, n), _F32)
    tail = LANES - QK_NOPE_DIM - QK_ROPE_DIM
    cos_t = jnp.concatenate([jnp.ones((tp, QK_NOPE_DIM), _F32), cos, cos, zeros(tail)], axis=1)
    sin_a = jnp.concatenate([zeros(QK_NOPE_DIM), -sin, zeros(half + tail)], axis=1)
    sin_b = jnp.concatenate([zeros(QK_NOPE_DIM + half), sin, zeros(tail)], axis=1)
    tab = jnp.stack([cos_t, sin_a, sin_b])
    return jnp.tile(tab, (1, reps, 1))


def kernel(x, meta_tokens, ln_in_g, ln_in_b, w_in, mu_shift, w0, w_lora_up, a0, a_lora_up,
           g_lora_up, k_k, k_a, r_k, lnx_g, lnx_b, q_norm_g, w_uq, kv_norm_g, w_uk, w_uv,
           w_proj_rwkv, w_proj_mla, w_out, ln1_g, ln1_b, w_up, conv_w, conv_b, w_down,
           ln2_g, ln2_b):
    bsz, seq, d = x.shape
    depth = w_in.shape[0]
    tp = seq + FRONT
    r = bsz * tp
    assert tp % ATT_TILE == 0 and r % ROW_TM == 0 and r % PREP_TM == 0
    alpha = float((2 * depth) ** 0.25)
    w = RWKV_WIDTH

    meta_pad = jnp.pad(meta_tokens.astype(x.dtype), ((PADF, 0), (0, 0)))
    h = (x, meta_pad, ln_in_g[None], ln_in_b[None])

    hid = jnp.arange(MXU_DIM) // HEAD_DIM
    head_sum = (hid[:, None] == hid[None, :]).astype(_BF16)
    tt = jnp.arange(PREP_TM)
    tri = ((tt[:, None] >= tt[None, :])
           & (tt[:, None] // WKV_CHUNK == tt[None, :] // WKV_CHUNK)).astype(_BF16)
    reps = 1
    while (reps * tp) % PREP_TM:
        reps += 1
    rope_tab = _rope_tables(tp, reps)

    for l in range(depth):
        w_in_l = _rearrange_in_cols(w_in[l]).astype(_BF16)
        mu = _rearrange_in_cols(jnp.pad(mu_shift[l][None], ((0, 0), (0, w_in.shape[2] - mu_shift.shape[1]))))
        mu = mu[:, :RW_END]
        vec = jnp.stack([w0[l], a0[l], k_k[l], k_a[l]])
        wa_up = jnp.concatenate([
            jnp.concatenate([w_lora_up[l], jnp.zeros((DECAY_LORA, w), _F32)], axis=1),
            jnp.concatenate([jnp.zeros((ICLR_LORA, w), _F32), a_lora_up[l]], axis=1)], axis=0)
        g_up = jnp.pad(g_lora_up[l], ((0, GATE_LORA_PAD - GATE_LORA), (0, 0)))
        wuq = _pad_heads(w_uq[l], QK_NOPE_DIM + QK_ROPE_DIM, 0)
        wuk = _pad_heads(w_uk[l], QK_NOPE_DIM, 0)

        params = (mu, vec, r_k[l].reshape(1, w), wa_up.astype(_BF16), g_up.astype(_BF16),
                  head_sum, tri, q_norm_g[l][None], kv_norm_g[l][None],
                  wuq.astype(_BF16), wuk.astype(_BF16), w_uv[l].T.astype(_BF16))
        outs = _mixin(h, w_in_l, rope_tab, params, tp, bsz)
        if l == 0:
            h, outs = outs[0], outs[1:]
        pgate, kt, rt, kd, bd, v, pc, bonus, g, q, kk, vv = outs
        y = _wkv(kt, rt, kd, bd, v, pc, bsz)
        ymla = _attention(q, kk, vv, bsz)
        h = _merge(y, bonus, g, ymla, pgate, h, jnp.stack([lnx_g[l], lnx_b[l]]), head_sum,
                   w_proj_rwkv[l].astype(_BF16), w_proj_mla[l].astype(_BF16),
                   w_out[l].astype(_BF16), jnp.stack([ln1_g[l], ln1_b[l]]), tp, alpha)
        h = _ffn(h, w_up[l].astype(_BF16), conv_w[l], conv_b[l][None], w_down[l].astype(_BF16),
                 jnp.stack([ln2_g[l], ln2_b[l]]), tp, alpha, bsz, final=(l == depth - 1))
    return h
```

```python
import functools

import numpy as np
import jax
import jax.numpy as jnp
from jax import lax
from jax.experimental import pallas as pl
from jax.experimental.pallas import tpu as pltpu

N_META = 16
HEAD_DIM = 64
RWKV_HEADS = 8
RWKV_WIDTH = RWKV_HEADS * HEAD_DIM
DECAY_LORA = 64
ICLR_LORA = 64
GATE_LORA = 160
GN_EPS = 64e-5
MLA_HEADS = 8
QK_NOPE_DIM = 64
QK_ROPE_DIM = 32
V_HEAD_DIM = 64
Q_LORA_RANK = 256
KV_LORA_RANK = 256
ROPE_THETA = 10000.0
MLA_WIDTH = MLA_HEADS * V_HEAD_DIM
CONV_WIDTH = 3
LN_EPS = 1e-5
RMS_EPS = 1e-6

LANES = 128
MXU_DIM = 256
VMEM_LIMIT = 56 * 1024 * 1024

FRONT = MXU_DIM
PADF = FRONT - N_META
C_R, C_K, C_V = 0, RWKV_WIDTH, 2 * RWKV_WIDTH
C_WA = 3 * RWKV_WIDTH
GATE_LORA_PAD = 2 * LANES
C_G = C_WA + LANES
RW_END = C_G + GATE_LORA_PAD
C_CQ = RW_END
C_CKV = C_CQ + Q_LORA_RANK
C_KR = C_CKV + KV_LORA_RANK
C_GATE = C_KR + LANES
QK_PAD = LANES

WKV_CHUNK = 64
WKV_BATCH = 4
ATT_TILE = MXU_DIM
ATT_PAIRS = 4
PREP_TM = 256
PREP_TM_LATER = 256
ROW_TM = 512
FFN_TM = 256
MERGE_SUB = 256
FF_CHUNK = 256
NEG_BIG = -1e30
BF16_ROWS = 16
SUM_ROW = (V_HEAD_DIM, 0)

_F32 = jnp.float32
_BF16 = jnp.bfloat16


def _cparams(*sem):
    return pltpu.CompilerParams(dimension_semantics=sem, vmem_limit_bytes=VMEM_LIMIT)


def _dot(a, b):
    return jnp.dot(a, b, preferred_element_type=_F32)


def _dot_nt(a, b):
    return lax.dot_general(a, b, (((1,), (1,)), ((), ())), preferred_element_type=_F32)


def _dot_tn(a, b):
    return lax.dot_general(a, b, (((0,), (0,)), ((), ())), preferred_element_type=_F32)


def _split_dot(x, w_bf16, parts, left=False):
    acc = None
    rem = x
    for _ in range(parts):
        hi = rem.astype(_BF16)
        term = _dot(w_bf16, hi) if left else _dot(hi, w_bf16)
        acc = term if acc is None else acc + term
        rem = rem - hi.astype(_F32)
    return acc


def _head_sum(x, ones_bd):
    n = ones_bd.shape[0]
    xb = x.astype(_BF16)
    return jnp.concatenate([_dot(xb[:, c0:c0 + n], ones_bd) for c0 in range(0, x.shape[1], n)],
                           axis=1)


def _layer_norm(x, g, b):
    mu = jnp.mean(x, axis=-1, keepdims=True)
    xc = x - mu
    var = jnp.mean(xc * xc, axis=-1, keepdims=True)
    return xc * lax.rsqrt(var + LN_EPS) * g + b


def _sigmoid(x):
    return 1.0 / (1.0 + jnp.exp(-x))


def _valid_rows(tile_idx, tm, tp):
    row = tile_idx * tm + lax.broadcasted_iota(jnp.int32, (tm, 1), 0)
    bidx = jnp.floor((row.astype(_F32) + 0.5) * (1.0 / tp)).astype(jnp.int32)
    return (row - bidx * tp) >= PADF


def _mixin_kernel(*refs, embed, tp):
    if embed:
        x_ref, meta_ref, lng_ref, lnb_ref = refs[:4]
        refs = refs[4:]
    else:
        h_ref = refs[0]
        refs = refs[1:]
    (w_ref, rope_ref, mu_ref, vec_ref, rk_ref, wa_up_ref, g_up_ref, head_sum_ref, tri_ref,
     qg_ref, kvg_ref, wuq_ref, wuk_ref, wuv_ref) = refs[:14]
    refs = refs[14:]
    if embed:
        h_out_ref = refs[0]
        refs = refs[1:]
    (gate_ref, kt_ref, rt_ref, kd_ref, bd_ref, v_ref, pc_ref, bonus_ref, g_ref,
     q_ref, kk_ref, vv_ref, carry_ref) = refs
    i = pl.program_id(0)
    tm = kt_ref.shape[0]
    w = RWKV_WIDTH

    if embed:
        src = jnp.where(i % (tp // tm) == 0, meta_ref[...], x_ref[0])
        h = _layer_norm(src, lng_ref[...], lnb_ref[...])
        h = jnp.where(_valid_rows(i, tm, tp), h, 0.0)
        h_out_ref[...] = h
    else:
        h = h_ref[...]
    hb = h.astype(_BF16)

    @pl.when(i == 0)
    def _():
        carry_ref[...] = jnp.zeros_like(carry_ref)

    def gate_cols(c0, c1):
        gate_ref[:, c0:c1] = _dot(hb, w_ref[:, C_GATE + c0:C_GATE + c1]).astype(gate_ref.dtype)

    ngate = gate_ref.shape[1]
    gq = ngate // 4

    pr = _dot(hb, w_ref[:, :RW_END])
    gate_cols(0, gq)
    shifted = pltpu.roll(pr, 1, axis=0)
    row = lax.broadcasted_iota(jnp.int32, (tm, 1), 0)
    shifted = jnp.where(row == 0, carry_ref[7:8, :], shifted)
    carry_ref[...] = pr[tm - 8:, :]
    ps = pr + (shifted - pr) * mu_ref[...]

    r = ps[:, C_R:C_R + w]
    k = ps[:, C_K:C_K + w]
    v = ps[:, C_V:C_V + w]
    wa_in = ps[:, C_WA:C_WA + LANES]
    lane = lax.broadcasted_iota(jnp.int32, (1, LANES), 1)
    wa_in = jnp.where(lane < DECAY_LORA, jnp.tanh(wa_in), wa_in)
    wa = _dot(wa_in.astype(_BF16), wa_up_ref[...])
    w0, a0, k_k, k_a = (vec_ref[0:1, :], vec_ref[1:2, :], vec_ref[2:3, :], vec_ref[3:4, :])

    z = -(w0 + wa[:, :w])
    softplus = jnp.maximum(z, 0.0) + jnp.log1p(jnp.exp(-jnp.abs(z)))
    logw = -jnp.exp(-softplus - 0.5)
    a = _sigmoid(a0 + wa[:, w:])
    gate = _dot(_sigmoid(ps[:, C_G:C_G + GATE_LORA_PAD]).astype(_BF16), g_up_ref[...])
    gate_cols(gq, 2 * gq)

    head_sum = head_sum_ref[...]
    kk = k * k_k
    kk_ss = _head_sum(kk * kk, head_sum)
    kk = kk / jnp.maximum(jnp.sqrt(kk_ss), 1e-12)
    k = k * (1.0 + (a - 1.0) * k_a)
    b = kk * a
    bonus = _head_sum(r * k * rk_ref[...], head_sum) * v
    gate_cols(2 * gq, 3 * gq)

    tri = tri_ref[...]
    tn = tri.shape[0]
    cum = jnp.concatenate([_split_dot(logw[r0:r0 + tn], tri, 3, left=True)
                           for r0 in range(0, tm, tn)], axis=0)
    pm = _dot(hb, w_ref[:, C_CQ:C_GATE])
    e_neg = jnp.exp(-cum)
    kt_ref[...] = (kk * jnp.exp(cum - logw)).astype(_BF16)
    rt_ref[...] = (r * jnp.exp(cum)).astype(_BF16)
    kd_ref[...] = (k * e_neg).astype(_BF16)
    bd_ref[...] = (b * e_neg).astype(_BF16)
    v_ref[...] = v.astype(_BF16)
    bonus_ref[...] = bonus.astype(_BF16)
    g_ref[...] = gate.astype(_BF16)
    for ci in range(tm // WKV_CHUNK):
        last = (ci + 1) * WKV_CHUNK - 1
        pc_ref[ci] = jnp.exp(cum[last:last + 1, :])

    cos_t, sin_a, sin_b = rope_ref[0], rope_ref[1], rope_ref[2]

    def rope(x):
        nrep = x.shape[1] // LANES
        n = x.shape[1]
        c, sa, sb = (jnp.tile(t, (1, nrep)) if nrep > 1 else t for t in (cos_t, sin_a, sin_b))
        half = QK_ROPE_DIM // 2
        return x * c + pltpu.roll(x, n - half, axis=1) * sa + pltpu.roll(x, half, axis=1) * sb

    def rms(x, gain):
        return x * lax.rsqrt(jnp.mean(x * x, axis=-1, keepdims=True) + RMS_EPS) * gain

    cq = pm[:, :Q_LORA_RANK]
    ckv = pm[:, Q_LORA_RANK:Q_LORA_RANK + KV_LORA_RANK]
    kr = pm[:, C_KR - C_CQ:]
    qn = rms(cq, qg_ref[...]).astype(_BF16)
    scale = float(np.log2(np.e) / np.sqrt(QK_NOPE_DIM + QK_ROPE_DIM))
    q_ref[...] = (rope(_dot(qn, wuq_ref[...])) * scale).astype(_BF16)
    kvn = rms(ckv, kvg_ref[...]).astype(_BF16)
    gate_cols(3 * gq, ngate)
    kk_ref[...] = (_dot(kvn, wuk_ref[...]) + jnp.tile(rope(kr), (1, MLA_HEADS))).astype(_BF16)
    vv_ref[...] = _dot_nt(wuv_ref[...], kvn).astype(_BF16)


def _mixin(src, w_in, rope_tab, params, tp, bsz, tm):
    embed = isinstance(src, tuple)
    d = w_in.shape[0]
    r = bsz * tp
    nt = r // tm
    per = tp // tm
    nrope = rope_tab.shape[1] // tm
    w = RWKV_WIDTH
    nchunk = tm // WKV_CHUNK
    ngate = w_in.shape[1] - C_GATE
    full = lambda a: pl.BlockSpec(a.shape, lambda i: (0,) * a.ndim)
    row_out = lambda n: pl.BlockSpec((tm, n), lambda i: (i, 0))
    if embed:
        assert FRONT == tm
        x, meta_pad, ln_g, ln_b = src
        lead_args = [x, meta_pad, ln_g, ln_b]
        lead_specs = [pl.BlockSpec((1, tm, d), lambda i: (i // per, jnp.maximum(i % per - 1, 0), 0)),
                      full(meta_pad), full(ln_g), full(ln_b)]
        lead_out, lead_out_specs = [jax.ShapeDtypeStruct((r, d), _F32)], [row_out(d)]
    else:
        lead_args, lead_specs = [src], [row_out(d)]
        lead_out, lead_out_specs = [], []
    out_shape = (
        lead_out
        + [jax.ShapeDtypeStruct((r, ngate), _BF16)]
        + [jax.ShapeDtypeStruct((r, w), _BF16)] * 5
        + [jax.ShapeDtypeStruct((nt * nchunk, 1, w), _F32)]
        + [jax.ShapeDtypeStruct((r, w), _BF16)] * 2
        + [jax.ShapeDtypeStruct((r, MLA_HEADS * QK_PAD), _BF16)] * 2
        + [jax.ShapeDtypeStruct((MLA_WIDTH, r), _BF16)]
    )
    out_specs = (
        lead_out_specs
        + [row_out(ngate)]
        + [row_out(w)] * 5
        + [pl.BlockSpec((nchunk, 1, w), lambda i: (i, 0, 0))]
        + [row_out(w)] * 2
        + [row_out(MLA_HEADS * QK_PAD)] * 2
        + [pl.BlockSpec((MLA_WIDTH, tm), lambda i: (0, i))]
    )
    return pl.pallas_call(
        functools.partial(_mixin_kernel, embed=embed, tp=tp),
        out_shape=out_shape,
        grid=(nt,),
        in_specs=lead_specs + [full(w_in), pl.BlockSpec((3, tm, LANES), lambda i: (0, i % nrope, 0))]
        + [full(a) for a in params],
        out_specs=out_specs,
        scratch_shapes=[pltpu.VMEM((8, RW_END), _F32)],
        compiler_params=_cparams("arbitrary"),
        name="mix_in",
    )(*lead_args, w_in, rope_tab, *params)


def _wkv_kernel(kt_ref, rt_ref, kd_ref, bd_ref, v_ref, pc_ref, y_ref, h_scr):
    c = pl.program_id(1)

    @pl.when(c < PADF // WKV_CHUNK)
    def _():
        h_scr[...] = jnp.zeros_like(h_scr)
        y_ref[...] = jnp.zeros_like(y_ref)

    @pl.when(c >= PADF // WKV_CHUNK)
    def _():
        _wkv_chunk(kt_ref, rt_ref, kd_ref, bd_ref, v_ref, pc_ref, y_ref, h_scr)


def _wkv_chunk(kt_ref, rt_ref, kd_ref, bd_ref, v_ref, pc_ref, y_ref, h_scr):
    npair = RWKV_HEADS // 2
    cl = WKV_CHUNK
    n2 = 2 * cl
    lane = lax.broadcasted_iota(jnp.int32, (1, LANES), 1)
    first = lane < HEAD_DIM
    row = lax.broadcasted_iota(jnp.int32, (n2, n2), 0)
    col = lax.broadcasted_iota(jnp.int32, (n2, n2), 1)
    strict = row > col
    incl = row >= col
    eye = (row == col).astype(_F32)

    def stack(x):
        zero = jnp.zeros_like(x)
        return jnp.concatenate([jnp.where(first, x, zero), jnp.where(first, zero, x)], axis=0)

    def same_block(size):
        sh = size.bit_length() - 1
        return jnp.right_shift(row, sh) == jnp.right_shift(col, sh)

    units = [(bl, j) for bl in range(kt_ref.shape[0]) for j in range(npair)]
    each = lambda fn, *lists: [fn(*args) for args in zip(*lists)]
    bf = lambda x: x.astype(_BF16)

    def load(ref):
        return [stack(ref[bl, :, j * LANES:(j + 1) * LANES]) for bl, j in units]

    kt2, rt2, kd2, bd2, v2 = load(kt_ref), load(rt_ref), load(kd_ref), load(bd_ref), load(v_ref)
    lhs = each(lambda a, b: jnp.concatenate([a, b], axis=0), kt2, rt2)
    rhs = each(lambda a, b: jnp.concatenate([a, b], axis=0), kd2, bd2)
    gram = each(_dot_nt, lhs, rhs)
    a_kk = [bf(jnp.where(strict, g[:n2, :n2], 0.0)) for g in gram]
    a_kb = [jnp.where(strict, g[:n2, n2:], 0.0) for g in gram]
    a_out = [bf(jnp.concatenate([jnp.where(incl, g[n2:, :n2], 0.0),
                                 jnp.where(incl, -g[n2:, n2:], 0.0)], axis=1)) for g in gram]

    blk = same_block(8)
    nd = [bf(jnp.where(blk, a, 0.0)) for a in a_kb]
    s2 = [bf(x) for x in each(_dot, nd, nd)]
    i_minus = [eye - x.astype(_F32) for x in nd]
    p1 = each(lambda im, s: im + _dot(bf(im), s), i_minus, s2)
    s4 = [bf(x) for x in each(_dot, s2, s2)]
    t_inv = each(lambda p, s: p + _dot(bf(p), s), p1, s4)
    size = 8
    while size < cl:
        outer = same_block(2 * size)
        sel = outer & jnp.logical_not(blk)
        off = [bf(jnp.where(sel, a, 0.0)) for a in a_kb]
        tb = [bf(t) for t in t_inv]
        m = [bf(x) for x in each(_dot, tb, off)]
        t_inv = each(lambda t, mm, tbb: t - _dot(mm, tbb), t_inv, m, tb)
        blk = outer
        size *= 2
    tb = [bf(t) for t in t_inv]
    akv = each(_dot, a_kk, v2)

    hbd = [h_scr[i] for i in range(len(units))]
    x0 = each(lambda l, h: _dot(l, bf(h)), lhs, hbd)
    u2 = each(lambda t, x, w: bf(_dot(t, bf(x[:n2] + w))), tb, x0, akv)
    vu = each(lambda a, b: jnp.concatenate([a, b], axis=0), v2, u2)
    y2 = each(lambda x, a, z: x[n2:] + _dot(a, z), x0, a_out, vu)
    upd = each(lambda kd, bd, z: _dot_tn(jnp.concatenate([kd, -bd], axis=0), z), kd2, bd2, vu)
    for i, (bl, j) in enumerate(units):
        sl = slice(j * LANES, (j + 1) * LANES)
        y_ref[bl, :, sl] = (y2[i][:cl] + y2[i][cl:]).astype(y_ref.dtype)
        pc_row = pc_ref[bl, 0, :, sl]
        pc_col = jnp.transpose(jnp.broadcast_to(pc_row, (LANES, LANES)))
        h_scr[i] = (hbd[i] + upd[i]) * pc_col


def _wkv(kt, rt, kd, bd, v, pc, bsz):
    r, w = kt.shape
    tp = r // bsz
    nc = tp // WKV_CHUNK
    nb = WKV_BATCH
    assert bsz % nb == 0
    blk = pl.BlockSpec((nb, WKV_CHUNK, w), lambda bg, c: (bg, c, 0))
    args = [a.reshape(bsz, tp, w) for a in (kt, rt, kd, bd, v)] + [pc.reshape(bsz, nc, 1, w)]
    y = pl.pallas_call(
        _wkv_kernel,
        out_shape=jax.ShapeDtypeStruct((bsz, tp, w), _F32),
        grid=(bsz // nb, nc),
        in_specs=[blk] * 5 + [pl.BlockSpec((nb, 1, 1, w), lambda bg, c: (bg, c, 0, 0))],
        out_specs=blk,
        scratch_shapes=[pltpu.VMEM((nb * (RWKV_HEADS // 2), LANES, LANES), _F32)],
        compiler_params=_cparams("parallel", "arbitrary"),
        name="wkv7",
    )(*args)
    return y.reshape(r, w)


def _attn_kernel(q_ref, k_ref, v_ref, o_ref, s0_scr, s1_scr):
    i = pl.program_id(2)
    tq = q_ref.shape[0]
    tk = tq
    nhead = q_ref.shape[1] // QK_PAD
    q_pos = i * tq + lax.broadcasted_iota(jnp.int32, (1, tq), 1)
    first_v = lax.broadcasted_iota(jnp.int32, (LANES, 1), 0) < V_HEAD_DIM
    head_cols = lambda n: slice(n * QK_PAD, (n + 1) * QK_PAD)
    qh = [q_ref[:, head_cols(n)] for n in range(nhead)]

    s_slots = (s0_scr, s1_scr)

    def scores(jb, slot):
        start = pl.multiple_of(jb * tk, tk)
        for n in range(nhead):
            s_slots[slot][n] = _dot_nt(k_ref[pl.ds(start, tk), head_cols(n)], qh[n])

    def accumulate(n, vt, s, m_prev, acc):
        m_new = jnp.maximum(m_prev, jnp.max(s, axis=0, keepdims=True))
        prob = jnp.exp2(s - m_new).astype(_BF16)
        if prob.shape[0] < vt.shape[1]:
            prob = jnp.concatenate(
                [jnp.zeros((vt.shape[1] - prob.shape[0], tq), _BF16), prob], axis=0)
        lo = SUM_ROW[n % 2]
        pieces = [vt[:lo]] if lo else []
        pieces += [jnp.ones((BF16_ROWS, vt.shape[1]), _BF16), vt[lo + BF16_ROWS:]]
        acc = jnp.exp2(m_prev - m_new) * acc + _dot(jnp.concatenate(pieces, axis=0), prob)
        return [m_new, acc]

    def update(jb, slot, stats, masked):
        start = pl.multiple_of(jb * tk, tk)
        if masked:
            mask = (start + lax.broadcasted_iota(jnp.int32, (tk, 1), 0)) <= q_pos
        out = []
        for n in range(nhead):
            vt = v_ref[(n // 2) * LANES:(n // 2 + 1) * LANES, pl.ds(start, tk)]
            s = s_slots[slot][n]
            if masked:
                s = jnp.where(mask, s, NEG_BIG)
            out += accumulate(n, vt, s, stats[2 * n], stats[2 * n + 1])
        return tuple(out)

    def front(s_front, stats):
        mask = (PADF + lax.broadcasted_iota(jnp.int32, (N_META, 1), 0)) <= q_pos
        out = []
        for n in range(nhead):
            vt = v_ref[(n // 2) * LANES:(n // 2 + 1) * LANES, 0:tk]
            out += accumulate(n, vt, jnp.where(mask, s_front[n], NEG_BIG),
                              stats[2 * n], stats[2 * n + 1])
        return tuple(out)

    init = (jnp.full((1, tq), NEG_BIG, _F32), jnp.zeros((LANES, tq), _F32)) * nhead

    s_front = [_dot_nt(k_ref[PADF:FRONT, head_cols(n)], qh[n]) for n in range(nhead)]
    scores(1, 1)
    carry = front(s_front, init)

    def two_blocks(pidx, stats):
        jb = 2 * pidx + 1
        scores(jb + 1, 0)
        stats = update(jb, 1, stats, False)
        scores(jb + 2, 1)
        return update(jb + 1, 0, stats, False)

    def tail_even_i(_, stats):
        scores(i, 0)
        stats = update(i - 1, 1, stats, False)
        return update(i, 0, stats, True)

    def tail_odd_i(_, stats):
        return update(i, 1, stats, True)

    is_even = (i % 2 == 0)
    carry = lax.fori_loop(0, (i - 1) // 2, two_blocks, carry)
    carry = lax.fori_loop(0, jnp.where((i > 0) & is_even, 1, 0), tail_even_i, carry)
    carry = lax.fori_loop(0, jnp.where(is_even, 0, 1), tail_odd_i, carry)
    for g in range(nhead // 2):
        acc_a, acc_b = carry[4 * g + 1], carry[4 * g + 3]
        o = jnp.where(first_v, acc_a / acc_a[SUM_ROW[0]:SUM_ROW[0] + 1],
                      acc_b / acc_b[SUM_ROW[1]:SUM_ROW[1] + 1])
        o_ref[:, g * LANES:(g + 1) * LANES] = jnp.transpose(o).astype(o_ref.dtype)


def _attention(q, k, v, bsz):
    r = q.shape[0]
    tp = r // bsz
    nq = tp // ATT_TILE
    ng = MLA_HEADS // 2 // ATT_PAIRS
    qw, vw = ATT_PAIRS * 2 * QK_PAD, ATT_PAIRS * 2 * V_HEAD_DIM
    return pl.pallas_call(
        _attn_kernel,
        out_shape=jax.ShapeDtypeStruct((r, MLA_WIDTH), _BF16),
        grid=(bsz, ng, nq),
        in_specs=[
            pl.BlockSpec((ATT_TILE, qw), lambda bi, pj, i: (bi * nq + i, pj)),
            pl.BlockSpec((tp, qw), lambda bi, pj, i: (bi, pj)),
            pl.BlockSpec((vw, tp), lambda bi, pj, i: (pj, bi)),
        ],
        out_specs=pl.BlockSpec((ATT_TILE, vw), lambda bi, pj, i: (bi * nq + i, pj)),
        scratch_shapes=[pltpu.VMEM((2 * ATT_PAIRS, ATT_TILE, ATT_TILE), _F32)] * 2,
        compiler_params=_cparams("parallel", "parallel", "arbitrary"),
        name="mla_attention",
    )(q, k, v)


def _merge_kernel(y_ref, bonus_ref, g_ref, ymla_ref, gate_ref, h_ref, lnx_ref, head_sum_ref,
                  wpr_ref, wpm_ref, wout_ref, ln_ref, o_ref, *, tp, alpha):
    tm = y_ref.shape[0]
    d = h_ref.shape[1]
    head_sum = head_sum_ref[...]
    valid = _valid_rows(pl.program_id(0), tm, tp)
    subs = [slice(r0, r0 + MERGE_SUB) for r0 in range(0, tm, MERGE_SUB)]
    ys = [y_ref[s, :].astype(_F32) for s in subs]
    yms = [_head_sum(y, head_sum) * (1.0 / HEAD_DIM) for y in ys]
    ycs = [y - ym for y, ym in zip(ys, yms)]
    yvs = [_head_sum(yc * yc, head_sum) * (1.0 / HEAD_DIM) for yc in ycs]
    y_rwkv = [((yc * lax.rsqrt(yv + GN_EPS) * lnx_ref[0:1, :] + lnx_ref[1:2, :]
                + bonus_ref[s, :].astype(_F32)) * g_ref[s, :].astype(_F32)).astype(_BF16)
              for s, yc, yv in zip(subs, ycs, yvs)]
    pr = [_dot(yr, wpr_ref[...]) for yr in y_rwkv]
    pm = [_dot(ymla_ref[s, :], wpm_ref[...]) for s in subs]
    mixed = []
    for s, a, b in zip(subs, pr, pm):
        gates = _sigmoid(gate_ref[s, :].astype(_F32))
        mixed.append((gates[:, :d] * a + gates[:, d:] * b).astype(_BF16))
    outs = [_dot(m, wout_ref[...]) for m in mixed]
    for s, out in zip(subs, outs):
        hn = _layer_norm(alpha * h_ref[s, :] + out, ln_ref[0:1, :], ln_ref[1:2, :])
        o_ref[s, :] = jnp.where(valid[s], hn, 0.0)


def _merge(y, bonus, g, ymla, pgate, h, lnx, head_mean, wpr, wpm, wout, ln, tp, alpha):
    r, d = h.shape
    tm = ROW_TM
    w = RWKV_WIDTH
    full = lambda a: pl.BlockSpec(a.shape, lambda i: (0,) * a.ndim)
    rows = lambda n: pl.BlockSpec((tm, n), lambda i: (i, 0))
    return pl.pallas_call(
        functools.partial(_merge_kernel, tp=tp, alpha=alpha),
        out_shape=jax.ShapeDtypeStruct((r, d), _F32),
        grid=(r // tm,),
        in_specs=[rows(w), rows(w), rows(w), rows(MLA_WIDTH),
                  rows(2 * d),
                  rows(d), full(lnx), full(head_mean), full(wpr), full(wpm), full(wout), full(ln)],
        out_specs=rows(d),
        compiler_params=_cparams("parallel"),
        name="merge",
    )(y, bonus, g, ymla, pgate, h, lnx, head_mean, wpr, wpm, wout, ln)


def _ffn_kernel(h_ref, wup_ref, cw_ref, cb_ref, wdown_ref, ln_ref, o_ref, carry_ref, act_ref, *,
                tp, alpha):
    i = pl.program_id(0)
    tm, d = h_ref.shape
    dff = wdown_ref.shape[0]

    @pl.when(i == 0)
    def _():
        carry_ref[0:8, :] = jnp.zeros((8, carry_ref.shape[1]), _F32)

    h = h_ref[...]
    hb = h.astype(_BF16)
    def conv(u, c0):
        cs = slice(c0, c0 + FF_CHUNK)
        carry_ref[8:, cs] = u
        u1 = carry_ref[7:7 + tm, cs]
        u2 = carry_ref[6:6 + tm, cs]
        out = (cw_ref[2:3, cs] * u + cw_ref[1:2, cs] * u1 + cw_ref[0:1, cs] * u2 + cb_ref[:, cs])
        carry_ref[0:8, cs] = u[tm - 8:, :]
        return out

    def up(c0):
        return (_dot(hb, wup_ref[:, c0:c0 + FF_CHUNK]),
                _dot(hb, wup_ref[:, dff + c0:dff + c0 + FF_CHUNK]))

    chunks = list(range(0, dff, FF_CHUNK))
    ahead = up(chunks[0])
    for n, c0 in enumerate(chunks):
        gate_raw, val_raw = ahead
        if n + 1 < len(chunks):
            ahead = up(chunks[n + 1])
        gate = conv(gate_raw, c0)
        val = conv(val_raw, dff + c0)
        act_ref[:, c0:c0 + FF_CHUNK] = (gate * _sigmoid(gate) * val).astype(_BF16)
    acc = _dot(act_ref[...], wdown_ref[...])
    hn = _layer_norm(alpha * h + acc, ln_ref[0:1, :], ln_ref[1:2, :])
    o_ref[...] = jnp.where(_valid_rows(i, tm, tp), hn, 0.0).reshape(o_ref.shape)


def _ffn(h, wup, cw, cb, wdown, ln, tp, alpha, bsz, final):
    r, d = h.shape
    tm = FFN_TM
    full = lambda a: pl.BlockSpec(a.shape, lambda i: (0,) * a.ndim, pipeline_mode=pl.Buffered(1))
    if final:
        nt = tp // tm
        assert FRONT == tm
        out_shape = jax.ShapeDtypeStruct((bsz, tp - FRONT, d), _F32)
        out_spec = pl.BlockSpec((1, tm, d), lambda i: (i // nt, jnp.maximum(i % nt - 1, 0), 0))
    else:
        out_shape = jax.ShapeDtypeStruct((r, d), _F32)
        out_spec = pl.BlockSpec((tm, d), lambda i: (i, 0))
    return pl.pallas_call(
        functools.partial(_ffn_kernel, tp=tp, alpha=alpha),
        out_shape=out_shape,
        grid=(r // tm,),
        in_specs=[pl.BlockSpec((tm, d), lambda i: (i, 0)),
                  full(wup), full(cw), full(cb), full(wdown), full(ln)],
        out_specs=out_spec,
        scratch_shapes=[pltpu.VMEM((8 + tm, wup.shape[1]), _F32),
                        pltpu.VMEM((tm, wdown.shape[0]), _BF16)],
        compiler_params=_cparams("arbitrary"),
        name="conv_ffn",
    )(h, wup, cw, cb, wdown, ln)


def _pad_cols(a, n):
    return jnp.pad(a, ((0, 0), (0, n - a.shape[1])))


def _rearrange_in_cols(a):
    w = RWKV_WIDTH
    o_mla = 3 * w + DECAY_LORA + ICLR_LORA + GATE_LORA
    o_gate = o_mla + Q_LORA_RANK + KV_LORA_RANK + QK_ROPE_DIM
    rows = a.shape[0]
    z = lambda n: jnp.zeros((rows, n), a.dtype)
    kr = a[:, o_mla + Q_LORA_RANK + KV_LORA_RANK:o_gate]
    parts = [
        a[:, :3 * w + DECAY_LORA + ICLR_LORA],
        a[:, 3 * w + DECAY_LORA + ICLR_LORA:o_mla], z(GATE_LORA_PAD - GATE_LORA),
        a[:, o_mla:o_mla + Q_LORA_RANK + KV_LORA_RANK],
        z(QK_NOPE_DIM), kr, z(LANES - QK_NOPE_DIM - QK_ROPE_DIM),
        a[:, o_gate:],
    ]
    return jnp.concatenate(parts, axis=1)


def _pad_heads(wmat, per_head, lead):
    kdim = wmat.shape[0]
    wh = wmat.reshape(kdim, MLA_HEADS, per_head)
    wh = jnp.pad(wh, ((0, 0), (0, 0), (lead, QK_PAD - per_head - lead)))
    return wh.reshape(kdim, MLA_HEADS * QK_PAD)


def _rope_tables(tp, reps):
    half = QK_ROPE_DIM // 2
    pos = (jnp.arange(tp, dtype=jnp.int32) - PADF).astype(_F32)
    inv_freq = ROPE_THETA ** (-jnp.arange(half, dtype=_F32) / half)
    ang = pos[:, None] * inv_freq[None, :]
    cos, sin = jnp.cos(ang), jnp.sin(ang)
    zeros = lambda n: jnp.zeros((tp, n), _F32)
    tail = LANES - QK_NOPE_DIM - QK_ROPE_DIM
    cos_t = jnp.concatenate([jnp.ones((tp, QK_NOPE_DIM), _F32), cos, cos, zeros(tail)], axis=1)
    sin_a = jnp.concatenate([zeros(QK_NOPE_DIM), -sin, zeros(half + tail)], axis=1)
    sin_b = jnp.concatenate([zeros(QK_NOPE_DIM + half), sin, zeros(tail)], axis=1)
    tab = jnp.stack([cos_t, sin_a, sin_b])
    return jnp.tile(tab, (1, reps, 1))


def kernel(x, meta_tokens, ln_in_g, ln_in_b, w_in, mu_shift, w0, w_lora_up, a0, a_lora_up,
           g_lora_up, k_k, k_a, r_k, lnx_g, lnx_b, q_norm_g, w_uq, kv_norm_g, w_uk, w_uv,
           w_proj_rwkv, w_proj_mla, w_out, ln1_g, ln1_b, w_up, conv_w, conv_b, w_down,
           ln2_g, ln2_b):
    bsz, seq, d = x.shape
    depth = w_in.shape[0]
    tp = seq + FRONT
    r = bsz * tp
    assert tp % ATT_TILE == 0 and r % ROW_TM == 0 and r % PREP_TM == 0
    alpha = float((2 * depth) ** 0.25)
    w = RWKV_WIDTH

    meta_pad = jnp.pad(meta_tokens.astype(x.dtype), ((PADF, 0), (0, 0)))
    h = (x, meta_pad, ln_in_g[None], ln_in_b[None])

    hid = jnp.arange(MXU_DIM) // HEAD_DIM
    head_sum = (hid[:, None] == hid[None, :]).astype(_BF16)
    tt = jnp.arange(PREP_TM)
    tri = ((tt[:, None] >= tt[None, :])
           & (tt[:, None] // WKV_CHUNK == tt[None, :] // WKV_CHUNK)).astype(_BF16)

    def rope_for(tm):
        reps = 1
        while (reps * tp) % tm:
            reps += 1
        return _rope_tables(tp, reps)

    for l in range(depth):
        w_in_l = _rearrange_in_cols(w_in[l]).astype(_BF16)
        mu = _rearrange_in_cols(jnp.pad(mu_shift[l][None], ((0, 0), (0, w_in.shape[2] - mu_shift.shape[1]))))
        mu = mu[:, :RW_END]
        vec = jnp.stack([w0[l], a0[l], k_k[l], k_a[l]])
        wa_up = jnp.concatenate([
            jnp.concatenate([w_lora_up[l], jnp.zeros((DECAY_LORA, w), _F32)], axis=1),
            jnp.concatenate([jnp.zeros((ICLR_LORA, w), _F32), a_lora_up[l]], axis=1)], axis=0)
        g_up = jnp.pad(g_lora_up[l], ((0, GATE_LORA_PAD - GATE_LORA), (0, 0)))
        wuq = _pad_heads(w_uq[l], QK_NOPE_DIM + QK_ROPE_DIM, 0)
        wuk = _pad_heads(w_uk[l], QK_NOPE_DIM, 0)

        params = (mu, vec, r_k[l].reshape(1, w), wa_up.astype(_BF16), g_up.astype(_BF16),
                  head_sum, tri, q_norm_g[l][None], kv_norm_g[l][None],
                  wuq.astype(_BF16), wuk.astype(_BF16), w_uv[l].T.astype(_BF16))
        tm = PREP_TM if l == 0 else PREP_TM_LATER
        outs = _mixin(h, w_in_l, rope_for(tm), params, tp, bsz, tm)
        if l == 0:
            h, outs = outs[0], outs[1:]
        pgate, kt, rt, kd, bd, v, pc, bonus, g, q, kk, vv = outs
        y = _wkv(kt, rt, kd, bd, v, pc, bsz)
        ymla = _attention(q, kk, vv, bsz)
        h = _merge(y, bonus, g, ymla, pgate, h, jnp.stack([lnx_g[l], lnx_b[l]]), head_sum,
                   w_proj_rwkv[l].astype(_BF16), w_proj_mla[l].astype(_BF16),
                   w_out[l].astype(_BF16), jnp.stack([ln1_g[l], ln1_b[l]]), tp, alpha)
        h = _ffn(h, w_up[l].astype(_BF16), conv_w[l], conv_b[l][None], w_down[l].astype(_BF16),
                 jnp.stack([ln2_g[l], ln2_b[l]]), tp, alpha, bsz, final=(l == depth - 1))
    return h
```

```python
import functools

import numpy as np
import jax
import jax.numpy as jnp
from jax import lax
from jax.experimental import pallas as pl
from jax.experimental.pallas import tpu as pltpu

N_META = 16
HEAD_DIM = 64
RWKV_HEADS = 8
RWKV_WIDTH = RWKV_HEADS * HEAD_DIM
DECAY_LORA = 64
ICLR_LORA = 64
GATE_LORA = 160
GN_EPS = 64e-5
MLA_HEADS = 8
QK_NOPE_DIM = 64
QK_ROPE_DIM = 32
V_HEAD_DIM = 64
Q_LORA_RANK = 256
KV_LORA_RANK = 256
ROPE_THETA = 10000.0
MLA_WIDTH = MLA_HEADS * V_HEAD_DIM
CONV_WIDTH = 3
LN_EPS = 1e-5
RMS_EPS = 1e-6

LANES = 128
MXU_DIM = 256
VMEM_LIMIT = 56 * 1024 * 1024

FRONT = MXU_DIM
PADF = FRONT - N_META
C_R, C_K, C_V = 0, RWKV_WIDTH, 2 * RWKV_WIDTH
C_WA = 3 * RWKV_WIDTH
GATE_LORA_PAD = 2 * LANES
C_G = C_WA + LANES
RW_END = C_G + GATE_LORA_PAD
C_CQ = RW_END
C_CKV = C_CQ + Q_LORA_RANK
C_KR = C_CKV + KV_LORA_RANK
C_GATE = C_KR + LANES
QK_PAD = LANES

WKV_CHUNK = 64
WKV_BATCH = 4
ATT_TILE = MXU_DIM
ATT_PAIRS = 4
PREP_TM = 256
PREP_TM_LATER = 256
ROW_TM = 512
FFN_TM = 256
MERGE_SUB = 256
FF_CHUNK = 256
NEG_BIG = -1e30
BF16_ROWS = 16

_F32 = jnp.float32
_BF16 = jnp.bfloat16


def _cparams(*sem):
    return pltpu.CompilerParams(dimension_semantics=sem, vmem_limit_bytes=VMEM_LIMIT)


def _dot(a, b):
    return jnp.dot(a, b, preferred_element_type=_F32)


def _dot_nt(a, b):
    return lax.dot_general(a, b, (((1,), (1,)), ((), ())), preferred_element_type=_F32)


def _dot_tn(a, b):
    return lax.dot_general(a, b, (((0,), (0,)), ((), ())), preferred_element_type=_F32)


def _split_dot(x, w_bf16, parts, left=False):
    acc = None
    rem = x
    for _ in range(parts):
        hi = rem.astype(_BF16)
        term = _dot(w_bf16, hi) if left else _dot(hi, w_bf16)
        acc = term if acc is None else acc + term
        rem = rem - hi.astype(_F32)
    return acc


def _head_sum(x, ones_bd):
    n = ones_bd.shape[0]
    xb = x.astype(_BF16)
    return jnp.concatenate([_dot(xb[:, c0:c0 + n], ones_bd) for c0 in range(0, x.shape[1], n)],
                           axis=1)


def _layer_norm(x, g, b):
    mu = jnp.mean(x, axis=-1, keepdims=True)
    xc = x - mu
    var = jnp.mean(xc * xc, axis=-1, keepdims=True)
    return xc * lax.rsqrt(var + LN_EPS) * g + b


def _sigmoid(x):
    return 1.0 / (1.0 + jnp.exp(-x))


def _valid_rows(tile_idx, tm, tp):
    row = tile_idx * tm + lax.broadcasted_iota(jnp.int32, (tm, 1), 0)
    bidx = jnp.floor((row.astype(_F32) + 0.5) * (1.0 / tp)).astype(jnp.int32)
    return (row - bidx * tp) >= PADF


def _mixin_kernel(*refs, embed, tp):
    if embed:
        x_ref, meta_ref, lng_ref, lnb_ref = refs[:4]
        refs = refs[4:]
    else:
        h_ref = refs[0]
        refs = refs[1:]
    (w_ref, rope_ref, mu_ref, vec_ref, rk_ref, wa_up_ref, g_up_ref, head_sum_ref, tri_ref,
     qg_ref, kvg_ref, wuq_ref, wuk_ref, wuv_ref) = refs[:14]
    refs = refs[14:]
    if embed:
        h_out_ref = refs[0]
        refs = refs[1:]
    (gate_ref, kt_ref, rt_ref, kd_ref, bd_ref, v_ref, pc_ref, bonus_ref, g_ref,
     q_ref, kk_ref, vv_ref, carry_ref) = refs
    i = pl.program_id(0)
    tm = kt_ref.shape[0]
    w = RWKV_WIDTH

    if embed:
        src = jnp.where(i % (tp // tm) == 0, meta_ref[...], x_ref[0])
        h = _layer_norm(src, lng_ref[...], lnb_ref[...])
        h = jnp.where(_valid_rows(i, tm, tp), h, 0.0)
        h_out_ref[...] = h
    else:
        h = h_ref[...]
    hb = h.astype(_BF16)

    @pl.when(i == 0)
    def _():
        carry_ref[...] = jnp.zeros_like(carry_ref)

    def gate_cols(c0, c1):
        gate_ref[:, c0:c1] = _dot(hb, w_ref[:, C_GATE + c0:C_GATE + c1]).astype(gate_ref.dtype)

    ngate = gate_ref.shape[1]
    gq = ngate // 4

    pr = _dot(hb, w_ref[:, :RW_END])
    gate_cols(0, gq)
    shifted = pltpu.roll(pr, 1, axis=0)
    row = lax.broadcasted_iota(jnp.int32, (tm, 1), 0)
    shifted = jnp.where(row == 0, carry_ref[7:8, :], shifted)
    carry_ref[...] = pr[tm - 8:, :]
    ps = pr + (shifted - pr) * mu_ref[...]

    r = ps[:, C_R:C_R + w]
    k = ps[:, C_K:C_K + w]
    v = ps[:, C_V:C_V + w]
    wa_in = ps[:, C_WA:C_WA + LANES]
    lane = lax.broadcasted_iota(jnp.int32, (1, LANES), 1)
    wa_in = jnp.where(lane < DECAY_LORA, jnp.tanh(wa_in), wa_in)
    wa = _dot(wa_in.astype(_BF16), wa_up_ref[...])
    w0, a0, k_k, k_a = (vec_ref[0:1, :], vec_ref[1:2, :], vec_ref[2:3, :], vec_ref[3:4, :])

    z = -(w0 + wa[:, :w])
    softplus = jnp.maximum(z, 0.0) + jnp.log1p(jnp.exp(-jnp.abs(z)))
    logw = -jnp.exp(-softplus - 0.5)
    a = _sigmoid(a0 + wa[:, w:])
    gate = _dot(_sigmoid(ps[:, C_G:C_G + GATE_LORA_PAD]).astype(_BF16), g_up_ref[...])
    gate_cols(gq, 2 * gq)

    head_sum = head_sum_ref[...]
    kk = k * k_k
    kk_ss = _head_sum(kk * kk, head_sum)
    kk = kk / jnp.maximum(jnp.sqrt(kk_ss), 1e-12)
    k = k * (1.0 + (a - 1.0) * k_a)
    b = kk * a
    bonus = _head_sum(r * k * rk_ref[...], head_sum) * v
    gate_cols(2 * gq, 3 * gq)

    tri = tri_ref[...]
    tn = tri.shape[0]
    cum = jnp.concatenate([_split_dot(logw[r0:r0 + tn], tri, 3, left=True)
                           for r0 in range(0, tm, tn)], axis=0)
    pm = _dot(hb, w_ref[:, C_CQ:C_GATE])
    e_neg = jnp.exp(-cum)
    kt_ref[...] = (kk * jnp.exp(cum - logw)).astype(_BF16)
    rt_ref[...] = (r * jnp.exp(cum)).astype(_BF16)
    kd_ref[...] = (k * e_neg).astype(_BF16)
    bd_ref[...] = (b * e_neg).astype(_BF16)
    v_ref[...] = v.astype(_BF16)
    bonus_ref[...] = bonus.astype(_BF16)
    g_ref[...] = gate.astype(_BF16)
    for ci in range(tm // WKV_CHUNK):
        last = (ci + 1) * WKV_CHUNK - 1
        pc_ref[ci] = jnp.exp(cum[last:last + 1, :])

    cos_t, sin_a, sin_b = rope_ref[0], rope_ref[1], rope_ref[2]

    def rope(x):
        nrep = x.shape[1] // LANES
        n = x.shape[1]
        c, sa, sb = (jnp.tile(t, (1, nrep)) if nrep > 1 else t for t in (cos_t, sin_a, sin_b))
        half = QK_ROPE_DIM // 2
        return x * c + pltpu.roll(x, n - half, axis=1) * sa + pltpu.roll(x, half, axis=1) * sb

    def rms(x, gain):
        return x * lax.rsqrt(jnp.mean(x * x, axis=-1, keepdims=True) + RMS_EPS) * gain

    cq = pm[:, :Q_LORA_RANK]
    ckv = pm[:, Q_LORA_RANK:Q_LORA_RANK + KV_LORA_RANK]
    kr = pm[:, C_KR - C_CQ:]
    qn = rms(cq, qg_ref[...]).astype(_BF16)
    scale = float(np.log2(np.e) / np.sqrt(QK_NOPE_DIM + QK_ROPE_DIM))
    q_ref[...] = (rope(_dot(qn, wuq_ref[...])) * scale).astype(_BF16)
    kvn = rms(ckv, kvg_ref[...]).astype(_BF16)
    gate_cols(3 * gq, ngate)
    kk_ref[...] = (_dot(kvn, wuk_ref[...]) + jnp.tile(rope(kr), (1, MLA_HEADS))).astype(_BF16)
    vv_ref[...] = _dot_nt(wuv_ref[...], kvn).astype(_BF16)


def _mixin(src, w_in, rope_tab, params, tp, bsz, tm):
    embed = isinstance(src, tuple)
    d = w_in.shape[0]
    r = bsz * tp
    nt = r // tm
    per = tp // tm
    nrope = rope_tab.shape[1] // tm
    w = RWKV_WIDTH
    nchunk = tm // WKV_CHUNK
    ngate = w_in.shape[1] - C_GATE
    full = lambda a: pl.BlockSpec(a.shape, lambda i: (0,) * a.ndim)
    row_out = lambda n: pl.BlockSpec((tm, n), lambda i: (i, 0))
    if embed:
        assert FRONT == tm
        x, meta_pad, ln_g, ln_b = src
        lead_args = [x, meta_pad, ln_g, ln_b]
        lead_specs = [pl.BlockSpec((1, tm, d), lambda i: (i // per, jnp.maximum(i % per - 1, 0), 0)),
                      full(meta_pad), full(ln_g), full(ln_b)]
        lead_out, lead_out_specs = [jax.ShapeDtypeStruct((r, d), _F32)], [row_out(d)]
    else:
        lead_args, lead_specs = [src], [row_out(d)]
        lead_out, lead_out_specs = [], []
    out_shape = (
        lead_out
        + [jax.ShapeDtypeStruct((r, ngate), _BF16)]
        + [jax.ShapeDtypeStruct((r, w), _BF16)] * 5
        + [jax.ShapeDtypeStruct((nt * nchunk, 1, w), _F32)]
        + [jax.ShapeDtypeStruct((r, w), _BF16)] * 2
        + [jax.ShapeDtypeStruct((r, MLA_HEADS * QK_PAD), _BF16)] * 2
        + [jax.ShapeDtypeStruct((MLA_WIDTH, r), _BF16)]
    )
    out_specs = (
        lead_out_specs
        + [row_out(ngate)]
        + [row_out(w)] * 5
        + [pl.BlockSpec((nchunk, 1, w), lambda i: (i, 0, 0))]
        + [row_out(w)] * 2
        + [row_out(MLA_HEADS * QK_PAD)] * 2
        + [pl.BlockSpec((MLA_WIDTH, tm), lambda i: (0, i))]
    )
    return pl.pallas_call(
        functools.partial(_mixin_kernel, embed=embed, tp=tp),
        out_shape=out_shape,
        grid=(nt,),
        in_specs=lead_specs + [full(w_in), pl.BlockSpec((3, tm, LANES), lambda i: (0, i % nrope, 0))]
        + [full(a) for a in params],
        out_specs=out_specs,
        scratch_shapes=[pltpu.VMEM((8, RW_END), _F32)],
        compiler_params=_cparams("arbitrary"),
        name="mix_in",
    )(*lead_args, w_in, rope_tab, *params)


def _wkv_kernel(kt_ref, rt_ref, kd_ref, bd_ref, v_ref, pc_ref, y_ref, h_scr):
    c = pl.program_id(1)

    @pl.when(c < PADF // WKV_CHUNK)
    def _():
        h_scr[...] = jnp.zeros_like(h_scr)
        y_ref[...] = jnp.zeros_like(y_ref)

    @pl.when(c >= PADF // WKV_CHUNK)
    def _():
        _wkv_chunk(kt_ref, rt_ref, kd_ref, bd_ref, v_ref, pc_ref, y_ref, h_scr)


def _wkv_chunk(kt_ref, rt_ref, kd_ref, bd_ref, v_ref, pc_ref, y_ref, h_scr):
    npair = RWKV_HEADS // 2
    cl = WKV_CHUNK
    n2 = 2 * cl
    lane = lax.broadcasted_iota(jnp.int32, (1, LANES), 1)
    first = lane < HEAD_DIM
    row = lax.broadcasted_iota(jnp.int32, (n2, n2), 0)
    col = lax.broadcasted_iota(jnp.int32, (n2, n2), 1)
    strict = row > col
    incl = row >= col
    eye = (row == col).astype(_F32)

    def stack(x):
        zero = jnp.zeros_like(x)
        return jnp.concatenate([jnp.where(first, x, zero), jnp.where(first, zero, x)], axis=0)

    def same_block(size):
        sh = size.bit_length() - 1
        return jnp.right_shift(row, sh) == jnp.right_shift(col, sh)

    units = [(bl, j) for bl in range(kt_ref.shape[0]) for j in range(npair)]
    each = lambda fn, *lists: [fn(*args) for args in zip(*lists)]
    bf = lambda x: x.astype(_BF16)

    def load(ref):
        return [stack(ref[bl, :, j * LANES:(j + 1) * LANES]) for bl, j in units]

    kt2, rt2, kd2, bd2, v2 = load(kt_ref), load(rt_ref), load(kd_ref), load(bd_ref), load(v_ref)
    lhs = each(lambda a, b: jnp.concatenate([a, b], axis=0), kt2, rt2)
    rhs = each(lambda a, b: jnp.concatenate([a, b], axis=0), kd2, bd2)
    gram = each(_dot_nt, lhs, rhs)
    a_kk = [bf(jnp.where(strict, g[:n2, :n2], 0.0)) for g in gram]
    a_kb = [jnp.where(strict, g[:n2, n2:], 0.0) for g in gram]
    a_out = [bf(jnp.concatenate([jnp.where(incl, g[n2:, :n2], 0.0),
                                 jnp.where(incl, -g[n2:, n2:], 0.0)], axis=1)) for g in gram]

    blk = same_block(8)
    nd = [bf(jnp.where(blk, a, 0.0)) for a in a_kb]
    s2 = [bf(x) for x in each(_dot, nd, nd)]
    i_minus = [eye - x.astype(_F32) for x in nd]
    p1 = each(lambda im, s: im + _dot(bf(im), s), i_minus, s2)
    s4 = [bf(x) for x in each(_dot, s2, s2)]
    t_inv = each(lambda p, s: p + _dot(bf(p), s), p1, s4)
    size = 8
    while size < cl:
        outer = same_block(2 * size)
        sel = outer & jnp.logical_not(blk)
        off = [bf(jnp.where(sel, a, 0.0)) for a in a_kb]
        tb = [bf(t) for t in t_inv]
        m = [bf(x) for x in each(_dot, tb, off)]
        t_inv = each(lambda t, mm, tbb: t - _dot(mm, tbb), t_inv, m, tb)
        blk = outer
        size *= 2
    tb = [bf(t) for t in t_inv]
    akv = each(_dot, a_kk, v2)

    hbd = [h_scr[i] for i in range(len(units))]
    x0 = each(lambda l, h: _dot(l, bf(h)), lhs, hbd)
    u2 = each(lambda t, x, w: bf(_dot(t, bf(x[:n2] + w))), tb, x0, akv)
    vu = each(lambda a, b: jnp.concatenate([a, b], axis=0), v2, u2)
    y2 = each(lambda x, a, z: x[n2:] + _dot(a, z), x0, a_out, vu)
    upd = each(lambda kd, bd, z: _dot_tn(jnp.concatenate([kd, -bd], axis=0), z), kd2, bd2, vu)
    for i, (bl, j) in enumerate(units):
        sl = slice(j * LANES, (j + 1) * LANES)
        y_ref[bl, :, sl] = (y2[i][:cl] + y2[i][cl:]).astype(y_ref.dtype)
        pc_row = pc_ref[bl, 0, :, sl]
        pc_col = jnp.transpose(jnp.broadcast_to(pc_row, (LANES, LANES)))
        h_scr[i] = (hbd[i] + upd[i]) * pc_col


def _wkv(kt, rt, kd, bd, v, pc, bsz):
    r, w = kt.shape
    tp = r // bsz
    nc = tp // WKV_CHUNK
    nb = WKV_BATCH
    assert bsz % nb == 0
    blk = pl.BlockSpec((nb, WKV_CHUNK, w), lambda bg, c: (bg, c, 0))
    args = [a.reshape(bsz, tp, w) for a in (kt, rt, kd, bd, v)] + [pc.reshape(bsz, nc, 1, w)]
    y = pl.pallas_call(
        _wkv_kernel,
        out_shape=jax.ShapeDtypeStruct((bsz, tp, w), _F32),
        grid=(bsz // nb, nc),
        in_specs=[blk] * 5 + [pl.BlockSpec((nb, 1, 1, w), lambda bg, c: (bg, c, 0, 0))],
        out_specs=blk,
        scratch_shapes=[pltpu.VMEM((nb * (RWKV_HEADS // 2), LANES, LANES), _F32)],
        compiler_params=_cparams("parallel", "arbitrary"),
        name="wkv7",
    )(*args)
    return y.reshape(r, w)


def _attn_kernel(q_ref, k_ref, v_ref, o_ref, s0_scr, s1_scr):
    i = pl.program_id(2)
    tq = q_ref.shape[0]
    tk = tq
    nhead = q_ref.shape[1] // QK_PAD
    q_pos = i * tq + lax.broadcasted_iota(jnp.int32, (1, tq), 1)
    head_cols = lambda n: slice(n * QK_PAD, (n + 1) * QK_PAD)
    qh = [q_ref[:, head_cols(n)] for n in range(nhead)]

    s_slots = (s0_scr, s1_scr)

    def scores(jb, slot):
        start = pl.multiple_of(jb * tk, tk)
        for n in range(nhead):
            s_slots[slot][n] = _dot_nt(k_ref[pl.ds(start, tk), head_cols(n)], qh[n])

    def accumulate(n, vt, s, m_prev, acc):
        m_new = jnp.maximum(m_prev, jnp.max(s, axis=0, keepdims=True))
        prob = jnp.exp2(s - m_new).astype(_BF16)
        if prob.shape[0] < vt.shape[1]:
            prob = jnp.concatenate(
                [jnp.zeros((vt.shape[1] - prob.shape[0], tq), _BF16), prob], axis=0)
        lo = (n % 2) * V_HEAD_DIM
        vt_h = jnp.concatenate(
            [vt[lo:lo + V_HEAD_DIM], jnp.ones((BF16_ROWS, vt.shape[1]), _BF16)], axis=0)
        acc = jnp.exp2(m_prev - m_new) * acc + _dot(vt_h, prob)
        return [m_new, acc]

    def update(jb, slot, stats, masked):
        start = pl.multiple_of(jb * tk, tk)
        if masked:
            mask = (start + lax.broadcasted_iota(jnp.int32, (tk, 1), 0)) <= q_pos
        out = []
        for n in range(nhead):
            vt = v_ref[(n // 2) * LANES:(n // 2 + 1) * LANES, pl.ds(start, tk)]
            s = s_slots[slot][n]
            if masked:
                s = jnp.where(mask, s, NEG_BIG)
            out += accumulate(n, vt, s, stats[2 * n], stats[2 * n + 1])
        return tuple(out)

    def front(s_front, stats):
        mask = (PADF + lax.broadcasted_iota(jnp.int32, (N_META, 1), 0)) <= q_pos
        out = []
        for n in range(nhead):
            vt = v_ref[(n // 2) * LANES:(n // 2 + 1) * LANES, 0:tk]
            out += accumulate(n, vt, jnp.where(mask, s_front[n], NEG_BIG),
                              stats[2 * n], stats[2 * n + 1])
        return tuple(out)

    init = (jnp.full((1, tq), NEG_BIG, _F32),
            jnp.zeros((V_HEAD_DIM + BF16_ROWS, tq), _F32)) * nhead

    s_front = [_dot_nt(k_ref[PADF:FRONT, head_cols(n)], qh[n]) for n in range(nhead)]
    scores(1, 1)
    carry = front(s_front, init)

    def two_blocks(pidx, stats):
        jb = 2 * pidx + 1
        scores(jb + 1, 0)
        stats = update(jb, 1, stats, False)
        scores(jb + 2, 1)
        return update(jb + 1, 0, stats, False)

    def tail_even_i(_, stats):
        scores(i, 0)
        stats = update(i - 1, 1, stats, False)
        return update(i, 0, stats, True)

    def tail_odd_i(_, stats):
        return update(i, 1, stats, True)

    is_even = (i % 2 == 0)
    carry = lax.fori_loop(0, (i - 1) // 2, two_blocks, carry)
    carry = lax.fori_loop(0, jnp.where((i > 0) & is_even, 1, 0), tail_even_i, carry)
    carry = lax.fori_loop(0, jnp.where(is_even, 0, 1), tail_odd_i, carry)
    for g in range(nhead // 2):
        halves = [acc[:V_HEAD_DIM] / acc[V_HEAD_DIM:V_HEAD_DIM + 1]
                  for acc in (carry[4 * g + 1], carry[4 * g + 3])]
        o = jnp.concatenate(halves, axis=0)
        o_ref[:, g * LANES:(g + 1) * LANES] = jnp.transpose(o).astype(o_ref.dtype)


def _attention(q, k, v, bsz):
    r = q.shape[0]
    tp = r // bsz
    nq = tp // ATT_TILE
    ng = MLA_HEADS // 2 // ATT_PAIRS
    qw, vw = ATT_PAIRS * 2 * QK_PAD, ATT_PAIRS * 2 * V_HEAD_DIM
    return pl.pallas_call(
        _attn_kernel,
        out_shape=jax.ShapeDtypeStruct((r, MLA_WIDTH), _BF16),
        grid=(bsz, ng, nq),
        in_specs=[
            pl.BlockSpec((ATT_TILE, qw), lambda bi, pj, i: (bi * nq + i, pj)),
            pl.BlockSpec((tp, qw), lambda bi, pj, i: (bi, pj)),
            pl.BlockSpec((vw, tp), lambda bi, pj, i: (pj, bi)),
        ],
        out_specs=pl.BlockSpec((ATT_TILE, vw), lambda bi, pj, i: (bi * nq + i, pj)),
        scratch_shapes=[pltpu.VMEM((2 * ATT_PAIRS, ATT_TILE, ATT_TILE), _F32)] * 2,
        compiler_params=_cparams("parallel", "parallel", "arbitrary"),
        name="mla_attention",
    )(q, k, v)


def _merge_kernel(y_ref, bonus_ref, g_ref, ymla_ref, gate_ref, h_ref, lnx_ref, head_sum_ref,
                  wpr_ref, wpm_ref, wout_ref, ln_ref, o_ref, *, tp, alpha):
    tm = y_ref.shape[0]
    d = h_ref.shape[1]
    head_sum = head_sum_ref[...]
    valid = _valid_rows(pl.program_id(0), tm, tp)
    subs = [slice(r0, r0 + MERGE_SUB) for r0 in range(0, tm, MERGE_SUB)]
    ys = [y_ref[s, :].astype(_F32) for s in subs]
    yms = [_head_sum(y, head_sum) * (1.0 / HEAD_DIM) for y in ys]
    ycs = [y - ym for y, ym in zip(ys, yms)]
    yvs = [_head_sum(yc * yc, head_sum) * (1.0 / HEAD_DIM) for yc in ycs]
    y_rwkv = [((yc * lax.rsqrt(yv + GN_EPS) * lnx_ref[0:1, :] + lnx_ref[1:2, :]
                + bonus_ref[s, :].astype(_F32)) * g_ref[s, :].astype(_F32)).astype(_BF16)
              for s, yc, yv in zip(subs, ycs, yvs)]
    pr = [_dot(yr, wpr_ref[...]) for yr in y_rwkv]
    pm = [_dot(ymla_ref[s, :], wpm_ref[...]) for s in subs]
    mixed = []
    for s, a, b in zip(subs, pr, pm):
        gates = _sigmoid(gate_ref[s, :].astype(_F32))
        mixed.append((gates[:, :d] * a + gates[:, d:] * b).astype(_BF16))
    outs = [_dot(m, wout_ref[...]) for m in mixed]
    for s, out in zip(subs, outs):
        hn = _layer_norm(alpha * h_ref[s, :] + out, ln_ref[0:1, :], ln_ref[1:2, :])
        o_ref[s, :] = jnp.where(valid[s], hn, 0.0)


def _merge(y, bonus, g, ymla, pgate, h, lnx, head_mean, wpr, wpm, wout, ln, tp, alpha):
    r, d = h.shape
    tm = ROW_TM
    w = RWKV_WIDTH
    full = lambda a: pl.BlockSpec(a.shape, lambda i: (0,) * a.ndim)
    rows = lambda n: pl.BlockSpec((tm, n), lambda i: (i, 0))
    return pl.pallas_call(
        functools.partial(_merge_kernel, tp=tp, alpha=alpha),
        out_shape=jax.ShapeDtypeStruct((r, d), _F32),
        grid=(r // tm,),
        in_specs=[rows(w), rows(w), rows(w), rows(MLA_WIDTH),
                  rows(2 * d),
                  rows(d), full(lnx), full(head_mean), full(wpr), full(wpm), full(wout), full(ln)],
        out_specs=rows(d),
        compiler_params=_cparams("parallel"),
        name="merge",
    )(y, bonus, g, ymla, pgate, h, lnx, head_mean, wpr, wpm, wout, ln)


def _ffn_kernel(h_ref, wup_ref, cw_ref, cb_ref, wdown_ref, ln_ref, o_ref, carry_ref, act_ref,
                pre_ref, *, tp, alpha):
    i = pl.program_id(0)
    tm, d = h_ref.shape
    dff = wdown_ref.shape[0]

    @pl.when(i == 0)
    def _():
        carry_ref[0:8, :] = jnp.zeros((8, carry_ref.shape[1]), _F32)
        pre_ref[...] = jnp.zeros_like(pre_ref)

    h = h_ref[...]
    hb = h.astype(_BF16)
    def conv(u, c0):
        cs = slice(c0, c0 + FF_CHUNK)
        carry_ref[8:, cs] = u
        u1 = carry_ref[7:7 + tm, cs]
        u2 = carry_ref[6:6 + tm, cs]
        out = (cw_ref[2:3, cs] * u + cw_ref[1:2, cs] * u1 + cw_ref[0:1, cs] * u2 + cb_ref[:, cs])
        carry_ref[0:8, cs] = u[tm - 8:, :]
        return out

    def up(c0):
        return (_dot(hb, wup_ref[:, c0:c0 + FF_CHUNK]),
                _dot(hb, wup_ref[:, dff + c0:dff + c0 + FF_CHUNK]))

    chunks = list(range(0, dff, FF_CHUNK))
    ahead = up(chunks[0])
    hn = _layer_norm(pre_ref[...], ln_ref[0:1, :], ln_ref[1:2, :])
    o_ref[...] = jnp.where(_valid_rows(i - 1, tm, tp), hn, 0.0).reshape(o_ref.shape)
    for n, c0 in enumerate(chunks):
        gate_raw, val_raw = ahead
        if n + 1 < len(chunks):
            ahead = up(chunks[n + 1])
        gate = conv(gate_raw, c0)
        val = conv(val_raw, dff + c0)
        act_ref[:, c0:c0 + FF_CHUNK] = (gate * _sigmoid(gate) * val).astype(_BF16)
    pre_ref[...] = alpha * h + _dot(act_ref[...], wdown_ref[...])


def _ffn(h, wup, cw, cb, wdown, ln, tp, alpha, bsz, final):
    r, d = h.shape
    tm = FFN_TM
    full = lambda a: pl.BlockSpec(a.shape, lambda i: (0,) * a.ndim, pipeline_mode=pl.Buffered(1))
    nt = r // tm
    done = lambda i: jnp.maximum(i - 1, 0)
    if final:
        per = tp // tm
        assert FRONT == tm
        out_shape = jax.ShapeDtypeStruct((bsz, tp - FRONT, d), _F32)
        out_spec = pl.BlockSpec(
            (1, tm, d), lambda i: (done(i) // per, jnp.maximum(done(i) % per - 1, 0), 0))
    else:
        out_shape = jax.ShapeDtypeStruct((r, d), _F32)
        out_spec = pl.BlockSpec((tm, d), lambda i: (done(i), 0))
    return pl.pallas_call(
        functools.partial(_ffn_kernel, tp=tp, alpha=alpha),
        out_shape=out_shape,
        grid=(nt + 1,),
        in_specs=[pl.BlockSpec((tm, d), lambda i: (jnp.minimum(i, nt - 1), 0)),
                  full(wup), full(cw), full(cb), full(wdown), full(ln)],
        out_specs=out_spec,
        scratch_shapes=[pltpu.VMEM((8 + tm, wup.shape[1]), _F32),
                        pltpu.VMEM((tm, wdown.shape[0]), _BF16),
                        pltpu.VMEM((tm, d), _F32)],
        compiler_params=_cparams("arbitrary"),
        name="conv_ffn",
    )(h, wup, cw, cb, wdown, ln)


def _pad_cols(a, n):
    return jnp.pad(a, ((0, 0), (0, n - a.shape[1])))


def _rearrange_in_cols(a):
    w = RWKV_WIDTH
    o_mla = 3 * w + DECAY_LORA + ICLR_LORA + GATE_LORA
    o_gate = o_mla + Q_LORA_RANK + KV_LORA_RANK + QK_ROPE_DIM
    rows = a.shape[0]
    z = lambda n: jnp.zeros((rows, n), a.dtype)
    kr = a[:, o_mla + Q_LORA_RANK + KV_LORA_RANK:o_gate]
    parts = [
        a[:, :3 * w + DECAY_LORA + ICLR_LORA],
        a[:, 3 * w + DECAY_LORA + ICLR_LORA:o_mla], z(GATE_LORA_PAD - GATE_LORA),
        a[:, o_mla:o_mla + Q_LORA_RANK + KV_LORA_RANK],
        z(QK_NOPE_DIM), kr, z(LANES - QK_NOPE_DIM - QK_ROPE_DIM),
        a[:, o_gate:],
    ]
    return jnp.concatenate(parts, axis=1)


def _pad_heads(wmat, per_head, lead):
    kdim = wmat.shape[0]
    wh = wmat.reshape(kdim, MLA_HEADS, per_head)
    wh = jnp.pad(wh, ((0, 0), (0, 0), (lead, QK_PAD - per_head - lead)))
    return wh.reshape(kdim, MLA_HEADS * QK_PAD)


def _rope_tables(tp, reps):
    half = QK_ROPE_DIM // 2
    pos = (jnp.arange(tp, dtype=jnp.int32) - PADF).astype(_F32)
    inv_freq = ROPE_THETA ** (-jnp.arange(half, dtype=_F32) / half)
    ang = pos[:, None] * inv_freq[None, :]
    cos, sin = jnp.cos(ang), jnp.sin(ang)
    zeros = lambda n: jnp.zeros((tp, n), _F32)
    tail = LANES - QK_NOPE_DIM - QK_ROPE_DIM
    cos_t = jnp.concatenate([jnp.ones((tp, QK_NOPE_DIM), _F32), cos, cos, zeros(tail)], axis=1)
    sin_a = jnp.concatenate([zeros(QK_NOPE_DIM), -sin, zeros(half + tail)], axis=1)
    sin_b = jnp.concatenate([zeros(QK_NOPE_DIM + half), sin, zeros(tail)], axis=1)
    tab = jnp.stack([cos_t, sin_a, sin_b])
    return jnp.tile(tab, (1, reps, 1))


def kernel(x, meta_tokens, ln_in_g, ln_in_b, w_in, mu_shift, w0, w_lora_up, a0, a_lora_up,
           g_lora_up, k_k, k_a, r_k, lnx_g, lnx_b, q_norm_g, w_uq, kv_norm_g, w_uk, w_uv,
           w_proj_rwkv, w_proj_mla, w_out, ln1_g, ln1_b, w_up, conv_w, conv_b, w_down,
           ln2_g, ln2_b):
    bsz, seq, d = x.shape
    depth = w_in.shape[0]
    tp = seq + FRONT
    r = bsz * tp
    assert tp % ATT_TILE == 0 and r % ROW_TM == 0 and r % PREP_TM == 0
    alpha = float((2 * depth) ** 0.25)
    w = RWKV_WIDTH

    meta_pad = jnp.pad(meta_tokens.astype(x.dtype), ((PADF, 0), (0, 0)))
    h = (x, meta_pad, ln_in_g[None], ln_in_b[None])

    hid = jnp.arange(MXU_DIM) // HEAD_DIM
    head_sum = (hid[:, None] == hid[None, :]).astype(_BF16)
    tt = jnp.arange(PREP_TM)
    tri = ((tt[:, None] >= tt[None, :])
           & (tt[:, None] // WKV_CHUNK == tt[None, :] // WKV_CHUNK)).astype(_BF16)

    def rope_for(tm):
        reps = 1
        while (reps * tp) % tm:
            reps += 1
        return _rope_tables(tp, reps)

    for l in range(depth):
        w_in_l = _rearrange_in_cols(w_in[l]).astype(_BF16)
        mu = _rearrange_in_cols(jnp.pad(mu_shift[l][None], ((0, 0), (0, w_in.shape[2] - mu_shift.shape[1]))))
        mu = mu[:, :RW_END]
        vec = jnp.stack([w0[l], a0[l], k_k[l], k_a[l]])
        wa_up = jnp.concatenate([
            jnp.concatenate([w_lora_up[l], jnp.zeros((DECAY_LORA, w), _F32)], axis=1),
            jnp.concatenate([jnp.zeros((ICLR_LORA, w), _F32), a_lora_up[l]], axis=1)], axis=0)
        g_up = jnp.pad(g_lora_up[l], ((0, GATE_LORA_PAD - GATE_LORA), (0, 0)))
        wuq = _pad_heads(w_uq[l], QK_NOPE_DIM + QK_ROPE_DIM, 0)
        wuk = _pad_heads(w_uk[l], QK_NOPE_DIM, 0)

        params = (mu, vec, r_k[l].reshape(1, w), wa_up.astype(_BF16), g_up.astype(_BF16),
                  head_sum, tri, q_norm_g[l][None], kv_norm_g[l][None],
                  wuq.astype(_BF16), wuk.astype(_BF16), w_uv[l].T.astype(_BF16))
        tm = PREP_TM if l == 0 else PREP_TM_LATER
        outs = _mixin(h, w_in_l, rope_for(tm), params, tp, bsz, tm)
        if l == 0:
            h, outs = outs[0], outs[1:]
        pgate, kt, rt, kd, bd, v, pc, bonus, g, q, kk, vv = outs
        y = _wkv(kt, rt, kd, bd, v, pc, bsz)
        ymla = _attention(q, kk, vv, bsz)
        h = _merge(y, bonus, g, ymla, pgate, h, jnp.stack([lnx_g[l], lnx_b[l]]), head_sum,
                   w_proj_rwkv[l].astype(_BF16), w_proj_mla[l].astype(_BF16),
                   w_out[l].astype(_BF16), jnp.stack([ln1_g[l], ln1_b[l]]), tp, alpha)
        h = _ffn(h, w_up[l].astype(_BF16), conv_w[l], conv_b[l][None], w_down[l].astype(_BF16),
                 jnp.stack([ln2_g[l], ln2_b[l]]), tp, alpha, bsz, final=(l == depth - 1))
    return h
```

```python
import functools

import numpy as np
import jax
import jax.numpy as jnp
from jax import lax
from jax.experimental import pallas as pl
from jax.experimental.pallas import tpu as pltpu

N_META = 16
HEAD_DIM = 64
RWKV_HEADS = 8
RWKV_WIDTH = RWKV_HEADS * HEAD_DIM
DECAY_LORA = 64
ICLR_LORA = 64
GATE_LORA = 160
GN_EPS = 64e-5
MLA_HEADS = 8
QK_NOPE_DIM = 64
QK_ROPE_DIM = 32
V_HEAD_DIM = 64
Q_LORA_RANK = 256
KV_LORA_RANK = 256
ROPE_THETA = 10000.0
MLA_WIDTH = MLA_HEADS * V_HEAD_DIM
CONV_WIDTH = 3
LN_EPS = 1e-5
RMS_EPS = 1e-6

LANES = 128
MXU_DIM = 256
VMEM_LIMIT = 56 * 1024 * 1024

FRONT = MXU_DIM
PADF = FRONT - N_META
C_R, C_K, C_V = 0, RWKV_WIDTH, 2 * RWKV_WIDTH
C_WA = 3 * RWKV_WIDTH
GATE_LORA_PAD = 2 * LANES
C_G = C_WA + LANES
RW_END = C_G + GATE_LORA_PAD
C_CQ = RW_END
C_CKV = C_CQ + Q_LORA_RANK
C_KR = C_CKV + KV_LORA_RANK
C_GATE = C_KR + LANES
QK_PAD = LANES

WKV_CHUNK = 64
WKV_BATCH = 4
ATT_TILE = MXU_DIM
ATT_PAIRS = 4
PREP_TM = 256
PREP_TM_LATER = 256
ROW_TM = 512
FFN_TM = 256
MERGE_SUB = 256
FF_CHUNK = 256
NEG_BIG = -1e30
BF16_ROWS = 16

_F32 = jnp.float32
_BF16 = jnp.bfloat16


def _cparams(*sem):
    return pltpu.CompilerParams(dimension_semantics=sem, vmem_limit_bytes=VMEM_LIMIT)


def _dot(a, b):
    return jnp.dot(a, b, preferred_element_type=_F32)


def _dot_nt(a, b):
    return lax.dot_general(a, b, (((1,), (1,)), ((), ())), preferred_element_type=_F32)


def _dot_tn(a, b):
    return lax.dot_general(a, b, (((0,), (0,)), ((), ())), preferred_element_type=_F32)


def _split_dot(x, w_bf16, parts, left=False):
    acc = None
    rem = x
    for _ in range(parts):
        hi = rem.astype(_BF16)
        term = _dot(w_bf16, hi) if left else _dot(hi, w_bf16)
        acc = term if acc is None else acc + term
        rem = rem - hi.astype(_F32)
    return acc


def _head_sum(x, ones_bd):
    n = ones_bd.shape[0]
    xb = x.astype(_BF16)
    return jnp.concatenate([_dot(xb[:, c0:c0 + n], ones_bd) for c0 in range(0, x.shape[1], n)],
                           axis=1)


def _layer_norm(x, g, b):
    mu = jnp.mean(x, axis=-1, keepdims=True)
    xc = x - mu
    var = jnp.mean(xc * xc, axis=-1, keepdims=True)
    return xc * lax.rsqrt(var + LN_EPS) * g + b


def _sigmoid(x):
    return 1.0 / (1.0 + jnp.exp(-x))


def _valid_rows(tile_idx, tm, tp):
    row = tile_idx * tm + lax.broadcasted_iota(jnp.int32, (tm, 1), 0)
    bidx = jnp.floor((row.astype(_F32) + 0.5) * (1.0 / tp)).astype(jnp.int32)
    return (row - bidx * tp) >= PADF


def _mixin_kernel(*refs, embed, tp):
    if embed:
        x_ref, meta_ref, lng_ref, lnb_ref = refs[:4]
        refs = refs[4:]
    else:
        h_ref = refs[0]
        refs = refs[1:]
    (w_ref, rope_ref, mu_ref, vec_ref, rk_ref, wa_up_ref, g_up_ref, head_sum_ref, tri_ref,
     qg_ref, kvg_ref, wuq_ref, wuk_ref, wuv_ref) = refs[:14]
    refs = refs[14:]
    if embed:
        h_out_ref = refs[0]
        refs = refs[1:]
    (gate_ref, kt_ref, rt_ref, kd_ref, bd_ref, v_ref, pc_ref, bonus_ref, g_ref,
     q_ref, kk_ref, vv_ref, carry_ref) = refs
    i = pl.program_id(0)
    tm = kt_ref.shape[0]
    w = RWKV_WIDTH

    if embed:
        src = jnp.where(i % (tp // tm) == 0, meta_ref[...], x_ref[0])
        h = _layer_norm(src, lng_ref[...], lnb_ref[...])
        h = jnp.where(_valid_rows(i, tm, tp), h, 0.0)
        h_out_ref[...] = h
    else:
        h = h_ref[...]
    hb = h.astype(_BF16)

    @pl.when(i == 0)
    def _():
        carry_ref[...] = jnp.zeros_like(carry_ref)

    def gate_cols(c0, c1):
        gate_ref[:, c0:c1] = _dot(hb, w_ref[:, C_GATE + c0:C_GATE + c1]).astype(gate_ref.dtype)

    ngate = gate_ref.shape[1]
    gq = ngate // 4

    pr = _dot(hb, w_ref[:, :RW_END])
    gate_cols(0, gq)
    shifted = pltpu.roll(pr, 1, axis=0)
    row = lax.broadcasted_iota(jnp.int32, (tm, 1), 0)
    shifted = jnp.where(row == 0, carry_ref[7:8, :], shifted)
    carry_ref[...] = pr[tm - 8:, :]
    ps = pr + (shifted - pr) * mu_ref[...]

    r = ps[:, C_R:C_R + w]
    k = ps[:, C_K:C_K + w]
    v = ps[:, C_V:C_V + w]
    wa_in = ps[:, C_WA:C_WA + LANES]
    lane = lax.broadcasted_iota(jnp.int32, (1, LANES), 1)
    wa_in = jnp.where(lane < DECAY_LORA, jnp.tanh(wa_in), wa_in)
    wa = _dot(wa_in.astype(_BF16), wa_up_ref[...])
    w0, a0, k_k, k_a = (vec_ref[0:1, :], vec_ref[1:2, :], vec_ref[2:3, :], vec_ref[3:4, :])

    z = -(w0 + wa[:, :w])
    softplus = jnp.maximum(z, 0.0) + jnp.log1p(jnp.exp(-jnp.abs(z)))
    logw = -jnp.exp(-softplus - 0.5)
    a = _sigmoid(a0 + wa[:, w:])
    gate = _dot(_sigmoid(ps[:, C_G:C_G + GATE_LORA_PAD]).astype(_BF16), g_up_ref[...])
    gate_cols(gq, 2 * gq)

    head_sum = head_sum_ref[...]
    kk = k * k_k
    kk_ss = _head_sum(kk * kk, head_sum)
    kk = kk / jnp.maximum(jnp.sqrt(kk_ss), 1e-12)
    k = k * (1.0 + (a - 1.0) * k_a)
    b = kk * a
    bonus = _head_sum(r * k * rk_ref[...], head_sum) * v
    gate_cols(2 * gq, 3 * gq)

    tri = tri_ref[...]
    tn = tri.shape[0]
    cum = jnp.concatenate([_split_dot(logw[r0:r0 + tn], tri, 3, left=True)
                           for r0 in range(0, tm, tn)], axis=0)
    pm = _dot(hb, w_ref[:, C_CQ:C_GATE])
    e_neg = jnp.exp(-cum)
    kt_ref[...] = (kk * jnp.exp(cum - logw)).astype(_BF16)
    rt_ref[...] = (r * jnp.exp(cum)).astype(_BF16)
    kd_ref[...] = (k * e_neg).astype(_BF16)
    bd_ref[...] = (b * e_neg).astype(_BF16)
    v_ref[...] = v.astype(_BF16)
    bonus_ref[...] = bonus.astype(_BF16)
    g_ref[...] = gate.astype(_BF16)
    for ci in range(tm // WKV_CHUNK):
        last = (ci + 1) * WKV_CHUNK - 1
        pc_ref[ci] = jnp.exp(cum[last:last + 1, :])

    cos_t, sin_a, sin_b = rope_ref[0], rope_ref[1], rope_ref[2]

    def rope(x):
        nrep = x.shape[1] // LANES
        n = x.shape[1]
        c, sa, sb = (jnp.tile(t, (1, nrep)) if nrep > 1 else t for t in (cos_t, sin_a, sin_b))
        half = QK_ROPE_DIM // 2
        return x * c + pltpu.roll(x, n - half, axis=1) * sa + pltpu.roll(x, half, axis=1) * sb

    def rms(x, gain):
        return x * lax.rsqrt(jnp.mean(x * x, axis=-1, keepdims=True) + RMS_EPS) * gain

    cq = pm[:, :Q_LORA_RANK]
    ckv = pm[:, Q_LORA_RANK:Q_LORA_RANK + KV_LORA_RANK]
    kr = pm[:, C_KR - C_CQ:]
    qn = rms(cq, qg_ref[...]).astype(_BF16)
    scale = float(np.log2(np.e) / np.sqrt(QK_NOPE_DIM + QK_ROPE_DIM))
    q_ref[...] = (rope(_dot(qn, wuq_ref[...])) * scale).astype(_BF16)
    kvn = rms(ckv, kvg_ref[...]).astype(_BF16)
    gate_cols(3 * gq, ngate)
    kk_ref[...] = (_dot(kvn, wuk_ref[...]) + jnp.tile(rope(kr), (1, MLA_HEADS))).astype(_BF16)
    vv_ref[...] = _dot_nt(wuv_ref[...], kvn).astype(_BF16)


def _mixin(src, w_in, rope_tab, params, tp, bsz, tm):
    embed = isinstance(src, tuple)
    d = w_in.shape[0]
    r = bsz * tp
    nt = r // tm
    per = tp // tm
    nrope = rope_tab.shape[1] // tm
    w = RWKV_WIDTH
    nchunk = tm // WKV_CHUNK
    ngate = w_in.shape[1] - C_GATE
    full = lambda a: pl.BlockSpec(a.shape, lambda i: (0,) * a.ndim)
    row_out = lambda n: pl.BlockSpec((tm, n), lambda i: (i, 0))
    if embed:
        assert FRONT == tm
        x, meta_pad, ln_g, ln_b = src
        lead_args = [x, meta_pad, ln_g, ln_b]
        lead_specs = [pl.BlockSpec((1, tm, d), lambda i: (i // per, jnp.maximum(i % per - 1, 0), 0)),
                      full(meta_pad), full(ln_g), full(ln_b)]
        lead_out, lead_out_specs = [jax.ShapeDtypeStruct((r, d), _F32)], [row_out(d)]
    else:
        lead_args, lead_specs = [src], [row_out(d)]
        lead_out, lead_out_specs = [], []
    out_shape = (
        lead_out
        + [jax.ShapeDtypeStruct((r, ngate), _BF16)]
        + [jax.ShapeDtypeStruct((r, w), _BF16)] * 5
        + [jax.ShapeDtypeStruct((nt * nchunk, 1, w), _F32)]
        + [jax.ShapeDtypeStruct((r, w), _BF16)] * 2
        + [jax.ShapeDtypeStruct((r, MLA_HEADS * QK_PAD), _BF16)] * 2
        + [jax.ShapeDtypeStruct((MLA_WIDTH, r), _BF16)]
    )
    out_specs = (
        lead_out_specs
        + [row_out(ngate)]
        + [row_out(w)] * 5
        + [pl.BlockSpec((nchunk, 1, w), lambda i: (i, 0, 0))]
        + [row_out(w)] * 2
        + [row_out(MLA_HEADS * QK_PAD)] * 2
        + [pl.BlockSpec((MLA_WIDTH, tm), lambda i: (0, i))]
    )
    return pl.pallas_call(
        functools.partial(_mixin_kernel, embed=embed, tp=tp),
        out_shape=out_shape,
        grid=(nt,),
        in_specs=lead_specs + [full(w_in), pl.BlockSpec((3, tm, LANES), lambda i: (0, i % nrope, 0))]
        + [full(a) for a in params],
        out_specs=out_specs,
        scratch_shapes=[pltpu.VMEM((8, RW_END), _F32)],
        compiler_params=_cparams("arbitrary"),
        name="mix_in",
    )(*lead_args, w_in, rope_tab, *params)


def _wkv_kernel(kt_ref, rt_ref, kd_ref, bd_ref, v_ref, pc_ref, y_ref, h_scr):
    c = pl.program_id(1)

    @pl.when(c < PADF // WKV_CHUNK)
    def _():
        h_scr[...] = jnp.zeros_like(h_scr)
        y_ref[...] = jnp.zeros_like(y_ref)

    @pl.when(c >= PADF // WKV_CHUNK)
    def _():
        _wkv_chunk(kt_ref, rt_ref, kd_ref, bd_ref, v_ref, pc_ref, y_ref, h_scr)


def _wkv_chunk(kt_ref, rt_ref, kd_ref, bd_ref, v_ref, pc_ref, y_ref, h_scr):
    npair = RWKV_HEADS // 2
    cl = WKV_CHUNK
    n2 = 2 * cl
    lane = lax.broadcasted_iota(jnp.int32, (1, LANES), 1)
    first = lane < HEAD_DIM
    row = lax.broadcasted_iota(jnp.int32, (n2, n2), 0)
    col = lax.broadcasted_iota(jnp.int32, (n2, n2), 1)
    strict = row > col
    incl = row >= col
    eye = (row == col).astype(_F32)

    def stack(x):
        zero = jnp.zeros_like(x)
        return jnp.concatenate([jnp.where(first, x, zero), jnp.where(first, zero, x)], axis=0)

    def same_block(size):
        sh = size.bit_length() - 1
        return jnp.right_shift(row, sh) == jnp.right_shift(col, sh)

    units = [(bl, j) for bl in range(kt_ref.shape[0]) for j in range(npair)]
    each = lambda fn, *lists: [fn(*args) for args in zip(*lists)]
    bf = lambda x: x.astype(_BF16)

    def load(ref):
        return [stack(ref[bl, :, j * LANES:(j + 1) * LANES]) for bl, j in units]

    kt2, rt2, kd2, bd2, v2 = load(kt_ref), load(rt_ref), load(kd_ref), load(bd_ref), load(v_ref)
    lhs = each(lambda a, b: jnp.concatenate([a, b], axis=0), kt2, rt2)
    rhs = each(lambda a, b: jnp.concatenate([a, b], axis=0), kd2, bd2)
    gram = each(_dot_nt, lhs, rhs)
    a_kk = [bf(jnp.where(strict, g[:n2, :n2], 0.0)) for g in gram]
    a_kb = [jnp.where(strict, g[:n2, n2:], 0.0) for g in gram]
    a_out = [bf(jnp.concatenate([jnp.where(incl, g[n2:, :n2], 0.0),
                                 jnp.where(incl, -g[n2:, n2:], 0.0)], axis=1)) for g in gram]

    blk = same_block(8)
    nd = [bf(jnp.where(blk, a, 0.0)) for a in a_kb]
    s2 = [bf(x) for x in each(_dot, nd, nd)]
    i_minus = [eye - x.astype(_F32) for x in nd]
    p1 = each(lambda im, s: im + _dot(bf(im), s), i_minus, s2)
    s4 = [bf(x) for x in each(_dot, s2, s2)]
    t_inv = each(lambda p, s: p + _dot(bf(p), s), p1, s4)
    size = 8
    while size < cl:
        outer = same_block(2 * size)
        sel = outer & jnp.logical_not(blk)
        off = [bf(jnp.where(sel, a, 0.0)) for a in a_kb]
        tb = [bf(t) for t in t_inv]
        m = [bf(x) for x in each(_dot, tb, off)]
        t_inv = each(lambda t, mm, tbb: t - _dot(mm, tbb), t_inv, m, tb)
        blk = outer
        size *= 2
    tb = [bf(t) for t in t_inv]
    akv = each(_dot, a_kk, v2)

    hbd = [h_scr[i] for i in range(len(units))]
    x0 = each(lambda l, h: _dot(l, bf(h)), lhs, hbd)
    u2 = each(lambda t, x, w: bf(_dot(t, bf(x[:n2] + w))), tb, x0, akv)
    vu = each(lambda a, b: jnp.concatenate([a, b], axis=0), v2, u2)
    y2 = each(lambda x, a, z: x[n2:] + _dot(a, z), x0, a_out, vu)
    upd = each(lambda kd, bd, z: _dot_tn(jnp.concatenate([kd, -bd], axis=0), z), kd2, bd2, vu)
    for i, (bl, j) in enumerate(units):
        sl = slice(j * LANES, (j + 1) * LANES)
        y_ref[bl, :, sl] = (y2[i][:cl] + y2[i][cl:]).astype(y_ref.dtype)
        pc_row = pc_ref[bl, 0, :, sl]
        pc_col = jnp.transpose(jnp.broadcast_to(pc_row, (LANES, LANES)))
        h_scr[i] = (hbd[i] + upd[i]) * pc_col


def _wkv(kt, rt, kd, bd, v, pc, bsz):
    r, w = kt.shape
    tp = r // bsz
    nc = tp // WKV_CHUNK
    nb = WKV_BATCH
    assert bsz % nb == 0
    blk = pl.BlockSpec((nb, WKV_CHUNK, w), lambda bg, c: (bg, c, 0))
    args = [a.reshape(bsz, tp, w) for a in (kt, rt, kd, bd, v)] + [pc.reshape(bsz, nc, 1, w)]
    y = pl.pallas_call(
        _wkv_kernel,
        out_shape=jax.ShapeDtypeStruct((bsz, tp, w), _F32),
        grid=(bsz // nb, nc),
        in_specs=[blk] * 5 + [pl.BlockSpec((nb, 1, 1, w), lambda bg, c: (bg, c, 0, 0))],
        out_specs=blk,
        scratch_shapes=[pltpu.VMEM((nb * (RWKV_HEADS // 2), LANES, LANES), _F32)],
        compiler_params=_cparams("parallel", "arbitrary"),
        name="wkv7",
    )(*args)
    return y.reshape(r, w)


def _attn_kernel(q_ref, k_ref, v_ref, o_ref, s0_scr, s1_scr, m_scr, acc_scr):
    i = pl.program_id(2)
    tq = q_ref.shape[0]
    tk = tq
    nhead = q_ref.shape[1] // QK_PAD
    q_pos = i * tq + lax.broadcasted_iota(jnp.int32, (1, tq), 1)
    head_cols = lambda n: slice(n * QK_PAD, (n + 1) * QK_PAD)
    qh = [q_ref[:, head_cols(n)] for n in range(nhead)]

    s_slots = (s0_scr, s1_scr)

    def scores(jb, slot):
        start = pl.multiple_of(jb * tk, tk)
        for n in range(nhead):
            s_slots[slot][n] = _dot_nt(k_ref[pl.ds(start, tk), head_cols(n)], qh[n])

    def accumulate(n, vt, s, first=False):
        m_new = jnp.max(s, axis=0, keepdims=True)
        if not first:
            m_prev = m_scr[n]
            m_new = jnp.maximum(m_prev, m_new)
        m_scr[n] = m_new
        prob = jnp.exp2(s - m_new).astype(_BF16)
        if prob.shape[0] < vt.shape[1]:
            prob = jnp.concatenate(
                [jnp.zeros((vt.shape[1] - prob.shape[0], tq), _BF16), prob], axis=0)
        lo = (n % 2) * V_HEAD_DIM
        vt_h = jnp.concatenate(
            [vt[lo:lo + V_HEAD_DIM], jnp.ones((BF16_ROWS, vt.shape[1]), _BF16)], axis=0)
        pv = _dot(vt_h, prob)
        acc_scr[n] = pv if first else jnp.exp2(m_prev - m_new) * acc_scr[n] + pv

    def update(jb, slot, masked):
        start = pl.multiple_of(jb * tk, tk)
        if masked:
            mask = (start + lax.broadcasted_iota(jnp.int32, (tk, 1), 0)) <= q_pos
        for n in range(nhead):
            vt = v_ref[(n // 2) * LANES:(n // 2 + 1) * LANES, pl.ds(start, tk)]
            s = s_slots[slot][n]
            if masked:
                s = jnp.where(mask, s, NEG_BIG)
            accumulate(n, vt, s)

    def front(s_front):
        mask = (PADF + lax.broadcasted_iota(jnp.int32, (N_META, 1), 0)) <= q_pos
        for n in range(nhead):
            vt = v_ref[(n // 2) * LANES:(n // 2 + 1) * LANES, 0:tk]
            accumulate(n, vt, jnp.where(mask, s_front[n], NEG_BIG), first=True)

    s_front = [_dot_nt(k_ref[PADF:FRONT, head_cols(n)], qh[n]) for n in range(nhead)]
    scores(1, 1)
    front(s_front)

    def two_blocks(pidx, carry):
        jb = 2 * pidx + 1
        scores(jb + 1, 0)
        update(jb, 1, False)
        scores(jb + 2, 1)
        update(jb + 1, 0, False)
        return carry

    def tail_even_i(_, carry):
        scores(i, 0)
        update(i - 1, 1, False)
        update(i, 0, True)
        return carry

    def tail_odd_i(_, carry):
        update(i, 1, True)
        return carry

    is_even = (i % 2 == 0)
    lax.fori_loop(0, (i - 1) // 2, two_blocks, 0)
    lax.fori_loop(0, jnp.where((i > 0) & is_even, 1, 0), tail_even_i, 0)
    lax.fori_loop(0, jnp.where(is_even, 0, 1), tail_odd_i, 0)
    for g in range(nhead // 2):
        halves = [acc_scr[n, :V_HEAD_DIM] / acc_scr[n, V_HEAD_DIM:V_HEAD_DIM + 1]
                  for n in (2 * g, 2 * g + 1)]
        o = jnp.concatenate(halves, axis=0)
        o_ref[:, g * LANES:(g + 1) * LANES] = jnp.transpose(o).astype(o_ref.dtype)


def _attention(q, k, v, bsz):
    r = q.shape[0]
    tp = r // bsz
    nq = tp // ATT_TILE
    ng = MLA_HEADS // 2 // ATT_PAIRS
    qw, vw = ATT_PAIRS * 2 * QK_PAD, ATT_PAIRS * 2 * V_HEAD_DIM
    return pl.pallas_call(
        _attn_kernel,
        out_shape=jax.ShapeDtypeStruct((r, MLA_WIDTH), _BF16),
        grid=(bsz, ng, nq),
        in_specs=[
            pl.BlockSpec((ATT_TILE, qw), lambda bi, pj, i: (bi * nq + i, pj)),
            pl.BlockSpec((tp, qw), lambda bi, pj, i: (bi, pj)),
            pl.BlockSpec((vw, tp), lambda bi, pj, i: (pj, bi)),
        ],
        out_specs=pl.BlockSpec((ATT_TILE, vw), lambda bi, pj, i: (bi * nq + i, pj)),
        scratch_shapes=[pltpu.VMEM((2 * ATT_PAIRS, ATT_TILE, ATT_TILE), _F32)] * 2
        + [pltpu.VMEM((2 * ATT_PAIRS, 1, ATT_TILE), _F32),
           pltpu.VMEM((2 * ATT_PAIRS, V_HEAD_DIM + BF16_ROWS, ATT_TILE), _F32)],
        compiler_params=_cparams("parallel", "parallel", "arbitrary"),
        name="mla_attention",
    )(q, k, v)


def _merge_kernel(y_ref, bonus_ref, g_ref, ymla_ref, gate_ref, h_ref, lnx_ref, head_sum_ref,
                  wpr_ref, wpm_ref, wout_ref, ln_ref, o_ref, *, tp, alpha):
    tm = y_ref.shape[0]
    d = h_ref.shape[1]
    head_sum = head_sum_ref[...]
    valid = _valid_rows(pl.program_id(0), tm, tp)
    subs = [slice(r0, r0 + MERGE_SUB) for r0 in range(0, tm, MERGE_SUB)]
    ys = [y_ref[s, :].astype(_F32) for s in subs]
    yms = [_head_sum(y, head_sum) * (1.0 / HEAD_DIM) for y in ys]
    ycs = [y - ym for y, ym in zip(ys, yms)]
    yvs = [_head_sum(yc * yc, head_sum) * (1.0 / HEAD_DIM) for yc in ycs]
    y_rwkv = [((yc * lax.rsqrt(yv + GN_EPS) * lnx_ref[0:1, :] + lnx_ref[1:2, :]
                + bonus_ref[s, :].astype(_F32)) * g_ref[s, :].astype(_F32)).astype(_BF16)
              for s, yc, yv in zip(subs, ycs, yvs)]
    pr = [_dot(yr, wpr_ref[...]) for yr in y_rwkv]
    pm = [_dot(ymla_ref[s, :], wpm_ref[...]) for s in subs]
    mixed = []
    for s, a, b in zip(subs, pr, pm):
        gates = _sigmoid(gate_ref[s, :].astype(_F32))
        mixed.append((gates[:, :d] * a + gates[:, d:] * b).astype(_BF16))
    outs = [_dot(m, wout_ref[...]) for m in mixed]
    for s, out in zip(subs, outs):
        hn = _layer_norm(alpha * h_ref[s, :] + out, ln_ref[0:1, :], ln_ref[1:2, :])
        o_ref[s, :] = jnp.where(valid[s], hn, 0.0)


def _merge(y, bonus, g, ymla, pgate, h, lnx, head_mean, wpr, wpm, wout, ln, tp, alpha):
    r, d = h.shape
    tm = ROW_TM
    w = RWKV_WIDTH
    full = lambda a: pl.BlockSpec(a.shape, lambda i: (0,) * a.ndim)
    rows = lambda n: pl.BlockSpec((tm, n), lambda i: (i, 0))
    return pl.pallas_call(
        functools.partial(_merge_kernel, tp=tp, alpha=alpha),
        out_shape=jax.ShapeDtypeStruct((r, d), _F32),
        grid=(r // tm,),
        in_specs=[rows(w), rows(w), rows(w), rows(MLA_WIDTH),
                  rows(2 * d),
                  rows(d), full(lnx), full(head_mean), full(wpr), full(wpm), full(wout), full(ln)],
        out_specs=rows(d),
        compiler_params=_cparams("parallel"),
        name="merge",
    )(y, bonus, g, ymla, pgate, h, lnx, head_mean, wpr, wpm, wout, ln)


def _ffn_kernel(h_ref, wup_ref, cw_ref, cb_ref, wdown_ref, ln_ref, o_ref, carry_ref, act_ref,
                pre_ref, *, tp, alpha):
    i = pl.program_id(0)
    tm, d = h_ref.shape
    dff = wdown_ref.shape[0]

    @pl.when(i == 0)
    def _():
        carry_ref[0:8, :] = jnp.zeros((8, carry_ref.shape[1]), _F32)
        pre_ref[...] = jnp.zeros_like(pre_ref)

    h = h_ref[...]
    hb = h.astype(_BF16)
    def conv(u, c0):
        cs = slice(c0, c0 + FF_CHUNK)
        carry_ref[8:, cs] = u
        u1 = carry_ref[7:7 + tm, cs]
        u2 = carry_ref[6:6 + tm, cs]
        out = (cw_ref[2:3, cs] * u + cw_ref[1:2, cs] * u1 + cw_ref[0:1, cs] * u2 + cb_ref[:, cs])
        carry_ref[0:8, cs] = u[tm - 8:, :]
        return out

    def up(c0):
        return (_dot(hb, wup_ref[:, c0:c0 + FF_CHUNK]),
                _dot(hb, wup_ref[:, dff + c0:dff + c0 + FF_CHUNK]))

    chunks = list(range(0, dff, FF_CHUNK))
    ahead = up(chunks[0])
    hn = _layer_norm(pre_ref[...], ln_ref[0:1, :], ln_ref[1:2, :])
    o_ref[...] = jnp.where(_valid_rows(i - 1, tm, tp), hn, 0.0).reshape(o_ref.shape)
    for n, c0 in enumerate(chunks):
        gate_raw, val_raw = ahead
        if n + 1 < len(chunks):
            ahead = up(chunks[n + 1])
        gate = conv(gate_raw, c0)
        val = conv(val_raw, dff + c0)
        act_ref[:, c0:c0 + FF_CHUNK] = (gate * _sigmoid(gate) * val).astype(_BF16)
    pre_ref[...] = alpha * h + _dot(act_ref[...], wdown_ref[...])


def _ffn(h, wup, cw, cb, wdown, ln, tp, alpha, bsz, final):
    r, d = h.shape
    tm = FFN_TM
    full = lambda a: pl.BlockSpec(a.shape, lambda i: (0,) * a.ndim, pipeline_mode=pl.Buffered(1))
    nt = r // tm
    done = lambda i: jnp.maximum(i - 1, 0)
    if final:
        per = tp // tm
        assert FRONT == tm
        out_shape = jax.ShapeDtypeStruct((bsz, tp - FRONT, d), _F32)
        out_spec = pl.BlockSpec(
            (1, tm, d), lambda i: (done(i) // per, jnp.maximum(done(i) % per - 1, 0), 0))
    else:
        out_shape = jax.ShapeDtypeStruct((r, d), _F32)
        out_spec = pl.BlockSpec((tm, d), lambda i: (done(i), 0))
    return pl.pallas_call(
        functools.partial(_ffn_kernel, tp=tp, alpha=alpha),
        out_shape=out_shape,
        grid=(nt + 1,),
        in_specs=[pl.BlockSpec((tm, d), lambda i: (jnp.minimum(i, nt - 1), 0)),
                  full(wup), full(cw), full(cb), full(wdown), full(ln)],
        out_specs=out_spec,
        scratch_shapes=[pltpu.VMEM((8 + tm, wup.shape[1]), _F32),
                        pltpu.VMEM((tm, wdown.shape[0]), _BF16),
                        pltpu.VMEM((tm, d), _F32)],
        compiler_params=_cparams("arbitrary"),
        name="conv_ffn",
    )(h, wup, cw, cb, wdown, ln)


def _pad_cols(a, n):
    return jnp.pad(a, ((0, 0), (0, n - a.shape[1])))


def _rearrange_in_cols(a):
    w = RWKV_WIDTH
    o_mla = 3 * w + DECAY_LORA + ICLR_LORA + GATE_LORA
    o_gate = o_mla + Q_LORA_RANK + KV_LORA_RANK + QK_ROPE_DIM
    rows = a.shape[0]
    z = lambda n: jnp.zeros((rows, n), a.dtype)
    kr = a[:, o_mla + Q_LORA_RANK + KV_LORA_RANK:o_gate]
    parts = [
        a[:, :3 * w + DECAY_LORA + ICLR_LORA],
        a[:, 3 * w + DECAY_LORA + ICLR_LORA:o_mla], z(GATE_LORA_PAD - GATE_LORA),
        a[:, o_mla:o_mla + Q_LORA_RANK + KV_LORA_RANK],
        z(QK_NOPE_DIM), kr, z(LANES - QK_NOPE_DIM - QK_ROPE_DIM),
        a[:, o_gate:],
    ]
    return jnp.concatenate(parts, axis=1)


def _pad_heads(wmat, per_head, lead):
    kdim = wmat.shape[0]
    wh = wmat.reshape(kdim, MLA_HEADS, per_head)
    wh = jnp.pad(wh, ((0, 0), (0, 0), (lead, QK_PAD - per_head - lead)))
    return wh.reshape(kdim, MLA_HEADS * QK_PAD)


def _rope_tables(tp, reps):
    half = QK_ROPE_DIM // 2
    pos = (jnp.arange(tp, dtype=jnp.int32) - PADF).astype(_F32)
    inv_freq = ROPE_THETA ** (-jnp.arange(half, dtype=_F32) / half)
    ang = pos[:, None] * inv_freq[None, :]
    cos, sin = jnp.cos(ang), jnp.sin(ang)
    zeros = lambda n: jnp.zeros((tp, n), _F32)
    tail = LANES - QK_NOPE_DIM - QK_ROPE_DIM
    cos_t = jnp.concatenate([jnp.ones((tp, QK_NOPE_DIM), _F32), cos, cos, zeros(tail)], axis=1)
    sin_a = jnp.concatenate([zeros(QK_NOPE_DIM), -sin, zeros(half + tail)], axis=1)
    sin_b = jnp.concatenate([zeros(QK_NOPE_DIM + half), sin, zeros(tail)], axis=1)
    tab = jnp.stack([cos_t, sin_a, sin_b])
    return jnp.tile(tab, (1, reps, 1))


def kernel(x, meta_tokens, ln_in_g, ln_in_b, w_in, mu_shift, w0, w_lora_up, a0, a_lora_up,
           g_lora_up, k_k, k_a, r_k, lnx_g, lnx_b, q_norm_g, w_uq, kv_norm_g, w_uk, w_uv,
           w_proj_rwkv, w_proj_mla, w_out, ln1_g, ln1_b, w_up, conv_w, conv_b, w_down,
           ln2_g, ln2_b):
    bsz, seq, d = x.shape
    depth = w_in.shape[0]
    tp = seq + FRONT
    r = bsz * tp
    assert tp % ATT_TILE == 0 and r % ROW_TM == 0 and r % PREP_TM == 0
    alpha = float((2 * depth) ** 0.25)
    w = RWKV_WIDTH

    meta_pad = jnp.pad(meta_tokens.astype(x.dtype), ((PADF, 0), (0, 0)))
    h = (x, meta_pad, ln_in_g[None], ln_in_b[None])

    hid = jnp.arange(MXU_DIM) // HEAD_DIM
    head_sum = (hid[:, None] == hid[None, :]).astype(_BF16)
    tt = jnp.arange(PREP_TM)
    tri = ((tt[:, None] >= tt[None, :])
           & (tt[:, None] // WKV_CHUNK == tt[None, :] // WKV_CHUNK)).astype(_BF16)

    def rope_for(tm):
        reps = 1
        while (reps * tp) % tm:
            reps += 1
        return _rope_tables(tp, reps)

    for l in range(depth):
        w_in_l = _rearrange_in_cols(w_in[l]).astype(_BF16)
        mu = _rearrange_in_cols(jnp.pad(mu_shift[l][None], ((0, 0), (0, w_in.shape[2] - mu_shift.shape[1]))))
        mu = mu[:, :RW_END]
        vec = jnp.stack([w0[l], a0[l], k_k[l], k_a[l]])
        wa_up = jnp.concatenate([
            jnp.concatenate([w_lora_up[l], jnp.zeros((DECAY_LORA, w), _F32)], axis=1),
            jnp.concatenate([jnp.zeros((ICLR_LORA, w), _F32), a_lora_up[l]], axis=1)], axis=0)
        g_up = jnp.pad(g_lora_up[l], ((0, GATE_LORA_PAD - GATE_LORA), (0, 0)))
        wuq = _pad_heads(w_uq[l], QK_NOPE_DIM + QK_ROPE_DIM, 0)
        wuk = _pad_heads(w_uk[l], QK_NOPE_DIM, 0)

        params = (mu, vec, r_k[l].reshape(1, w), wa_up.astype(_BF16), g_up.astype(_BF16),
                  head_sum, tri, q_norm_g[l][None], kv_norm_g[l][None],
                  wuq.astype(_BF16), wuk.astype(_BF16), w_uv[l].T.astype(_BF16))
        tm = PREP_TM if l == 0 else PREP_TM_LATER
        outs = _mixin(h, w_in_l, rope_for(tm), params, tp, bsz, tm)
        if l == 0:
            h, outs = outs[0], outs[1:]
        pgate, kt, rt, kd, bd, v, pc, bonus, g, q, kk, vv = outs
        y = _wkv(kt, rt, kd, bd, v, pc, bsz)
        ymla = _attention(q, kk, vv, bsz)
        h = _merge(y, bonus, g, ymla, pgate, h, jnp.stack([lnx_g[l], lnx_b[l]]), head_sum,
                   w_proj_rwkv[l].astype(_BF16), w_proj_mla[l].astype(_BF16),
                   w_out[l].astype(_BF16), jnp.stack([ln1_g[l], ln1_b[l]]), tp, alpha)
        h = _ffn(h, w_up[l].astype(_BF16), conv_w[l], conv_b[l][None], w_down[l].astype(_BF16),
                 jnp.stack([ln2_g[l], ln2_b[l]]), tp, alpha, bsz, final=(l == depth - 1))
    return h
```

```python
import functools

import numpy as np
import jax
import jax.numpy as jnp
from jax import lax
from jax.experimental import pallas as pl
from jax.experimental.pallas import tpu as pltpu

N_META = 16
HEAD_DIM = 64
RWKV_HEADS = 8
RWKV_WIDTH = RWKV_HEADS * HEAD_DIM
DECAY_LORA = 64
ICLR_LORA = 64
GATE_LORA = 160
GN_EPS = 64e-5
MLA_HEADS = 8
QK_NOPE_DIM = 64
QK_ROPE_DIM = 32
V_HEAD_DIM = 64
Q_LORA_RANK = 256
KV_LORA_RANK = 256
ROPE_THETA = 10000.0
MLA_WIDTH = MLA_HEADS * V_HEAD_DIM
CONV_WIDTH = 3
LN_EPS = 1e-5
RMS_EPS = 1e-6

LANES = 128
MXU_DIM = 256
VMEM_LIMIT = 56 * 1024 * 1024

FRONT = MXU_DIM
PADF = FRONT - N_META
C_R, C_K, C_V = 0, RWKV_WIDTH, 2 * RWKV_WIDTH
C_WA = 3 * RWKV_WIDTH
GATE_LORA_PAD = 2 * LANES
C_G = C_WA + LANES
RW_END = C_G + GATE_LORA_PAD
C_CQ = RW_END
C_CKV = C_CQ + Q_LORA_RANK
C_KR = C_CKV + KV_LORA_RANK
C_GATE = C_KR + LANES
QK_PAD = LANES

WKV_CHUNK = 64
WKV_BATCH = 4
ATT_TILE = MXU_DIM
ATT_UNROLL = 4
ATT_PAIRS = 4
PREP_TM = 256
PREP_TM_LATER = 256
MIX_SUBTILES = 2
ROW_TM = 512
FFN_TM = 256
MERGE_SUB = 256
FF_CHUNK = 256
NEG_BIG = -1e30
BF16_ROWS = 16

_F32 = jnp.float32
_BF16 = jnp.bfloat16


def _cparams(*sem):
    return pltpu.CompilerParams(dimension_semantics=sem, vmem_limit_bytes=VMEM_LIMIT)


def _dot(a, b):
    return jnp.dot(a, b, preferred_element_type=_F32)


def _dot_nt(a, b):
    return lax.dot_general(a, b, (((1,), (1,)), ((), ())), preferred_element_type=_F32)


def _dot_tn(a, b):
    return lax.dot_general(a, b, (((0,), (0,)), ((), ())), preferred_element_type=_F32)


def _split_dot(x, w_bf16, parts, left=False):
    acc = None
    rem = x
    for _ in range(parts):
        hi = rem.astype(_BF16)
        term = _dot(w_bf16, hi) if left else _dot(hi, w_bf16)
        acc = term if acc is None else acc + term
        rem = rem - hi.astype(_F32)
    return acc


def _head_sum(x, ones_bd):
    n = ones_bd.shape[0]
    xb = x.astype(_BF16)
    return jnp.concatenate([_dot(xb[:, c0:c0 + n], ones_bd) for c0 in range(0, x.shape[1], n)],
                           axis=1)


def _layer_norm(x, g, b):
    mu = jnp.mean(x, axis=-1, keepdims=True)
    xc = x - mu
    var = jnp.mean(xc * xc, axis=-1, keepdims=True)
    return xc * lax.rsqrt(var + LN_EPS) * g + b


def _sigmoid(x):
    return 1.0 / (1.0 + jnp.exp(-x))


def _valid_rows(tile_idx, tm, tp):
    row = tile_idx * tm + lax.broadcasted_iota(jnp.int32, (tm, 1), 0)
    bidx = jnp.floor((row.astype(_F32) + 0.5) * (1.0 / tp)).astype(jnp.int32)
    return (row - bidx * tp) >= PADF


def _mixin_multi_kernel(*refs, embed, tp, nsub):
    i = pl.program_id(0)
    n_in = (nsub + 3 if embed else 1) + 14
    ins, outs, carry_ref = refs[:n_in], refs[n_in:-1], refs[-1]

    @pl.when(i == 0)
    def _():
        carry_ref[...] = jnp.zeros_like(carry_ref)

    for sub in range(nsub):
        t_rows = outs[-2].shape[0] // nsub
        rs = pl.ds(sub * t_rows, t_rows)
        if embed:
            lead = (ins[sub],) + ins[nsub:nsub + 3]
            rest = ins[nsub + 3:]
        else:
            lead = (ins[0].at[rs],)
            rest = ins[1:]
        rest = (rest[0], rest[1].at[:, rs]) + rest[2:]
        n_pc = outs[-6].shape[0] // nsub
        row_outs = [o.at[pl.ds(sub * n_pc, n_pc)] if o is outs[-6] else o.at[rs]
                    for o in outs[:-1]]
        vv_view = outs[-1].at[:, rs]
        _mixin_kernel(*lead, *rest, *row_outs, vv_view, carry_ref,
                      embed=embed, tp=tp, tile=i * nsub + sub)


def _mixin_kernel(*refs, embed, tp, tile):
    if embed:
        x_ref, meta_ref, lng_ref, lnb_ref = refs[:4]
        refs = refs[4:]
    else:
        h_ref = refs[0]
        refs = refs[1:]
    (w_ref, rope_ref, mu_ref, vec_ref, rk_ref, wa_up_ref, g_up_ref, head_sum_ref, tri_ref,
     qg_ref, kvg_ref, wuq_ref, wuk_ref, wuv_ref) = refs[:14]
    refs = refs[14:]
    if embed:
        h_out_ref = refs[0]
        refs = refs[1:]
    (gate_ref, kt_ref, rt_ref, kd_ref, bd_ref, v_ref, pc_ref, bonus_ref, g_ref,
     q_ref, kk_ref, vv_ref, carry_ref) = refs
    i = tile
    tm = kt_ref.shape[0]
    w = RWKV_WIDTH

    if embed:
        src = jnp.where(i % (tp // tm) == 0, meta_ref[...], x_ref[0])
        h = _layer_norm(src, lng_ref[...], lnb_ref[...])
        h = jnp.where(_valid_rows(i, tm, tp), h, 0.0)
        h_out_ref[...] = h
    else:
        h = h_ref[...]
    hb = h.astype(_BF16)

    def gate_cols(c0, c1):
        gate_ref[:, c0:c1] = _dot(hb, w_ref[:, C_GATE + c0:C_GATE + c1]).astype(gate_ref.dtype)

    ngate = gate_ref.shape[1]
    gq = ngate // 4

    pr = _dot(hb, w_ref[:, :RW_END])
    gate_cols(0, gq)
    shifted = pltpu.roll(pr, 1, axis=0)
    row = lax.broadcasted_iota(jnp.int32, (tm, 1), 0)
    shifted = jnp.where(row == 0, carry_ref[7:8, :], shifted)
    carry_ref[...] = pr[tm - 8:, :]
    ps = pr + (shifted - pr) * mu_ref[...]

    r = ps[:, C_R:C_R + w]
    k = ps[:, C_K:C_K + w]
    v = ps[:, C_V:C_V + w]
    wa_in = ps[:, C_WA:C_WA + LANES]
    lane = lax.broadcasted_iota(jnp.int32, (1, LANES), 1)
    wa_in = jnp.where(lane < DECAY_LORA, jnp.tanh(wa_in), wa_in)
    wa = _dot(wa_in.astype(_BF16), wa_up_ref[...])
    w0, a0, k_k, k_a = (vec_ref[0:1, :], vec_ref[1:2, :], vec_ref[2:3, :], vec_ref[3:4, :])

    z = -(w0 + wa[:, :w])
    softplus = jnp.maximum(z, 0.0) + jnp.log1p(jnp.exp(-jnp.abs(z)))
    logw = -jnp.exp(-softplus - 0.5)
    a = _sigmoid(a0 + wa[:, w:])
    gate = _dot(_sigmoid(ps[:, C_G:C_G + GATE_LORA_PAD]).astype(_BF16), g_up_ref[...])
    gate_cols(gq, 2 * gq)

    head_sum = head_sum_ref[...]
    kk = k * k_k
    kk_ss = _head_sum(kk * kk, head_sum)
    kk = kk / jnp.maximum(jnp.sqrt(kk_ss), 1e-12)
    k = k * (1.0 + (a - 1.0) * k_a)
    b = kk * a
    bonus = _head_sum(r * k * rk_ref[...], head_sum) * v
    gate_cols(2 * gq, 3 * gq)

    tri = tri_ref[...]
    tn = tri.shape[0]
    cum = jnp.concatenate([_split_dot(logw[r0:r0 + tn], tri, 3, left=True)
                           for r0 in range(0, tm, tn)], axis=0)
    pm = _dot(hb, w_ref[:, C_CQ:C_GATE])
    e_neg = jnp.exp(-cum)
    kt_ref[...] = (kk * jnp.exp(cum - logw)).astype(_BF16)
    rt_ref[...] = (r * jnp.exp(cum)).astype(_BF16)
    kd_ref[...] = (k * e_neg).astype(_BF16)
    bd_ref[...] = (b * e_neg).astype(_BF16)
    v_ref[...] = v.astype(_BF16)
    bonus_ref[...] = bonus.astype(_BF16)
    g_ref[...] = gate.astype(_BF16)
    for ci in range(tm // WKV_CHUNK):
        last = (ci + 1) * WKV_CHUNK - 1
        pc_ref[ci] = jnp.exp(cum[last:last + 1, :])

    cos_t, sin_a, sin_b = rope_ref[0], rope_ref[1], rope_ref[2]

    def rope(x):
        nrep = x.shape[1] // LANES
        n = x.shape[1]
        c, sa, sb = (jnp.tile(t, (1, nrep)) if nrep > 1 else t for t in (cos_t, sin_a, sin_b))
        half = QK_ROPE_DIM // 2
        return x * c + pltpu.roll(x, n - half, axis=1) * sa + pltpu.roll(x, half, axis=1) * sb

    def rms(x, gain):
        return x * lax.rsqrt(jnp.mean(x * x, axis=-1, keepdims=True) + RMS_EPS) * gain

    cq = pm[:, :Q_LORA_RANK]
    ckv = pm[:, Q_LORA_RANK:Q_LORA_RANK + KV_LORA_RANK]
    kr = pm[:, C_KR - C_CQ:]
    qn = rms(cq, qg_ref[...]).astype(_BF16)
    scale = float(np.log2(np.e) / np.sqrt(QK_NOPE_DIM + QK_ROPE_DIM))
    q_ref[...] = (rope(_dot(qn, wuq_ref[...])) * scale).astype(_BF16)
    kvn = rms(ckv, kvg_ref[...]).astype(_BF16)
    gate_cols(3 * gq, ngate)
    kk_ref[...] = (_dot(kvn, wuk_ref[...]) + jnp.tile(rope(kr), (1, MLA_HEADS))).astype(_BF16)
    vv_ref[...] = _dot_nt(wuv_ref[...], kvn).astype(_BF16)


def _mixin(src, w_in, rope_tab, params, tp, bsz, tm):
    embed = isinstance(src, tuple)
    nsub = MIX_SUBTILES
    sub_rows = tm
    tm = sub_rows * nsub
    d = w_in.shape[0]
    r = bsz * tp
    nt = r // tm
    per = tp // sub_rows
    nrope = rope_tab.shape[1] // tm
    w = RWKV_WIDTH
    nchunk = tm // WKV_CHUNK
    ngate = w_in.shape[1] - C_GATE
    full = lambda a: pl.BlockSpec(a.shape, lambda i: (0,) * a.ndim)
    row_out = lambda n: pl.BlockSpec((tm, n), lambda i: (i, 0))
    if embed:
        assert FRONT == sub_rows
        x, meta_pad, ln_g, ln_b = src
        lead_args = [x] * nsub + [meta_pad, ln_g, ln_b]

        def x_spec(sub):
            tile = lambda i: i * nsub + sub
            return pl.BlockSpec(
                (1, sub_rows, d), lambda i: (tile(i) // per, jnp.maximum(tile(i) % per - 1, 0), 0))

        lead_specs = [x_spec(sub) for sub in range(nsub)] + [full(meta_pad), full(ln_g), full(ln_b)]
        lead_out, lead_out_specs = [jax.ShapeDtypeStruct((r, d), _F32)], [row_out(d)]
    else:
        lead_args, lead_specs = [src], [row_out(d)]
        lead_out, lead_out_specs = [], []
    out_shape = (
        lead_out
        + [jax.ShapeDtypeStruct((r, ngate), _BF16)]
        + [jax.ShapeDtypeStruct((r, w), _BF16)] * 5
        + [jax.ShapeDtypeStruct((nt * nchunk, 1, w), _F32)]
        + [jax.ShapeDtypeStruct((r, w), _BF16)] * 2
        + [jax.ShapeDtypeStruct((r, MLA_HEADS * QK_PAD), _BF16)] * 2
        + [jax.ShapeDtypeStruct((MLA_WIDTH, r), _BF16)]
    )
    out_specs = (
        lead_out_specs
        + [row_out(ngate)]
        + [row_out(w)] * 5
        + [pl.BlockSpec((nchunk, 1, w), lambda i: (i, 0, 0))]
        + [row_out(w)] * 2
        + [row_out(MLA_HEADS * QK_PAD)] * 2
        + [pl.BlockSpec((MLA_WIDTH, tm), lambda i: (0, i))]
    )
    return pl.pallas_call(
        functools.partial(_mixin_multi_kernel, embed=embed, tp=tp, nsub=nsub),
        out_shape=out_shape,
        grid=(nt,),
        in_specs=lead_specs + [full(w_in), pl.BlockSpec((3, tm, LANES), lambda i: (0, i % nrope, 0))]
        + [full(a) for a in params],
        out_specs=out_specs,
        scratch_shapes=[pltpu.VMEM((8, RW_END), _F32)],
        compiler_params=_cparams("arbitrary"),
        name="mix_in",
    )(*lead_args, w_in, rope_tab, *params)


def _wkv_kernel(kt_ref, rt_ref, kd_ref, bd_ref, v_ref, pc_ref, y_ref, h_scr):
    c = pl.program_id(1)

    @pl.when(c < PADF // WKV_CHUNK)
    def _():
        h_scr[...] = jnp.zeros_like(h_scr)
        y_ref[...] = jnp.zeros_like(y_ref)

    @pl.when(c >= PADF // WKV_CHUNK)
    def _():
        _wkv_chunk(kt_ref, rt_ref, kd_ref, bd_ref, v_ref, pc_ref, y_ref, h_scr)


def _wkv_chunk(kt_ref, rt_ref, kd_ref, bd_ref, v_ref, pc_ref, y_ref, h_scr):
    npair = RWKV_HEADS // 2
    cl = WKV_CHUNK
    n2 = 2 * cl
    lane = lax.broadcasted_iota(jnp.int32, (1, LANES), 1)
    first = lane < HEAD_DIM
    row = lax.broadcasted_iota(jnp.int32, (n2, n2), 0)
    col = lax.broadcasted_iota(jnp.int32, (n2, n2), 1)
    strict = row > col
    incl = row >= col
    eye = (row == col).astype(_F32)

    def stack(x):
        zero = jnp.zeros_like(x)
        return jnp.concatenate([jnp.where(first, x, zero), jnp.where(first, zero, x)], axis=0)

    def same_block(size):
        sh = size.bit_length() - 1
        return jnp.right_shift(row, sh) == jnp.right_shift(col, sh)

    units = [(bl, j) for bl in range(kt_ref.shape[0]) for j in range(npair)]
    each = lambda fn, *lists: [fn(*args) for args in zip(*lists)]
    bf = lambda x: x.astype(_BF16)

    def load(ref):
        return [stack(ref[bl, :, j * LANES:(j + 1) * LANES]) for bl, j in units]

    kt2, rt2, kd2, bd2, v2 = load(kt_ref), load(rt_ref), load(kd_ref), load(bd_ref), load(v_ref)
    lhs = each(lambda a, b: jnp.concatenate([a, b], axis=0), kt2, rt2)
    rhs = each(lambda a, b: jnp.concatenate([a, b], axis=0), kd2, bd2)
    gram = each(_dot_nt, lhs, rhs)
    a_kk = [bf(jnp.where(strict, g[:n2, :n2], 0.0)) for g in gram]
    a_kb = [jnp.where(strict, g[:n2, n2:], 0.0) for g in gram]
    a_out = [bf(jnp.concatenate([jnp.where(incl, g[n2:, :n2], 0.0),
                                 jnp.where(incl, -g[n2:, n2:], 0.0)], axis=1)) for g in gram]

    blk = same_block(8)
    nd = [bf(jnp.where(blk, a, 0.0)) for a in a_kb]
    s2 = [bf(x) for x in each(_dot, nd, nd)]
    i_minus = [eye - x.astype(_F32) for x in nd]
    p1 = each(lambda im, s: im + _dot(bf(im), s), i_minus, s2)
    s4 = [bf(x) for x in each(_dot, s2, s2)]
    t_inv = each(lambda p, s: p + _dot(bf(p), s), p1, s4)
    size = 8
    while size < cl:
        outer = same_block(2 * size)
        sel = outer & jnp.logical_not(blk)
        off = [bf(jnp.where(sel, a, 0.0)) for a in a_kb]
        tb = [bf(t) for t in t_inv]
        m = [bf(x) for x in each(_dot, tb, off)]
        t_inv = each(lambda t, mm, tbb: t - _dot(mm, tbb), t_inv, m, tb)
        blk = outer
        size *= 2
    tb = [bf(t) for t in t_inv]
    akv = each(_dot, a_kk, v2)

    hbd = [h_scr[i] for i in range(len(units))]
    x0 = each(lambda l, h: _dot(l, bf(h)), lhs, hbd)
    u2 = each(lambda t, x, w: bf(_dot(t, bf(x[:n2] + w))), tb, x0, akv)
    vu = each(lambda a, b: jnp.concatenate([a, b], axis=0), v2, u2)
    y2 = each(lambda x, a, z: x[n2:] + _dot(a, z), x0, a_out, vu)
    upd = each(lambda kd, bd, z: _dot_tn(jnp.concatenate([kd, -bd], axis=0), z), kd2, bd2, vu)
    for i, (bl, j) in enumerate(units):
        sl = slice(j * LANES, (j + 1) * LANES)
        y_ref[bl, :, sl] = (y2[i][:cl] + y2[i][cl:]).astype(y_ref.dtype)
        pc_row = pc_ref[bl, 0, :, sl]
        pc_col = jnp.transpose(jnp.broadcast_to(pc_row, (LANES, LANES)))
        h_scr[i] = (hbd[i] + upd[i]) * pc_col


def _wkv(kt, rt, kd, bd, v, pc, bsz):
    r, w = kt.shape
    tp = r // bsz
    nc = tp // WKV_CHUNK
    nb = WKV_BATCH
    assert bsz % nb == 0
    blk = pl.BlockSpec((nb, WKV_CHUNK, w), lambda bg, c: (bg, c, 0))
    args = [a.reshape(bsz, tp, w) for a in (kt, rt, kd, bd, v)] + [pc.reshape(bsz, nc, 1, w)]
    y = pl.pallas_call(
        _wkv_kernel,
        out_shape=jax.ShapeDtypeStruct((bsz, tp, w), _F32),
        grid=(bsz // nb, nc),
        in_specs=[blk] * 5 + [pl.BlockSpec((nb, 1, 1, w), lambda bg, c: (bg, c, 0, 0))],
        out_specs=blk,
        scratch_shapes=[pltpu.VMEM((nb * (RWKV_HEADS // 2), LANES, LANES), _F32)],
        compiler_params=_cparams("parallel", "arbitrary"),
        name="wkv7",
    )(*args)
    return y.reshape(r, w)


def _attn_kernel(q_ref, k_ref, v_ref, o_ref, s_scr, m_scr, acc_scr):
    i = pl.program_id(2)
    tq = q_ref.shape[0]
    tk = tq
    nhead = q_ref.shape[1] // QK_PAD
    q_pos = i * tq + lax.broadcasted_iota(jnp.int32, (1, tq), 1)
    head_cols = lambda n: slice(n * QK_PAD, (n + 1) * QK_PAD)
    qh = [q_ref[:, head_cols(n)] for n in range(nhead)]

    def scores(jb, n):
        start = pl.multiple_of(jb * tk, tk)
        s_scr[n] = _dot_nt(k_ref[pl.ds(start, tk), head_cols(n)], qh[n])

    def softmax(n, s, first=False):
        m_new = jnp.max(s, axis=0, keepdims=True)
        rescale = None
        if not first:
            m_prev = m_scr[n]
            m_new = jnp.maximum(m_prev, m_new)
            rescale = jnp.exp2(m_prev - m_new)
        m_scr[n] = m_new
        return jnp.exp2(s - m_new).astype(_BF16), rescale

    def values(n, start, prob, rescale):
        if prob.shape[0] < tk:
            prob = jnp.concatenate([jnp.zeros((tk - prob.shape[0], tq), _BF16), prob], axis=0)
        lo = (n // 2) * LANES + (n % 2) * V_HEAD_DIM
        vt_h = jnp.concatenate([v_ref[lo:lo + V_HEAD_DIM, pl.ds(start, tk)],
                                jnp.ones((BF16_ROWS, tk), _BF16)], axis=0)
        pv = _dot(vt_h, prob)
        acc_scr[n] = pv if rescale is None else rescale * acc_scr[n] + pv

    ahead = 6
    behind = 1

    def block(jb, masked, last):
        start = pl.multiple_of(jb * tk, tk)
        if masked:
            mask = (start + lax.broadcasted_iota(jnp.int32, (tk, 1), 0)) <= q_pos
        pending = []
        for n in range(nhead):
            if n + ahead < nhead:
                scores(jb, n + ahead)
            elif not last:
                scores(jb + 1, n + ahead - nhead)
            s = s_scr[n]
            if masked:
                s = jnp.where(mask, s, NEG_BIG)
            pending.append((n,) + softmax(n, s))
            if len(pending) > behind:
                m, prob, rescale = pending.pop(0)
                values(m, start, prob, rescale)
        for m, prob, rescale in pending:
            values(m, start, prob, rescale)

    def front(s_front):
        mask = (PADF + lax.broadcasted_iota(jnp.int32, (N_META, 1), 0)) <= q_pos
        for n in range(nhead):
            prob, _ = softmax(n, jnp.where(mask, s_front[n], NEG_BIG), first=True)
            values(n, 0, prob, None)

    s_front = [_dot_nt(k_ref[PADF:FRONT, head_cols(n)], qh[n]) for n in range(nhead)]
    for n in range(ahead):
        scores(1, n)
    front(s_front)

    def several_blocks(pidx, carry):
        for u in range(ATT_UNROLL):
            block(ATT_UNROLL * pidx + 1 + u, False, False)
        return carry

    def one_block(jb, carry):
        block(jb, False, False)
        return carry

    def diagonal(_, carry):
        block(i, True, True)
        return carry

    n_plain = jnp.maximum(i - 1, 0)
    n_rolled = n_plain // ATT_UNROLL
    lax.fori_loop(0, n_rolled, several_blocks, 0)
    lax.fori_loop(ATT_UNROLL * n_rolled + 1, n_plain + 1, one_block, 0)
    lax.fori_loop(0, jnp.minimum(i, 1), diagonal, 0)
    for g in range(nhead // 2):
        halves = [acc_scr[n, :V_HEAD_DIM] / acc_scr[n, V_HEAD_DIM:V_HEAD_DIM + 1]
                  for n in (2 * g, 2 * g + 1)]
        o = jnp.concatenate(halves, axis=0)
        o_ref[:, g * LANES:(g + 1) * LANES] = jnp.transpose(o).astype(o_ref.dtype)


def _attention(q, k, v, bsz):
    r = q.shape[0]
    tp = r // bsz
    nq = tp // ATT_TILE
    ng = MLA_HEADS // 2 // ATT_PAIRS
    qw, vw = ATT_PAIRS * 2 * QK_PAD, ATT_PAIRS * 2 * V_HEAD_DIM
    return pl.pallas_call(
        _attn_kernel,
        out_shape=jax.ShapeDtypeStruct((r, MLA_WIDTH), _BF16),
        grid=(bsz, ng, nq),
        in_specs=[
            pl.BlockSpec((ATT_TILE, qw), lambda bi, pj, i: (bi * nq + i, pj)),
            pl.BlockSpec((tp, qw), lambda bi, pj, i: (bi, pj)),
            pl.BlockSpec((vw, tp), lambda bi, pj, i: (pj, bi)),
        ],
        out_specs=pl.BlockSpec((ATT_TILE, vw), lambda bi, pj, i: (bi * nq + i, pj)),
        scratch_shapes=[pltpu.VMEM((2 * ATT_PAIRS, ATT_TILE, ATT_TILE), _F32)]
        + [pltpu.VMEM((2 * ATT_PAIRS, 1, ATT_TILE), _F32),
           pltpu.VMEM((2 * ATT_PAIRS, V_HEAD_DIM + BF16_ROWS, ATT_TILE), _F32)],
        compiler_params=_cparams("parallel", "parallel", "arbitrary"),
        name="mla_attention",
    )(q, k, v)


def _merge_kernel(y_ref, bonus_ref, g_ref, ymla_ref, gate_ref, h_ref, lnx_ref, head_sum_ref,
                  wpr_ref, wpm_ref, wout_ref, ln_ref, o_ref, *, tp, alpha):
    tm = y_ref.shape[0]
    d = h_ref.shape[1]
    head_sum = head_sum_ref[...]
    valid = _valid_rows(pl.program_id(0), tm, tp)
    subs = [slice(r0, r0 + MERGE_SUB) for r0 in range(0, tm, MERGE_SUB)]
    ys = [y_ref[s, :].astype(_F32) for s in subs]
    yms = [_head_sum(y, head_sum) * (1.0 / HEAD_DIM) for y in ys]
    ycs = [y - ym for y, ym in zip(ys, yms)]
    yvs = [_head_sum(yc * yc, head_sum) * (1.0 / HEAD_DIM) for yc in ycs]
    y_rwkv = [((yc * lax.rsqrt(yv + GN_EPS) * lnx_ref[0:1, :] + lnx_ref[1:2, :]
                + bonus_ref[s, :].astype(_F32)) * g_ref[s, :].astype(_F32)).astype(_BF16)
              for s, yc, yv in zip(subs, ycs, yvs)]
    pr = [_dot(yr, wpr_ref[...]) for yr in y_rwkv]
    pm = [_dot(ymla_ref[s, :], wpm_ref[...]) for s in subs]
    mixed = []
    for s, a, b in zip(subs, pr, pm):
        gates = _sigmoid(gate_ref[s, :].astype(_F32))
        mixed.append((gates[:, :d] * a + gates[:, d:] * b).astype(_BF16))
    outs = [_dot(m, wout_ref[...]) for m in mixed]
    for s, out in zip(subs, outs):
        hn = _layer_norm(alpha * h_ref[s, :] + out, ln_ref[0:1, :], ln_ref[1:2, :])
        o_ref[s, :] = jnp.where(valid[s], hn, 0.0)


def _merge(y, bonus, g, ymla, pgate, h, lnx, head_mean, wpr, wpm, wout, ln, tp, alpha):
    r, d = h.shape
    tm = ROW_TM
    w = RWKV_WIDTH
    full = lambda a: pl.BlockSpec(a.shape, lambda i: (0,) * a.ndim)
    rows = lambda n: pl.BlockSpec((tm, n), lambda i: (i, 0))
    return pl.pallas_call(
        functools.partial(_merge_kernel, tp=tp, alpha=alpha),
        out_shape=jax.ShapeDtypeStruct((r, d), _F32),
        grid=(r // tm,),
        in_specs=[rows(w), rows(w), rows(w), rows(MLA_WIDTH),
                  rows(2 * d),
                  rows(d), full(lnx), full(head_mean), full(wpr), full(wpm), full(wout), full(ln)],
        out_specs=rows(d),
        compiler_params=_cparams("parallel"),
        name="merge",
    )(y, bonus, g, ymla, pgate, h, lnx, head_mean, wpr, wpm, wout, ln)


def _ffn_kernel(h_ref, wup_ref, cw_ref, cb_ref, wdown_ref, ln_ref, o_ref, carry_ref, act_ref,
                pre_ref, *, tp, alpha):
    i = pl.program_id(0)
    tm, d = h_ref.shape
    dff = wdown_ref.shape[0]

    @pl.when(i == 0)
    def _():
        carry_ref[0:8, :] = jnp.zeros((8, carry_ref.shape[1]), _F32)
        pre_ref[...] = jnp.zeros_like(pre_ref)

    h = h_ref[...]
    hb = h.astype(_BF16)
    def conv(u, c0):
        cs = slice(c0, c0 + FF_CHUNK)
        carry_ref[8:, cs] = u
        u1 = carry_ref[7:7 + tm, cs]
        u2 = carry_ref[6:6 + tm, cs]
        out = (cw_ref[2:3, cs] * u + cw_ref[1:2, cs] * u1 + cw_ref[0:1, cs] * u2 + cb_ref[:, cs])
        carry_ref[0:8, cs] = u[tm - 8:, :]
        return out

    def up(c0):
        return (_dot(hb, wup_ref[:, c0:c0 + FF_CHUNK]),
                _dot(hb, wup_ref[:, dff + c0:dff + c0 + FF_CHUNK]))

    chunks = list(range(0, dff, FF_CHUNK))
    ahead = up(chunks[0])
    hn = _layer_norm(pre_ref[...], ln_ref[0:1, :], ln_ref[1:2, :])
    o_ref[...] = jnp.where(_valid_rows(i - 1, tm, tp), hn, 0.0).reshape(o_ref.shape)
    for n, c0 in enumerate(chunks):
        gate_raw, val_raw = ahead
        if n + 1 < len(chunks):
            ahead = up(chunks[n + 1])
        gate = conv(gate_raw, c0)
        val = conv(val_raw, dff + c0)
        act_ref[:, c0:c0 + FF_CHUNK] = (gate * _sigmoid(gate) * val).astype(_BF16)
    pre_ref[...] = alpha * h + _dot(act_ref[...], wdown_ref[...])


def _ffn(h, wup, cw, cb, wdown, ln, tp, alpha, bsz, final):
    r, d = h.shape
    tm = FFN_TM
    full = lambda a: pl.BlockSpec(a.shape, lambda i: (0,) * a.ndim, pipeline_mode=pl.Buffered(1))
    nt = r // tm
    done = lambda i: jnp.maximum(i - 1, 0)
    if final:
        per = tp // tm
        assert FRONT == tm
        out_shape = jax.ShapeDtypeStruct((bsz, tp - FRONT, d), _F32)
        out_spec = pl.BlockSpec(
            (1, tm, d), lambda i: (done(i) // per, jnp.maximum(done(i) % per - 1, 0), 0))
    else:
        out_shape = jax.ShapeDtypeStruct((r, d), _F32)
        out_spec = pl.BlockSpec((tm, d), lambda i: (done(i), 0))
    return pl.pallas_call(
        functools.partial(_ffn_kernel, tp=tp, alpha=alpha),
        out_shape=out_shape,
        grid=(nt + 1,),
        in_specs=[pl.BlockSpec((tm, d), lambda i: (jnp.minimum(i, nt - 1), 0)),
                  full(wup), full(cw), full(cb), full(wdown), full(ln)],
        out_specs=out_spec,
        scratch_shapes=[pltpu.VMEM((8 + tm, wup.shape[1]), _F32),
                        pltpu.VMEM((tm, wdown.shape[0]), _BF16),
                        pltpu.VMEM((tm, d), _F32)],
        compiler_params=_cparams("arbitrary"),
        name="conv_ffn",
    )(h, wup, cw, cb, wdown, ln)


def _pad_cols(a, n):
    return jnp.pad(a, ((0, 0), (0, n - a.shape[1])))


def _rearrange_in_cols(a):
    w = RWKV_WIDTH
    o_mla = 3 * w + DECAY_LORA + ICLR_LORA + GATE_LORA
    o_gate = o_mla + Q_LORA_RANK + KV_LORA_RANK + QK_ROPE_DIM
    rows = a.shape[0]
    z = lambda n: jnp.zeros((rows, n), a.dtype)
    kr = a[:, o_mla + Q_LORA_RANK + KV_LORA_RANK:o_gate]
    parts = [
        a[:, :3 * w + DECAY_LORA + ICLR_LORA],
        a[:, 3 * w + DECAY_LORA + ICLR_LORA:o_mla], z(GATE_LORA_PAD - GATE_LORA),
        a[:, o_mla:o_mla + Q_LORA_RANK + KV_LORA_RANK],
        z(QK_NOPE_DIM), kr, z(LANES - QK_NOPE_DIM - QK_ROPE_DIM),
        a[:, o_gate:],
    ]
    return jnp.concatenate(parts, axis=1)


def _pad_heads(wmat, per_head, lead):
    kdim = wmat.shape[0]
    wh = wmat.reshape(kdim, MLA_HEADS, per_head)
    wh = jnp.pad(wh, ((0, 0), (0, 0), (lead, QK_PAD - per_head - lead)))
    return wh.reshape(kdim, MLA_HEADS * QK_PAD)


def _rope_tables(tp, reps):
    half = QK_ROPE_DIM // 2
    pos = (jnp.arange(tp, dtype=jnp.int32) - PADF).astype(_F32)
    inv_freq = ROPE_THETA ** (-jnp.arange(half, dtype=_F32) / half)
    ang = pos[:, None] * inv_freq[None, :]
    cos, sin = jnp.cos(ang), jnp.sin(ang)
    zeros = lambda n: jnp.zeros((tp, n), _F32)
    tail = LANES - QK_NOPE_DIM - QK_ROPE_DIM
    cos_t = jnp.concatenate([jnp.ones((tp, QK_NOPE_DIM), _F32), cos, cos, zeros(tail)], axis=1)
    sin_a = jnp.concatenate([zeros(QK_NOPE_DIM), -sin, zeros(half + tail)], axis=1)
    sin_b = jnp.concatenate([zeros(QK_NOPE_DIM + half), sin, zeros(tail)], axis=1)
    tab = jnp.stack([cos_t, sin_a, sin_b])
    return jnp.tile(tab, (1, reps, 1))


def kernel(x, meta_tokens, ln_in_g, ln_in_b, w_in, mu_shift, w0, w_lora_up, a0, a_lora_up,
           g_lora_up, k_k, k_a, r_k, lnx_g, lnx_b, q_norm_g, w_uq, kv_norm_g, w_uk, w_uv,
           w_proj_rwkv, w_proj_mla, w_out, ln1_g, ln1_b, w_up, conv_w, conv_b, w_down,
           ln2_g, ln2_b):
    bsz, seq, d = x.shape
    depth = w_in.shape[0]
    tp = seq + FRONT
    r = bsz * tp
    assert tp % ATT_TILE == 0 and r % ROW_TM == 0 and r % PREP_TM == 0
    alpha = float((2 * depth) ** 0.25)
    w = RWKV_WIDTH

    meta_pad = jnp.pad(meta_tokens.astype(x.dtype), ((PADF, 0), (0, 0)))
    h = (x, meta_pad, ln_in_g[None], ln_in_b[None])

    hid = jnp.arange(MXU_DIM) // HEAD_DIM
    head_sum = (hid[:, None] == hid[None, :]).astype(_BF16)
    tt = jnp.arange(PREP_TM)
    tri = ((tt[:, None] >= tt[None, :])
           & (tt[:, None] // WKV_CHUNK == tt[None, :] // WKV_CHUNK)).astype(_BF16)

    def rope_for(tm):
        reps = 1
        while (reps * tp) % tm:
            reps += 1
        return _rope_tables(tp, reps)

    for l in range(depth):
        w_in_l = _rearrange_in_cols(w_in[l]).astype(_BF16)
        mu = _rearrange_in_cols(jnp.pad(mu_shift[l][None], ((0, 0), (0, w_in.shape[2] - mu_shift.shape[1]))))
        mu = mu[:, :RW_END]
        vec = jnp.stack([w0[l], a0[l], k_k[l], k_a[l]])
        wa_up = jnp.concatenate([
            jnp.concatenate([w_lora_up[l], jnp.zeros((DECAY_LORA, w), _F32)], axis=1),
            jnp.concatenate([jnp.zeros((ICLR_LORA, w), _F32), a_lora_up[l]], axis=1)], axis=0)
        g_up = jnp.pad(g_lora_up[l], ((0, GATE_LORA_PAD - GATE_LORA), (0, 0)))
        wuq = _pad_heads(w_uq[l], QK_NOPE_DIM + QK_ROPE_DIM, 0)
        wuk = _pad_heads(w_uk[l], QK_NOPE_DIM, 0)

        params = (mu, vec, r_k[l].reshape(1, w), wa_up.astype(_BF16), g_up.astype(_BF16),
                  head_sum, tri, q_norm_g[l][None], kv_norm_g[l][None],
                  wuq.astype(_BF16), wuk.astype(_BF16), w_uv[l].T.astype(_BF16))
        tm = PREP_TM if l == 0 else PREP_TM_LATER
        outs = _mixin(h, w_in_l, rope_for(tm * MIX_SUBTILES), params, tp, bsz, tm)
        if l == 0:
            h, outs = outs[0], outs[1:]
        pgate, kt, rt, kd, bd, v, pc, bonus, g, q, kk, vv = outs
        y = _wkv(kt, rt, kd, bd, v, pc, bsz)
        ymla = _attention(q, kk, vv, bsz)
        h = _merge(y, bonus, g, ymla, pgate, h, jnp.stack([lnx_g[l], lnx_b[l]]), head_sum,
                   w_proj_rwkv[l].astype(_BF16), w_proj_mla[l].astype(_BF16),
                   w_out[l].astype(_BF16), jnp.stack([ln1_g[l], ln1_b[l]]), tp, alpha)
        h = _ffn(h, w_up[l].astype(_BF16), conv_w[l], conv_b[l][None], w_down[l].astype(_BF16),
                 jnp.stack([ln2_g[l], ln2_b[l]]), tp, alpha, bsz, final=(l == depth - 1))
    return h
```

```python
import functools

import numpy as np
import jax
import jax.numpy as jnp
from jax import lax
from jax.experimental import pallas as pl
from jax.experimental.pallas import tpu as pltpu

N_META = 16
HEAD_DIM = 64
RWKV_HEADS = 8
RWKV_WIDTH = RWKV_HEADS * HEAD_DIM
DECAY_LORA = 64
ICLR_LORA = 64
GATE_LORA = 160
GN_EPS = 64e-5
MLA_HEADS = 8
QK_NOPE_DIM = 64
QK_ROPE_DIM = 32
V_HEAD_DIM = 64
Q_LORA_RANK = 256
KV_LORA_RANK = 256
ROPE_THETA = 10000.0
MLA_WIDTH = MLA_HEADS * V_HEAD_DIM
CONV_WIDTH = 3
LN_EPS = 1e-5
RMS_EPS = 1e-6

LANES = 128
MXU_DIM = 256
VMEM_LIMIT = 56 * 1024 * 1024

FRONT = MXU_DIM
PADF = FRONT - N_META
C_R, C_K, C_V = 0, RWKV_WIDTH, 2 * RWKV_WIDTH
C_WA = 3 * RWKV_WIDTH
GATE_LORA_PAD = 2 * LANES
C_G = C_WA + LANES
RW_END = C_G + GATE_LORA_PAD
C_CQ = RW_END
C_CKV = C_CQ + Q_LORA_RANK
C_KR = C_CKV + KV_LORA_RANK
C_GATE = C_KR + LANES
QK_PAD = LANES

WKV_CHUNK = 64
WKV_BATCH = 4
ATT_TILE = MXU_DIM
ATT_UNROLL = 4
ATT_PAIRS = 4
PREP_TM = 256
PREP_TM_LATER = 256
MIX_SUBTILES = 2
ROW_TM = 1024
FFN_TM = 256
MERGE_SUB = 256
FF_CHUNK = 256
NEG_BIG = -1e30
BF16_ROWS = 16

_F32 = jnp.float32
_BF16 = jnp.bfloat16


def _cparams(*sem):
    return pltpu.CompilerParams(dimension_semantics=sem, vmem_limit_bytes=VMEM_LIMIT)


def _dot(a, b):
    return jnp.dot(a, b, preferred_element_type=_F32)


def _dot_nt(a, b):
    return lax.dot_general(a, b, (((1,), (1,)), ((), ())), preferred_element_type=_F32)


def _dot_tn(a, b):
    return lax.dot_general(a, b, (((0,), (0,)), ((), ())), preferred_element_type=_F32)


def _split_dot(x, w_bf16, parts, left=False):
    acc = None
    rem = x
    for _ in range(parts):
        hi = rem.astype(_BF16)
        term = _dot(w_bf16, hi) if left else _dot(hi, w_bf16)
        acc = term if acc is None else acc + term
        rem = rem - hi.astype(_F32)
    return acc


def _head_sum(x, ones_bd):
    n = ones_bd.shape[0]
    xb = x.astype(_BF16)
    return jnp.concatenate([_dot(xb[:, c0:c0 + n], ones_bd) for c0 in range(0, x.shape[1], n)],
                           axis=1)


def _layer_norm(x, g, b):
    mu = jnp.mean(x, axis=-1, keepdims=True)
    xc = x - mu
    var = jnp.mean(xc * xc, axis=-1, keepdims=True)
    return xc * lax.rsqrt(var + LN_EPS) * g + b


def _sigmoid(x):
    return 1.0 / (1.0 + jnp.exp(-x))


def _valid_rows(tile_idx, tm, tp):
    row = tile_idx * tm + lax.broadcasted_iota(jnp.int32, (tm, 1), 0)
    bidx = jnp.floor((row.astype(_F32) + 0.5) * (1.0 / tp)).astype(jnp.int32)
    return (row - bidx * tp) >= PADF


def _mixin_multi_kernel(*refs, embed, tp, nsub):
    i = pl.program_id(0)
    n_in = (nsub + 3 if embed else 1) + 14
    ins, outs, carry_ref = refs[:n_in], refs[n_in:-1], refs[-1]

    @pl.when(i == 0)
    def _():
        carry_ref[...] = jnp.zeros_like(carry_ref)

    for sub in range(nsub):
        t_rows = outs[-2].shape[0] // nsub
        rs = pl.ds(sub * t_rows, t_rows)
        if embed:
            lead = (ins[sub],) + ins[nsub:nsub + 3]
            rest = ins[nsub + 3:]
        else:
            lead = (ins[0].at[rs],)
            rest = ins[1:]
        rest = (rest[0], rest[1].at[:, rs]) + rest[2:]
        n_pc = outs[-6].shape[0] // nsub
        row_outs = [o.at[pl.ds(sub * n_pc, n_pc)] if o is outs[-6] else o.at[rs]
                    for o in outs[:-1]]
        vv_view = outs[-1].at[:, rs]
        _mixin_kernel(*lead, *rest, *row_outs, vv_view, carry_ref,
                      embed=embed, tp=tp, tile=i * nsub + sub)


def _mixin_kernel(*refs, embed, tp, tile):
    if embed:
        x_ref, meta_ref, lng_ref, lnb_ref = refs[:4]
        refs = refs[4:]
    else:
        h_ref = refs[0]
        refs = refs[1:]
    (w_ref, rope_ref, mu_ref, vec_ref, rk_ref, wa_up_ref, g_up_ref, head_sum_ref, tri_ref,
     qg_ref, kvg_ref, wuq_ref, wuk_ref, wuv_ref) = refs[:14]
    refs = refs[14:]
    if embed:
        h_out_ref = refs[0]
        refs = refs[1:]
    (gate_ref, kt_ref, rt_ref, kd_ref, bd_ref, v_ref, pc_ref, bonus_ref, g_ref,
     q_ref, kk_ref, vv_ref, carry_ref) = refs
    i = tile
    tm = kt_ref.shape[0]
    w = RWKV_WIDTH

    if embed:
        src = jnp.where(i % (tp // tm) == 0, meta_ref[...], x_ref[0])
        h = _layer_norm(src, lng_ref[...], lnb_ref[...])
        h = jnp.where(_valid_rows(i, tm, tp), h, 0.0)
        h_out_ref[...] = h
    else:
        h = h_ref[...]
    hb = h.astype(_BF16)

    def gate_cols(c0, c1):
        gate_ref[:, c0:c1] = _dot(hb, w_ref[:, C_GATE + c0:C_GATE + c1]).astype(gate_ref.dtype)

    ngate = gate_ref.shape[1]
    gq = ngate // 4

    pr = _dot(hb, w_ref[:, :RW_END])
    gate_cols(0, gq)
    shifted = pltpu.roll(pr, 1, axis=0)
    row = lax.broadcasted_iota(jnp.int32, (tm, 1), 0)
    shifted = jnp.where(row == 0, carry_ref[7:8, :], shifted)
    carry_ref[...] = pr[tm - 8:, :]
    ps = pr + (shifted - pr) * mu_ref[...]

    r = ps[:, C_R:C_R + w]
    k = ps[:, C_K:C_K + w]
    v = ps[:, C_V:C_V + w]
    wa_in = ps[:, C_WA:C_WA + LANES]
    lane = lax.broadcasted_iota(jnp.int32, (1, LANES), 1)
    wa_in = jnp.where(lane < DECAY_LORA, jnp.tanh(wa_in), wa_in)
    wa = _dot(wa_in.astype(_BF16), wa_up_ref[...])
    w0, a0, k_k, k_a = (vec_ref[0:1, :], vec_ref[1:2, :], vec_ref[2:3, :], vec_ref[3:4, :])

    z = -(w0 + wa[:, :w])
    softplus = jnp.maximum(z, 0.0) + jnp.log1p(jnp.exp(-jnp.abs(z)))
    logw = -jnp.exp(-softplus - 0.5)
    a = _sigmoid(a0 + wa[:, w:])
    gate = _dot(_sigmoid(ps[:, C_G:C_G + GATE_LORA_PAD]).astype(_BF16), g_up_ref[...])
    gate_cols(gq, 2 * gq)

    head_sum = head_sum_ref[...]
    kk = k * k_k
    kk_ss = _head_sum(kk * kk, head_sum)
    kk = kk / jnp.maximum(jnp.sqrt(kk_ss), 1e-12)
    k = k * (1.0 + (a - 1.0) * k_a)
    b = kk * a
    bonus = _head_sum(r * k * rk_ref[...], head_sum) * v
    gate_cols(2 * gq, 3 * gq)

    tri = tri_ref[...]
    tn = tri.shape[0]
    cum = jnp.concatenate([_split_dot(logw[r0:r0 + tn], tri, 3, left=True)
                           for r0 in range(0, tm, tn)], axis=0)
    pm = _dot(hb, w_ref[:, C_CQ:C_GATE])
    e_neg = jnp.exp(-cum)
    kt_ref[...] = (kk * jnp.exp(cum - logw)).astype(_BF16)
    rt_ref[...] = (r * jnp.exp(cum)).astype(_BF16)
    kd_ref[...] = (k * e_neg).astype(_BF16)
    bd_ref[...] = (b * e_neg).astype(_BF16)
    v_ref[...] = v.astype(_BF16)
    bonus_ref[...] = bonus.astype(_BF16)
    g_ref[...] = gate.astype(_BF16)
    for ci in range(tm // WKV_CHUNK):
        last = (ci + 1) * WKV_CHUNK - 1
        pc_ref[ci] = jnp.exp(cum[last:last + 1, :])

    cos_t, sin_a, sin_b = rope_ref[0], rope_ref[1], rope_ref[2]

    def rope(x):
        nrep = x.shape[1] // LANES
        n = x.shape[1]
        c, sa, sb = (jnp.tile(t, (1, nrep)) if nrep > 1 else t for t in (cos_t, sin_a, sin_b))
        half = QK_ROPE_DIM // 2
        return x * c + pltpu.roll(x, n - half, axis=1) * sa + pltpu.roll(x, half, axis=1) * sb

    def rms(x, gain):
        return x * lax.rsqrt(jnp.mean(x * x, axis=-1, keepdims=True) + RMS_EPS) * gain

    cq = pm[:, :Q_LORA_RANK]
    ckv = pm[:, Q_LORA_RANK:Q_LORA_RANK + KV_LORA_RANK]
    kr = pm[:, C_KR - C_CQ:]
    qn = rms(cq, qg_ref[...]).astype(_BF16)
    scale = float(np.log2(np.e) / np.sqrt(QK_NOPE_DIM + QK_ROPE_DIM))
    q_ref[...] = (rope(_dot(qn, wuq_ref[...])) * scale).astype(_BF16)
    kvn = rms(ckv, kvg_ref[...]).astype(_BF16)
    gate_cols(3 * gq, ngate)
    kk_ref[...] = (_dot(kvn, wuk_ref[...]) + jnp.tile(rope(kr), (1, MLA_HEADS))).astype(_BF16)
    vv_ref[...] = _dot_nt(wuv_ref[...], kvn).astype(_BF16)


def _mixin(src, w_in, rope_tab, params, tp, bsz, tm):
    embed = isinstance(src, tuple)
    nsub = MIX_SUBTILES
    sub_rows = tm
    tm = sub_rows * nsub
    d = w_in.shape[0]
    r = bsz * tp
    nt = r // tm
    per = tp // sub_rows
    nrope = rope_tab.shape[1] // tm
    w = RWKV_WIDTH
    nchunk = tm // WKV_CHUNK
    ngate = w_in.shape[1] - C_GATE
    full = lambda a: pl.BlockSpec(a.shape, lambda i: (0,) * a.ndim)
    row_out = lambda n: pl.BlockSpec((tm, n), lambda i: (i, 0))
    if embed:
        assert FRONT == sub_rows
        x, meta_pad, ln_g, ln_b = src
        lead_args = [x] * nsub + [meta_pad, ln_g, ln_b]

        def x_spec(sub):
            tile = lambda i: i * nsub + sub
            return pl.BlockSpec(
                (1, sub_rows, d), lambda i: (tile(i) // per, jnp.maximum(tile(i) % per - 1, 0), 0))

        lead_specs = [x_spec(sub) for sub in range(nsub)] + [full(meta_pad), full(ln_g), full(ln_b)]
        lead_out, lead_out_specs = [jax.ShapeDtypeStruct((r, d), _F32)], [row_out(d)]
    else:
        lead_args, lead_specs = [src], [row_out(d)]
        lead_out, lead_out_specs = [], []
    out_shape = (
        lead_out
        + [jax.ShapeDtypeStruct((r, ngate), _BF16)]
        + [jax.ShapeDtypeStruct((r, w), _BF16)] * 5
        + [jax.ShapeDtypeStruct((nt * nchunk, 1, w), _F32)]
        + [jax.ShapeDtypeStruct((r, w), _BF16)] * 2
        + [jax.ShapeDtypeStruct((r, MLA_HEADS * QK_PAD), _BF16)] * 2
        + [jax.ShapeDtypeStruct((MLA_WIDTH, r), _BF16)]
    )
    out_specs = (
        lead_out_specs
        + [row_out(ngate)]
        + [row_out(w)] * 5
        + [pl.BlockSpec((nchunk, 1, w), lambda i: (i, 0, 0))]
        + [row_out(w)] * 2
        + [row_out(MLA_HEADS * QK_PAD)] * 2
        + [pl.BlockSpec((MLA_WIDTH, tm), lambda i: (0, i))]
    )
    return pl.pallas_call(
        functools.partial(_mixin_multi_kernel, embed=embed, tp=tp, nsub=nsub),
        out_shape=out_shape,
        grid=(nt,),
        in_specs=lead_specs + [full(w_in), pl.BlockSpec((3, tm, LANES), lambda i: (0, i % nrope, 0))]
        + [full(a) for a in params],
        out_specs=out_specs,
        scratch_shapes=[pltpu.VMEM((8, RW_END), _F32)],
        compiler_params=_cparams("arbitrary"),
        name="mix_in",
    )(*lead_args, w_in, rope_tab, *params)


def _wkv_kernel(kt_ref, rt_ref, kd_ref, bd_ref, v_ref, pc_ref, y_ref, h_scr):
    c = pl.program_id(1)

    @pl.when(c < PADF // WKV_CHUNK)
    def _():
        h_scr[...] = jnp.zeros_like(h_scr)
        y_ref[...] = jnp.zeros_like(y_ref)

    @pl.when(c >= PADF // WKV_CHUNK)
    def _():
        _wkv_chunk(kt_ref, rt_ref, kd_ref, bd_ref, v_ref, pc_ref, y_ref, h_scr)


def _wkv_chunk(kt_ref, rt_ref, kd_ref, bd_ref, v_ref, pc_ref, y_ref, h_scr):
    npair = RWKV_HEADS // 2
    cl = WKV_CHUNK
    n2 = 2 * cl
    lane = lax.broadcasted_iota(jnp.int32, (1, LANES), 1)
    first = lane < HEAD_DIM
    row = lax.broadcasted_iota(jnp.int32, (n2, n2), 0)
    col = lax.broadcasted_iota(jnp.int32, (n2, n2), 1)
    strict = row > col
    incl = row >= col
    eye = (row == col).astype(_F32)

    def stack(x):
        zero = jnp.zeros_like(x)
        return jnp.concatenate([jnp.where(first, x, zero), jnp.where(first, zero, x)], axis=0)

    def same_block(size):
        sh = size.bit_length() - 1
        return jnp.right_shift(row, sh) == jnp.right_shift(col, sh)

    units = [(bl, j) for bl in range(kt_ref.shape[0]) for j in range(npair)]
    each = lambda fn, *lists: [fn(*args) for args in zip(*lists)]
    bf = lambda x: x.astype(_BF16)

    def load(ref):
        return [stack(ref[bl, :, j * LANES:(j + 1) * LANES]) for bl, j in units]

    kt2, rt2, kd2, bd2, v2 = load(kt_ref), load(rt_ref), load(kd_ref), load(bd_ref), load(v_ref)
    lhs = each(lambda a, b: jnp.concatenate([a, b], axis=0), kt2, rt2)
    rhs = each(lambda a, b: jnp.concatenate([a, b], axis=0), kd2, bd2)
    gram = each(_dot_nt, lhs, rhs)
    a_kk = [bf(jnp.where(strict, g[:n2, :n2], 0.0)) for g in gram]
    a_kb = [jnp.where(strict, g[:n2, n2:], 0.0) for g in gram]
    a_out = [bf(jnp.concatenate([jnp.where(incl, g[n2:, :n2], 0.0),
                                 jnp.where(incl, -g[n2:, n2:], 0.0)], axis=1)) for g in gram]

    blk = same_block(8)
    nd = [bf(jnp.where(blk, a, 0.0)) for a in a_kb]
    s2 = [bf(x) for x in each(_dot, nd, nd)]
    i_minus = [eye - x.astype(_F32) for x in nd]
    p1 = each(lambda im, s: im + _dot(bf(im), s), i_minus, s2)
    s4 = [bf(x) for x in each(_dot, s2, s2)]
    t_inv = each(lambda p, s: p + _dot(bf(p), s), p1, s4)
    size = 8
    while size < cl:
        outer = same_block(2 * size)
        sel = outer & jnp.logical_not(blk)
        off = [bf(jnp.where(sel, a, 0.0)) for a in a_kb]
        tb = [bf(t) for t in t_inv]
        m = [bf(x) for x in each(_dot, tb, off)]
        t_inv = each(lambda t, mm, tbb: t - _dot(mm, tbb), t_inv, m, tb)
        blk = outer
        size *= 2
    tb = [bf(t) for t in t_inv]
    akv = each(_dot, a_kk, v2)

    hbd = [h_scr[i] for i in range(len(units))]
    x0 = each(lambda l, h: _dot(l, bf(h)), lhs, hbd)
    u2 = each(lambda t, x, w: bf(_dot(t, bf(x[:n2] + w))), tb, x0, akv)
    vu = each(lambda a, b: jnp.concatenate([a, b], axis=0), v2, u2)
    y2 = each(lambda x, a, z: x[n2:] + _dot(a, z), x0, a_out, vu)
    upd = each(lambda kd, bd, z: _dot_tn(jnp.concatenate([kd, -bd], axis=0), z), kd2, bd2, vu)
    for i, (bl, j) in enumerate(units):
        sl = slice(j * LANES, (j + 1) * LANES)
        y_ref[bl, :, sl] = (y2[i][:cl] + y2[i][cl:]).astype(y_ref.dtype)
        pc_row = pc_ref[bl, 0, :, sl]
        pc_col = jnp.transpose(jnp.broadcast_to(pc_row, (LANES, LANES)))
        h_scr[i] = (hbd[i] + upd[i]) * pc_col


def _wkv(kt, rt, kd, bd, v, pc, bsz):
    r, w = kt.shape
    tp = r // bsz
    nc = tp // WKV_CHUNK
    nb = WKV_BATCH
    assert bsz % nb == 0
    blk = pl.BlockSpec((nb, WKV_CHUNK, w), lambda bg, c: (bg, c, 0))
    args = [a.reshape(bsz, tp, w) for a in (kt, rt, kd, bd, v)] + [pc.reshape(bsz, nc, 1, w)]
    y = pl.pallas_call(
        _wkv_kernel,
        out_shape=jax.ShapeDtypeStruct((bsz, tp, w), _F32),
        grid=(bsz // nb, nc),
        in_specs=[blk] * 5 + [pl.BlockSpec((nb, 1, 1, w), lambda bg, c: (bg, c, 0, 0))],
        out_specs=blk,
        scratch_shapes=[pltpu.VMEM((nb * (RWKV_HEADS // 2), LANES, LANES), _F32)],
        compiler_params=_cparams("parallel", "arbitrary"),
        name="wkv7",
    )(*args)
    return y.reshape(r, w)


def _attn_kernel(q_ref, k_ref, v_ref, o_ref, s_scr, m_scr, acc_scr):
    i = pl.program_id(2)
    tq = q_ref.shape[0]
    tk = tq
    nhead = q_ref.shape[1] // QK_PAD
    q_pos = i * tq + lax.broadcasted_iota(jnp.int32, (1, tq), 1)
    head_cols = lambda n: slice(n * QK_PAD, (n + 1) * QK_PAD)
    qh = [q_ref[:, head_cols(n)] for n in range(nhead)]

    def scores(jb, n):
        start = pl.multiple_of(jb * tk, tk)
        s_scr[n] = _dot_nt(k_ref[pl.ds(start, tk), head_cols(n)], qh[n])

    def softmax(n, s, first=False):
        m_new = jnp.max(s, axis=0, keepdims=True)
        rescale = None
        if not first:
            m_prev = m_scr[n]
            m_new = jnp.maximum(m_prev, m_new)
            rescale = jnp.exp2(m_prev - m_new)
        m_scr[n] = m_new
        return jnp.exp2(s - m_new).astype(_BF16), rescale

    def values(n, start, prob, rescale):
        if prob.shape[0] < tk:
            prob = jnp.concatenate([jnp.zeros((tk - prob.shape[0], tq), _BF16), prob], axis=0)
        lo = (n // 2) * LANES + (n % 2) * V_HEAD_DIM
        vt_h = jnp.concatenate([v_ref[lo:lo + V_HEAD_DIM, pl.ds(start, tk)],
                                jnp.ones((BF16_ROWS, tk), _BF16)], axis=0)
        pv = _dot(vt_h, prob)
        acc_scr[n] = pv if rescale is None else rescale * acc_scr[n] + pv

    ahead = 6
    behind = 1

    def block(jb, masked, last):
        start = pl.multiple_of(jb * tk, tk)
        if masked:
            mask = (start + lax.broadcasted_iota(jnp.int32, (tk, 1), 0)) <= q_pos
        pending = []
        for n in range(nhead):
            if n + ahead < nhead:
                scores(jb, n + ahead)
            elif not last:
                scores(jb + 1, n + ahead - nhead)
            s = s_scr[n]
            if masked:
                s = jnp.where(mask, s, NEG_BIG)
            pending.append((n,) + softmax(n, s))
            if len(pending) > behind:
                m, prob, rescale = pending.pop(0)
                values(m, start, prob, rescale)
        for m, prob, rescale in pending:
            values(m, start, prob, rescale)

    def front(s_front):
        mask = (PADF + lax.broadcasted_iota(jnp.int32, (N_META, 1), 0)) <= q_pos
        for n in range(nhead):
            prob, _ = softmax(n, jnp.where(mask, s_front[n], NEG_BIG), first=True)
            values(n, 0, prob, None)

    s_front = [_dot_nt(k_ref[PADF:FRONT, head_cols(n)], qh[n]) for n in range(nhead)]
    for n in range(ahead):
        scores(1, n)
    front(s_front)

    def several_blocks(pidx, carry):
        for u in range(ATT_UNROLL):
            block(ATT_UNROLL * pidx + 1 + u, False, False)
        return carry

    def one_block(jb, carry):
        block(jb, False, False)
        return carry

    def diagonal(_, carry):
        block(i, True, True)
        return carry

    n_plain = jnp.maximum(i - 1, 0)
    n_rolled = n_plain // ATT_UNROLL
    lax.fori_loop(0, n_rolled, several_blocks, 0)
    lax.fori_loop(ATT_UNROLL * n_rolled + 1, n_plain + 1, one_block, 0)
    lax.fori_loop(0, jnp.minimum(i, 1), diagonal, 0)
    for g in range(nhead // 2):
        halves = [acc_scr[n, :V_HEAD_DIM] / acc_scr[n, V_HEAD_DIM:V_HEAD_DIM + 1]
                  for n in (2 * g, 2 * g + 1)]
        o = jnp.concatenate(halves, axis=0)
        o_ref[:, g * LANES:(g + 1) * LANES] = jnp.transpose(o).astype(o_ref.dtype)


def _attention(q, k, v, bsz):
    r = q.shape[0]
    tp = r // bsz
    nq = tp // ATT_TILE
    ng = MLA_HEADS // 2 // ATT_PAIRS
    qw, vw = ATT_PAIRS * 2 * QK_PAD, ATT_PAIRS * 2 * V_HEAD_DIM
    return pl.pallas_call(
        _attn_kernel,
        out_shape=jax.ShapeDtypeStruct((r, MLA_WIDTH), _BF16),
        grid=(bsz, ng, nq),
        in_specs=[
            pl.BlockSpec((ATT_TILE, qw), lambda bi, pj, i: (bi * nq + i, pj)),
            pl.BlockSpec((tp, qw), lambda bi, pj, i: (bi, pj)),
            pl.BlockSpec((vw, tp), lambda bi, pj, i: (pj, bi)),
        ],
        out_specs=pl.BlockSpec((ATT_TILE, vw), lambda bi, pj, i: (bi * nq + i, pj)),
        scratch_shapes=[pltpu.VMEM((2 * ATT_PAIRS, ATT_TILE, ATT_TILE), _F32)]
        + [pltpu.VMEM((2 * ATT_PAIRS, 1, ATT_TILE), _F32),
           pltpu.VMEM((2 * ATT_PAIRS, V_HEAD_DIM + BF16_ROWS, ATT_TILE), _F32)],
        compiler_params=_cparams("parallel", "parallel", "arbitrary"),
        name="mla_attention",
    )(q, k, v)


def _merge_kernel(y_ref, bonus_ref, g_ref, ymla_ref, gate_ref, h_ref, lnx_ref, head_sum_ref,
                  wpr_ref, wpm_ref, wout_ref, ln_ref, o_ref, *, tp, alpha):
    tm = y_ref.shape[0]
    d = h_ref.shape[1]
    head_sum = head_sum_ref[...]
    valid = _valid_rows(pl.program_id(0), tm, tp)
    subs = [slice(r0, r0 + MERGE_SUB) for r0 in range(0, tm, MERGE_SUB)]
    ys = [y_ref[s, :].astype(_F32) for s in subs]
    yms = [_head_sum(y, head_sum) * (1.0 / HEAD_DIM) for y in ys]
    ycs = [y - ym for y, ym in zip(ys, yms)]
    yvs = [_head_sum(yc * yc, head_sum) * (1.0 / HEAD_DIM) for yc in ycs]
    y_rwkv = [((yc * lax.rsqrt(yv + GN_EPS) * lnx_ref[0:1, :] + lnx_ref[1:2, :]
                + bonus_ref[s, :].astype(_F32)) * g_ref[s, :].astype(_F32)).astype(_BF16)
              for s, yc, yv in zip(subs, ycs, yvs)]
    pr = [_dot(yr, wpr_ref[...]) for yr in y_rwkv]
    pm = [_dot(ymla_ref[s, :], wpm_ref[...]) for s in subs]
    mixed = []
    for s, a, b in zip(subs, pr, pm):
        gates = _sigmoid(gate_ref[s, :].astype(_F32))
        mixed.append((gates[:, :d] * a + gates[:, d:] * b).astype(_BF16))
    outs = [_dot(m, wout_ref[...]) for m in mixed]
    for s, out in zip(subs, outs):
        hn = _layer_norm(alpha * h_ref[s, :] + out, ln_ref[0:1, :], ln_ref[1:2, :])
        o_ref[s, :] = jnp.where(valid[s], hn, 0.0)


def _merge(y, bonus, g, ymla, pgate, h, lnx, head_mean, wpr, wpm, wout, ln, tp, alpha):
    r, d = h.shape
    tm = ROW_TM
    w = RWKV_WIDTH
    full = lambda a: pl.BlockSpec(a.shape, lambda i: (0,) * a.ndim)
    rows = lambda n: pl.BlockSpec((tm, n), lambda i: (i, 0))
    return pl.pallas_call(
        functools.partial(_merge_kernel, tp=tp, alpha=alpha),
        out_shape=jax.ShapeDtypeStruct((r, d), _F32),
        grid=(r // tm,),
        in_specs=[rows(w), rows(w), rows(w), rows(MLA_WIDTH),
                  rows(2 * d),
                  rows(d), full(lnx), full(head_mean), full(wpr), full(wpm), full(wout), full(ln)],
        out_specs=rows(d),
        compiler_params=_cparams("parallel"),
        name="merge",
    )(y, bonus, g, ymla, pgate, h, lnx, head_mean, wpr, wpm, wout, ln)


def _ffn_kernel(h_ref, wup_ref, cw_ref, cb_ref, wdown_ref, ln_ref, o_ref, carry_ref, act_ref,
                pre_ref, *, tp, alpha):
    i = pl.program_id(0)
    tm, d = h_ref.shape
    dff = wdown_ref.shape[0]

    @pl.when(i == 0)
    def _():
        carry_ref[0:8, :] = jnp.zeros((8, carry_ref.shape[1]), _F32)
        pre_ref[...] = jnp.zeros_like(pre_ref)

    h = h_ref[...]
    hb = h.astype(_BF16)
    def conv(u, c0):
        cs = slice(c0, c0 + FF_CHUNK)
        carry_ref[8:, cs] = u
        u1 = carry_ref[7:7 + tm, cs]
        u2 = carry_ref[6:6 + tm, cs]
        out = (cw_ref[2:3, cs] * u + cw_ref[1:2, cs] * u1 + cw_ref[0:1, cs] * u2 + cb_ref[:, cs])
        carry_ref[0:8, cs] = u[tm - 8:, :]
        return out

    def up(c0):
        return (_dot(hb, wup_ref[:, c0:c0 + FF_CHUNK]),
                _dot(hb, wup_ref[:, dff + c0:dff + c0 + FF_CHUNK]))

    chunks = list(range(0, dff, FF_CHUNK))
    ahead = up(chunks[0])
    hn = _layer_norm(pre_ref[...], ln_ref[0:1, :], ln_ref[1:2, :])
    o_ref[...] = jnp.where(_valid_rows(i - 1, tm, tp), hn, 0.0).reshape(o_ref.shape)
    for n, c0 in enumerate(chunks):
        gate_raw, val_raw = ahead
        if n + 1 < len(chunks):
            ahead = up(chunks[n + 1])
        gate = conv(gate_raw, c0)
        val = conv(val_raw, dff + c0)
        act_ref[:, c0:c0 + FF_CHUNK] = (gate * _sigmoid(gate) * val).astype(_BF16)
    pre_ref[...] = alpha * h + _dot(act_ref[...], wdown_ref[...])


def _ffn(h, wup, cw, cb, wdown, ln, tp, alpha, bsz, final):
    r, d = h.shape
    tm = FFN_TM
    full = lambda a: pl.BlockSpec(a.shape, lambda i: (0,) * a.ndim, pipeline_mode=pl.Buffered(1))
    nt = r // tm
    done = lambda i: jnp.maximum(i - 1, 0)
    if final:
        per = tp // tm
        assert FRONT == tm
        out_shape = jax.ShapeDtypeStruct((bsz, tp - FRONT, d), _F32)
        out_spec = pl.BlockSpec(
            (1, tm, d), lambda i: (done(i) // per, jnp.maximum(done(i) % per - 1, 0), 0))
    else:
        out_shape = jax.ShapeDtypeStruct((r, d), _F32)
        out_spec = pl.BlockSpec((tm, d), lambda i: (done(i), 0))
    return pl.pallas_call(
        functools.partial(_ffn_kernel, tp=tp, alpha=alpha),
        out_shape=out_shape,
        grid=(nt + 1,),
        in_specs=[pl.BlockSpec((tm, d), lambda i: (jnp.minimum(i, nt - 1), 0)),
                  full(wup), full(cw), full(cb), full(wdown), full(ln)],
        out_specs=out_spec,
        scratch_shapes=[pltpu.VMEM((8 + tm, wup.shape[1]), _F32),
                        pltpu.VMEM((tm, wdown.shape[0]), _BF16),
                        pltpu.VMEM((tm, d), _F32)],
        compiler_params=_cparams("arbitrary"),
        name="conv_ffn",
    )(h, wup, cw, cb, wdown, ln)


def _pad_cols(a, n):
    return jnp.pad(a, ((0, 0), (0, n - a.shape[1])))


def _rearrange_in_cols(a):
    w = RWKV_WIDTH
    o_mla = 3 * w + DECAY_LORA + ICLR_LORA + GATE_LORA
    o_gate = o_mla + Q_LORA_RANK + KV_LORA_RANK + QK_ROPE_DIM
    rows = a.shape[0]
    z = lambda n: jnp.zeros((rows, n), a.dtype)
    kr = a[:, o_mla + Q_LORA_RANK + KV_LORA_RANK:o_gate]
    parts = [
        a[:, :3 * w + DECAY_LORA + ICLR_LORA],
        a[:, 3 * w + DECAY_LORA + ICLR_LORA:o_mla], z(GATE_LORA_PAD - GATE_LORA),
        a[:, o_mla:o_mla + Q_LORA_RANK + KV_LORA_RANK],
        z(QK_NOPE_DIM), kr, z(LANES - QK_NOPE_DIM - QK_ROPE_DIM),
        a[:, o_gate:],
    ]
    return jnp.concatenate(parts, axis=1)


def _pad_heads(wmat, per_head, lead):
    kdim = wmat.shape[0]
    wh = wmat.reshape(kdim, MLA_HEADS, per_head)
    wh = jnp.pad(wh, ((0, 0), (0, 0), (lead, QK_PAD - per_head - lead)))
    return wh.reshape(kdim, MLA_HEADS * QK_PAD)


def _rope_tables(tp, reps):
    half = QK_ROPE_DIM // 2
    pos = (jnp.arange(tp, dtype=jnp.int32) - PADF).astype(_F32)
    inv_freq = ROPE_THETA ** (-jnp.arange(half, dtype=_F32) / half)
    ang = pos[:, None] * inv_freq[None, :]
    cos, sin = jnp.cos(ang), jnp.sin(ang)
    zeros = lambda n: jnp.zeros((tp, n), _F32)
    tail = LANES - QK_NOPE_DIM - QK_ROPE_DIM
    cos_t = jnp.concatenate([jnp.ones((tp, QK_NOPE_DIM), _F32), cos, cos, zeros(tail)], axis=1)
    sin_a = jnp.concatenate([zeros(QK_NOPE_DIM), -sin, zeros(half + tail)], axis=1)
    sin_b = jnp.concatenate([zeros(QK_NOPE_DIM + half), sin, zeros(tail)], axis=1)
    tab = jnp.stack([cos_t, sin_a, sin_b])
    return jnp.tile(tab, (1, reps, 1))


def kernel(x, meta_tokens, ln_in_g, ln_in_b, w_in, mu_shift, w0, w_lora_up, a0, a_lora_up,
           g_lora_up, k_k, k_a, r_k, lnx_g, lnx_b, q_norm_g, w_uq, kv_norm_g, w_uk, w_uv,
           w_proj_rwkv, w_proj_mla, w_out, ln1_g, ln1_b, w_up, conv_w, conv_b, w_down,
           ln2_g, ln2_b):
    bsz, seq, d = x.shape
    depth = w_in.shape[0]
    tp = seq + FRONT
    r = bsz * tp
    assert tp % ATT_TILE == 0 and r % ROW_TM == 0 and r % PREP_TM == 0
    alpha = float((2 * depth) ** 0.25)
    w = RWKV_WIDTH

    meta_pad = jnp.pad(meta_tokens.astype(x.dtype), ((PADF, 0), (0, 0)))
    h = (x, meta_pad, ln_in_g[None], ln_in_b[None])

    hid = jnp.arange(MXU_DIM) // HEAD_DIM
    head_sum = (hid[:, None] == hid[None, :]).astype(_BF16)
    tt = jnp.arange(PREP_TM)
    tri = ((tt[:, None] >= tt[None, :])
           & (tt[:, None] // WKV_CHUNK == tt[None, :] // WKV_CHUNK)).astype(_BF16)

    def rope_for(tm):
        reps = 1
        while (reps * tp) % tm:
            reps += 1
        return _rope_tables(tp, reps)

    for l in range(depth):
        w_in_l = _rearrange_in_cols(w_in[l]).astype(_BF16)
        mu = _rearrange_in_cols(jnp.pad(mu_shift[l][None], ((0, 0), (0, w_in.shape[2] - mu_shift.shape[1]))))
        mu = mu[:, :RW_END]
        vec = jnp.stack([w0[l], a0[l], k_k[l], k_a[l]])
        wa_up = jnp.concatenate([
            jnp.concatenate([w_lora_up[l], jnp.zeros((DECAY_LORA, w), _F32)], axis=1),
            jnp.concatenate([jnp.zeros((ICLR_LORA, w), _F32), a_lora_up[l]], axis=1)], axis=0)
        g_up = jnp.pad(g_lora_up[l], ((0, GATE_LORA_PAD - GATE_LORA), (0, 0)))
        wuq = _pad_heads(w_uq[l], QK_NOPE_DIM + QK_ROPE_DIM, 0)
        wuk = _pad_heads(w_uk[l], QK_NOPE_DIM, 0)

        params = (mu, vec, r_k[l].reshape(1, w), wa_up.astype(_BF16), g_up.astype(_BF16),
                  head_sum, tri, q_norm_g[l][None], kv_norm_g[l][None],
                  wuq.astype(_BF16), wuk.astype(_BF16), w_uv[l].T.astype(_BF16))
        tm = PREP_TM if l == 0 else PREP_TM_LATER
        outs = _mixin(h, w_in_l, rope_for(tm * MIX_SUBTILES), params, tp, bsz, tm)
        if l == 0:
            h, outs = outs[0], outs[1:]
        pgate, kt, rt, kd, bd, v, pc, bonus, g, q, kk, vv = outs
        y = _wkv(kt, rt, kd, bd, v, pc, bsz)
        ymla = _attention(q, kk, vv, bsz)
        h = _merge(y, bonus, g, ymla, pgate, h, jnp.stack([lnx_g[l], lnx_b[l]]), head_sum,
                   w_proj_rwkv[l].astype(_BF16), w_proj_mla[l].astype(_BF16),
                   w_out[l].astype(_BF16), jnp.stack([ln1_g[l], ln1_b[l]]), tp, alpha)
        h = _ffn(h, w_up[l].astype(_BF16), conv_w[l], conv_b[l][None], w_down[l].astype(_BF16),
                 jnp.stack([ln2_g[l], ln2_b[l]]), tp, alpha, bsz, final=(l == depth - 1))
    return h
```

```python
import functools

import numpy as np
import jax
import jax.numpy as jnp
from jax import lax
from jax.experimental import pallas as pl
from jax.experimental.pallas import tpu as pltpu

N_META = 16
HEAD_DIM = 64
RWKV_HEADS = 8
RWKV_WIDTH = RWKV_HEADS * HEAD_DIM
DECAY_LORA = 64
ICLR_LORA = 64
GATE_LORA = 160
GN_EPS = 64e-5
MLA_HEADS = 8
QK_NOPE_DIM = 64
QK_ROPE_DIM = 32
V_HEAD_DIM = 64
Q_LORA_RANK = 256
KV_LORA_RANK = 256
ROPE_THETA = 10000.0
MLA_WIDTH = MLA_HEADS * V_HEAD_DIM
CONV_WIDTH = 3
LN_EPS = 1e-5
RMS_EPS = 1e-6

LANES = 128
MXU_DIM = 256
VMEM_LIMIT = 56 * 1024 * 1024

FRONT = MXU_DIM
PADF = FRONT - N_META
C_R, C_K, C_V = 0, RWKV_WIDTH, 2 * RWKV_WIDTH
C_WA = 3 * RWKV_WIDTH
GATE_LORA_PAD = 2 * LANES
C_G = C_WA + LANES
RW_END = C_G + GATE_LORA_PAD
C_CQ = RW_END
C_CKV = C_CQ + Q_LORA_RANK
C_KR = C_CKV + KV_LORA_RANK
C_GATE = C_KR + LANES
QK_PAD = LANES

WKV_CHUNK = 64
WKV_BATCH = 8
ATT_TILE = MXU_DIM
ATT_UNROLL = 4
ATT_PAIRS = 4
PREP_TM = 256
PREP_TM_LATER = 256
MIX_SUBTILES = 2
ROW_TM = 1024
FFN_TM = 256
MERGE_SUB = 256
FF_CHUNK = 256
NEG_BIG = -1e30
BF16_ROWS = 16

_F32 = jnp.float32
_BF16 = jnp.bfloat16


def _cparams(*sem):
    return pltpu.CompilerParams(dimension_semantics=sem, vmem_limit_bytes=VMEM_LIMIT)


def _dot(a, b):
    return jnp.dot(a, b, preferred_element_type=_F32)


def _dot_nt(a, b):
    return lax.dot_general(a, b, (((1,), (1,)), ((), ())), preferred_element_type=_F32)


def _dot_tn(a, b):
    return lax.dot_general(a, b, (((0,), (0,)), ((), ())), preferred_element_type=_F32)


def _split_dot(x, w_bf16, parts, left=False):
    acc = None
    rem = x
    for _ in range(parts):
        hi = rem.astype(_BF16)
        term = _dot(w_bf16, hi) if left else _dot(hi, w_bf16)
        acc = term if acc is None else acc + term
        rem = rem - hi.astype(_F32)
    return acc


def _head_sum(x, ones_bd):
    n = ones_bd.shape[0]
    xb = x.astype(_BF16)
    return jnp.concatenate([_dot(xb[:, c0:c0 + n], ones_bd) for c0 in range(0, x.shape[1], n)],
                           axis=1)


def _layer_norm(x, g, b):
    mu = jnp.mean(x, axis=-1, keepdims=True)
    xc = x - mu
    var = jnp.mean(xc * xc, axis=-1, keepdims=True)
    return xc * lax.rsqrt(var + LN_EPS) * g + b


def _sigmoid(x):
    return 1.0 / (1.0 + jnp.exp(-x))


def _valid_rows(tile_idx, tm, tp):
    row = tile_idx * tm + lax.broadcasted_iota(jnp.int32, (tm, 1), 0)
    bidx = jnp.floor((row.astype(_F32) + 0.5) * (1.0 / tp)).astype(jnp.int32)
    return (row - bidx * tp) >= PADF


def _mixin_multi_kernel(*refs, embed, tp, nsub):
    i = pl.program_id(0)
    n_in = (nsub + 3 if embed else 1) + 14
    ins, outs, carry_ref = refs[:n_in], refs[n_in:-1], refs[-1]

    @pl.when(i == 0)
    def _():
        carry_ref[...] = jnp.zeros_like(carry_ref)

    for sub in range(nsub):
        t_rows = outs[-2].shape[0] // nsub
        rs = pl.ds(sub * t_rows, t_rows)
        if embed:
            lead = (ins[sub],) + ins[nsub:nsub + 3]
            rest = ins[nsub + 3:]
        else:
            lead = (ins[0].at[rs],)
            rest = ins[1:]
        rest = (rest[0], rest[1].at[:, rs]) + rest[2:]
        n_pc = outs[-6].shape[0] // nsub
        row_outs = [o.at[pl.ds(sub * n_pc, n_pc)] if o is outs[-6] else o.at[rs]
                    for o in outs[:-1]]
        vv_view = outs[-1].at[:, rs]
        _mixin_kernel(*lead, *rest, *row_outs, vv_view, carry_ref,
                      embed=embed, tp=tp, tile=i * nsub + sub)


def _mixin_kernel(*refs, embed, tp, tile):
    if embed:
        x_ref, meta_ref, lng_ref, lnb_ref = refs[:4]
        refs = refs[4:]
    else:
        h_ref = refs[0]
        refs = refs[1:]
    (w_ref, rope_ref, mu_ref, vec_ref, rk_ref, wa_up_ref, g_up_ref, head_sum_ref, tri_ref,
     qg_ref, kvg_ref, wuq_ref, wuk_ref, wuv_ref) = refs[:14]
    refs = refs[14:]
    if embed:
        h_out_ref = refs[0]
        refs = refs[1:]
    (gate_ref, kt_ref, rt_ref, kd_ref, bd_ref, v_ref, pc_ref, bonus_ref, g_ref,
     q_ref, kk_ref, vv_ref, carry_ref) = refs
    i = tile
    tm = kt_ref.shape[0]
    w = RWKV_WIDTH

    if embed:
        src = jnp.where(i % (tp // tm) == 0, meta_ref[...], x_ref[0])
        h = _layer_norm(src, lng_ref[...], lnb_ref[...])
        h = jnp.where(_valid_rows(i, tm, tp), h, 0.0)
        h_out_ref[...] = h
    else:
        h = h_ref[...]
    hb = h.astype(_BF16)

    def gate_cols(c0, c1):
        gate_ref[:, c0:c1] = _dot(hb, w_ref[:, C_GATE + c0:C_GATE + c1]).astype(gate_ref.dtype)

    ngate = gate_ref.shape[1]
    gq = ngate // 4

    pr = _dot(hb, w_ref[:, :RW_END])
    gate_cols(0, gq)
    shifted = pltpu.roll(pr, 1, axis=0)
    row = lax.broadcasted_iota(jnp.int32, (tm, 1), 0)
    shifted = jnp.where(row == 0, carry_ref[7:8, :], shifted)
    carry_ref[...] = pr[tm - 8:, :]
    ps = pr + (shifted - pr) * mu_ref[...]

    r = ps[:, C_R:C_R + w]
    k = ps[:, C_K:C_K + w]
    v = ps[:, C_V:C_V + w]
    wa_in = ps[:, C_WA:C_WA + LANES]
    lane = lax.broadcasted_iota(jnp.int32, (1, LANES), 1)
    wa_in = jnp.where(lane < DECAY_LORA, jnp.tanh(wa_in), wa_in)
    wa = _dot(wa_in.astype(_BF16), wa_up_ref[...])
    w0, a0, k_k, k_a = (vec_ref[0:1, :], vec_ref[1:2, :], vec_ref[2:3, :], vec_ref[3:4, :])

    z = -(w0 + wa[:, :w])
    softplus = jnp.maximum(z, 0.0) + jnp.log1p(jnp.exp(-jnp.abs(z)))
    logw = -jnp.exp(-softplus - 0.5)
    a = _sigmoid(a0 + wa[:, w:])
    gate = _dot(_sigmoid(ps[:, C_G:C_G + GATE_LORA_PAD]).astype(_BF16), g_up_ref[...])
    gate_cols(gq, 2 * gq)

    head_sum = head_sum_ref[...]
    kk = k * k_k
    kk_ss = _head_sum(kk * kk, head_sum)
    kk = kk / jnp.maximum(jnp.sqrt(kk_ss), 1e-12)
    k = k * (1.0 + (a - 1.0) * k_a)
    b = kk * a
    bonus = _head_sum(r * k * rk_ref[...], head_sum) * v
    gate_cols(2 * gq, 3 * gq)

    tri = tri_ref[...]
    tn = tri.shape[0]
    cum = jnp.concatenate([_split_dot(logw[r0:r0 + tn], tri, 3, left=True)
                           for r0 in range(0, tm, tn)], axis=0)
    pm = _dot(hb, w_ref[:, C_CQ:C_GATE])
    e_neg = jnp.exp(-cum)
    kt_ref[...] = (kk * jnp.exp(cum - logw)).astype(_BF16)
    rt_ref[...] = (r * jnp.exp(cum)).astype(_BF16)
    kd_ref[...] = (k * e_neg).astype(_BF16)
    bd_ref[...] = (b * e_neg).astype(_BF16)
    v_ref[...] = v.astype(_BF16)
    bonus_ref[...] = bonus.astype(_BF16)
    g_ref[...] = gate.astype(_BF16)
    for ci in range(tm // WKV_CHUNK):
        last = (ci + 1) * WKV_CHUNK - 1
        pc_ref[ci] = jnp.exp(cum[last:last + 1, :])

    cos_t, sin_a, sin_b = rope_ref[0], rope_ref[1], rope_ref[2]

    def rope(x):
        nrep = x.shape[1] // LANES
        n = x.shape[1]
        c, sa, sb = (jnp.tile(t, (1, nrep)) if nrep > 1 else t for t in (cos_t, sin_a, sin_b))
        half = QK_ROPE_DIM // 2
        return x * c + pltpu.roll(x, n - half, axis=1) * sa + pltpu.roll(x, half, axis=1) * sb

    def rms(x, gain):
        return x * lax.rsqrt(jnp.mean(x * x, axis=-1, keepdims=True) + RMS_EPS) * gain

    cq = pm[:, :Q_LORA_RANK]
    ckv = pm[:, Q_LORA_RANK:Q_LORA_RANK + KV_LORA_RANK]
    kr = pm[:, C_KR - C_CQ:]
    qn = rms(cq, qg_ref[...]).astype(_BF16)
    scale = float(np.log2(np.e) / np.sqrt(QK_NOPE_DIM + QK_ROPE_DIM))
    q_ref[...] = (rope(_dot(qn, wuq_ref[...])) * scale).astype(_BF16)
    kvn = rms(ckv, kvg_ref[...]).astype(_BF16)
    gate_cols(3 * gq, ngate)
    kk_ref[...] = (_dot(kvn, wuk_ref[...]) + jnp.tile(rope(kr), (1, MLA_HEADS))).astype(_BF16)
    vv_ref[...] = _dot_nt(wuv_ref[...], kvn).astype(_BF16)


def _mixin(src, w_in, rope_tab, params, tp, bsz, tm):
    embed = isinstance(src, tuple)
    nsub = MIX_SUBTILES
    sub_rows = tm
    tm = sub_rows * nsub
    d = w_in.shape[0]
    r = bsz * tp
    nt = r // tm
    per = tp // sub_rows
    nrope = rope_tab.shape[1] // tm
    w = RWKV_WIDTH
    nchunk = tm // WKV_CHUNK
    ngate = w_in.shape[1] - C_GATE
    full = lambda a: pl.BlockSpec(a.shape, lambda i: (0,) * a.ndim)
    row_out = lambda n: pl.BlockSpec((tm, n), lambda i: (i, 0))
    if embed:
        assert FRONT == sub_rows
        x, meta_pad, ln_g, ln_b = src
        lead_args = [x] * nsub + [meta_pad, ln_g, ln_b]

        def x_spec(sub):
            tile = lambda i: i * nsub + sub
            return pl.BlockSpec(
                (1, sub_rows, d), lambda i: (tile(i) // per, jnp.maximum(tile(i) % per - 1, 0), 0))

        lead_specs = [x_spec(sub) for sub in range(nsub)] + [full(meta_pad), full(ln_g), full(ln_b)]
        lead_out, lead_out_specs = [jax.ShapeDtypeStruct((r, d), _F32)], [row_out(d)]
    else:
        lead_args, lead_specs = [src], [row_out(d)]
        lead_out, lead_out_specs = [], []
    out_shape = (
        lead_out
        + [jax.ShapeDtypeStruct((r, ngate), _BF16)]
        + [jax.ShapeDtypeStruct((r, w), _BF16)] * 5
        + [jax.ShapeDtypeStruct((nt * nchunk, 1, w), _F32)]
        + [jax.ShapeDtypeStruct((r, w), _BF16)] * 2
        + [jax.ShapeDtypeStruct((r, MLA_HEADS * QK_PAD), _BF16)] * 2
        + [jax.ShapeDtypeStruct((MLA_WIDTH, r), _BF16)]
    )
    out_specs = (
        lead_out_specs
        + [row_out(ngate)]
        + [row_out(w)] * 5
        + [pl.BlockSpec((nchunk, 1, w), lambda i: (i, 0, 0))]
        + [row_out(w)] * 2
        + [row_out(MLA_HEADS * QK_PAD)] * 2
        + [pl.BlockSpec((MLA_WIDTH, tm), lambda i: (0, i))]
    )
    return pl.pallas_call(
        functools.partial(_mixin_multi_kernel, embed=embed, tp=tp, nsub=nsub),
        out_shape=out_shape,
        grid=(nt,),
        in_specs=lead_specs + [full(w_in), pl.BlockSpec((3, tm, LANES), lambda i: (0, i % nrope, 0))]
        + [full(a) for a in params],
        out_specs=out_specs,
        scratch_shapes=[pltpu.VMEM((8, RW_END), _F32)],
        compiler_params=_cparams("arbitrary"),
        name="mix_in",
    )(*lead_args, w_in, rope_tab, *params)


def _wkv_kernel(kt_ref, rt_ref, kd_ref, bd_ref, v_ref, pc_ref, y_ref, h_scr):
    c = pl.program_id(1)

    @pl.when(c < PADF // WKV_CHUNK)
    def _():
        h_scr[...] = jnp.zeros_like(h_scr)
        y_ref[...] = jnp.zeros_like(y_ref)

    @pl.when(c >= PADF // WKV_CHUNK)
    def _():
        _wkv_chunk(kt_ref, rt_ref, kd_ref, bd_ref, v_ref, pc_ref, y_ref, h_scr)


def _wkv_chunk(kt_ref, rt_ref, kd_ref, bd_ref, v_ref, pc_ref, y_ref, h_scr):
    npair = RWKV_HEADS // 2
    cl = WKV_CHUNK
    n2 = 2 * cl
    lane = lax.broadcasted_iota(jnp.int32, (1, LANES), 1)
    first = lane < HEAD_DIM
    row = lax.broadcasted_iota(jnp.int32, (n2, n2), 0)
    col = lax.broadcasted_iota(jnp.int32, (n2, n2), 1)
    strict = row > col
    incl = row >= col
    eye = (row == col).astype(_F32)

    def stack(x):
        zero = jnp.zeros_like(x)
        return jnp.concatenate([jnp.where(first, x, zero), jnp.where(first, zero, x)], axis=0)

    def same_block(size):
        sh = size.bit_length() - 1
        return jnp.right_shift(row, sh) == jnp.right_shift(col, sh)

    units = [(bl, j) for bl in range(kt_ref.shape[0]) for j in range(npair)]
    each = lambda fn, *lists: [fn(*args) for args in zip(*lists)]
    bf = lambda x: x.astype(_BF16)

    def load(ref):
        return [stack(ref[bl, :, j * LANES:(j + 1) * LANES]) for bl, j in units]

    kt2, rt2, kd2, bd2, v2 = load(kt_ref), load(rt_ref), load(kd_ref), load(bd_ref), load(v_ref)
    lhs = each(lambda a, b: jnp.concatenate([a, b], axis=0), kt2, rt2)
    rhs = each(lambda a, b: jnp.concatenate([a, b], axis=0), kd2, bd2)
    gram = each(_dot_nt, lhs, rhs)
    a_kk = [bf(jnp.where(strict, g[:n2, :n2], 0.0)) for g in gram]
    a_kb = [jnp.where(strict, g[:n2, n2:], 0.0) for g in gram]
    a_out = [bf(jnp.concatenate([jnp.where(incl, g[n2:, :n2], 0.0),
                                 jnp.where(incl, -g[n2:, n2:], 0.0)], axis=1)) for g in gram]

    blk = same_block(8)
    nd = [bf(jnp.where(blk, a, 0.0)) for a in a_kb]
    s2 = [bf(x) for x in each(_dot, nd, nd)]
    i_minus = [eye - x.astype(_F32) for x in nd]
    p1 = each(lambda im, s: im + _dot(bf(im), s), i_minus, s2)
    s4 = [bf(x) for x in each(_dot, s2, s2)]
    t_inv = each(lambda p, s: p + _dot(bf(p), s), p1, s4)
    size = 8
    while size < cl:
        outer = same_block(2 * size)
        sel = outer & jnp.logical_not(blk)
        off = [bf(jnp.where(sel, a, 0.0)) for a in a_kb]
        tb = [bf(t) for t in t_inv]
        m = [bf(x) for x in each(_dot, tb, off)]
        t_inv = each(lambda t, mm, tbb: t - _dot(mm, tbb), t_inv, m, tb)
        blk = outer
        size *= 2
    tb = [bf(t) for t in t_inv]
    akv = each(_dot, a_kk, v2)

    hbd = [h_scr[i] for i in range(len(units))]
    x0 = each(lambda l, h: _dot(l, bf(h)), lhs, hbd)
    u2 = each(lambda t, x, w: bf(_dot(t, bf(x[:n2] + w))), tb, x0, akv)
    vu = each(lambda a, b: jnp.concatenate([a, b], axis=0), v2, u2)
    y2 = each(lambda x, a, z: x[n2:] + _dot(a, z), x0, a_out, vu)
    upd = each(lambda kd, bd, z: _dot_tn(jnp.concatenate([kd, -bd], axis=0), z), kd2, bd2, vu)
    for i, (bl, j) in enumerate(units):
        sl = slice(j * LANES, (j + 1) * LANES)
        y_ref[bl, :, sl] = (y2[i][:cl] + y2[i][cl:]).astype(y_ref.dtype)
        pc_row = pc_ref[bl, 0, :, sl]
        pc_col = jnp.transpose(jnp.broadcast_to(pc_row, (LANES, LANES)))
        h_scr[i] = (hbd[i] + upd[i]) * pc_col


def _wkv(kt, rt, kd, bd, v, pc, bsz):
    r, w = kt.shape
    tp = r // bsz
    nc = tp // WKV_CHUNK
    nb = WKV_BATCH
    assert bsz % nb == 0
    blk = pl.BlockSpec((nb, WKV_CHUNK, w), lambda bg, c: (bg, c, 0))
    args = [a.reshape(bsz, tp, w) for a in (kt, rt, kd, bd, v)] + [pc.reshape(bsz, nc, 1, w)]
    y = pl.pallas_call(
        _wkv_kernel,
        out_shape=jax.ShapeDtypeStruct((bsz, tp, w), _F32),
        grid=(bsz // nb, nc),
        in_specs=[blk] * 5 + [pl.BlockSpec((nb, 1, 1, w), lambda bg, c: (bg, c, 0, 0))],
        out_specs=blk,
        scratch_shapes=[pltpu.VMEM((nb * (RWKV_HEADS // 2), LANES, LANES), _F32)],
        compiler_params=_cparams("parallel", "arbitrary"),
        name="wkv7",
    )(*args)
    return y.reshape(r, w)


def _attn_kernel(q_ref, k_ref, v_ref, o_ref, s_scr, m_scr, acc_scr):
    i = pl.program_id(2)
    tq = q_ref.shape[0]
    tk = tq
    nhead = q_ref.shape[1] // QK_PAD
    q_pos = i * tq + lax.broadcasted_iota(jnp.int32, (1, tq), 1)
    head_cols = lambda n: slice(n * QK_PAD, (n + 1) * QK_PAD)
    qh = [q_ref[:, head_cols(n)] for n in range(nhead)]

    def scores(jb, n):
        start = pl.multiple_of(jb * tk, tk)
        s_scr[n] = _dot_nt(k_ref[pl.ds(start, tk), head_cols(n)], qh[n])

    def softmax(n, s, first=False):
        m_new = jnp.max(s, axis=0, keepdims=True)
        rescale = None
        if not first:
            m_prev = m_scr[n]
            m_new = jnp.maximum(m_prev, m_new)
            rescale = jnp.exp2(m_prev - m_new)
        m_scr[n] = m_new
        return jnp.exp2(s - m_new).astype(_BF16), rescale

    def values(n, start, prob, rescale):
        if prob.shape[0] < tk:
            prob = jnp.concatenate([jnp.zeros((tk - prob.shape[0], tq), _BF16), prob], axis=0)
        lo = (n // 2) * LANES + (n % 2) * V_HEAD_DIM
        vt_h = jnp.concatenate([v_ref[lo:lo + V_HEAD_DIM, pl.ds(start, tk)],
                                jnp.ones((BF16_ROWS, tk), _BF16)], axis=0)
        pv = _dot(vt_h, prob)
        acc_scr[n] = pv if rescale is None else rescale * acc_scr[n] + pv

    ahead = 6
    behind = 1

    def block(jb, masked, last):
        start = pl.multiple_of(jb * tk, tk)
        if masked:
            mask = (start + lax.broadcasted_iota(jnp.int32, (tk, 1), 0)) <= q_pos
        pending = []
        for n in range(nhead):
            if n + ahead < nhead:
                scores(jb, n + ahead)
            elif not last:
                scores(jb + 1, n + ahead - nhead)
            s = s_scr[n]
            if masked:
                s = jnp.where(mask, s, NEG_BIG)
            pending.append((n,) + softmax(n, s))
            if len(pending) > behind:
                m, prob, rescale = pending.pop(0)
                values(m, start, prob, rescale)
        for m, prob, rescale in pending:
            values(m, start, prob, rescale)

    def front(s_front):
        mask = (PADF + lax.broadcasted_iota(jnp.int32, (N_META, 1), 0)) <= q_pos
        for n in range(nhead):
            prob, _ = softmax(n, jnp.where(mask, s_front[n], NEG_BIG), first=True)
            values(n, 0, prob, None)

    s_front = [_dot_nt(k_ref[PADF:FRONT, head_cols(n)], qh[n]) for n in range(nhead)]
    for n in range(ahead):
        scores(1, n)
    front(s_front)

    def several_blocks(pidx, carry):
        for u in range(ATT_UNROLL):
            block(ATT_UNROLL * pidx + 1 + u, False, False)
        return carry

    def one_block(jb, carry):
        block(jb, False, False)
        return carry

    def diagonal(_, carry):
        block(i, True, True)
        return carry

    n_plain = jnp.maximum(i - 1, 0)
    n_rolled = n_plain // ATT_UNROLL
    lax.fori_loop(0, n_rolled, several_blocks, 0)
    lax.fori_loop(ATT_UNROLL * n_rolled + 1, n_plain + 1, one_block, 0)
    lax.fori_loop(0, jnp.minimum(i, 1), diagonal, 0)
    for g in range(nhead // 2):
        halves = [acc_scr[n, :V_HEAD_DIM] / acc_scr[n, V_HEAD_DIM:V_HEAD_DIM + 1]
                  for n in (2 * g, 2 * g + 1)]
        o = jnp.concatenate(halves, axis=0)
        o_ref[:, g * LANES:(g + 1) * LANES] = jnp.transpose(o).astype(o_ref.dtype)


def _attention(q, k, v, bsz):
    r = q.shape[0]
    tp = r // bsz
    nq = tp // ATT_TILE
    ng = MLA_HEADS // 2 // ATT_PAIRS
    qw, vw = ATT_PAIRS * 2 * QK_PAD, ATT_PAIRS * 2 * V_HEAD_DIM
    return pl.pallas_call(
        _attn_kernel,
        out_shape=jax.ShapeDtypeStruct((r, MLA_WIDTH), _BF16),
        grid=(bsz, ng, nq),
        in_specs=[
            pl.BlockSpec((ATT_TILE, qw), lambda bi, pj, i: (bi * nq + i, pj)),
            pl.BlockSpec((tp, qw), lambda bi, pj, i: (bi, pj)),
            pl.BlockSpec((vw, tp), lambda bi, pj, i: (pj, bi)),
        ],
        out_specs=pl.BlockSpec((ATT_TILE, vw), lambda bi, pj, i: (bi * nq + i, pj)),
        scratch_shapes=[pltpu.VMEM((2 * ATT_PAIRS, ATT_TILE, ATT_TILE), _F32)]
        + [pltpu.VMEM((2 * ATT_PAIRS, 1, ATT_TILE), _F32),
           pltpu.VMEM((2 * ATT_PAIRS, V_HEAD_DIM + BF16_ROWS, ATT_TILE), _F32)],
        compiler_params=_cparams("parallel", "parallel", "arbitrary"),
        name="mla_attention",
    )(q, k, v)


def _merge_kernel(y_ref, bonus_ref, g_ref, ymla_ref, gate_ref, h_ref, lnx_ref, head_sum_ref,
                  wpr_ref, wpm_ref, wout_ref, ln_ref, o_ref, *, tp, alpha):
    tm = y_ref.shape[0]
    d = h_ref.shape[1]
    head_sum = head_sum_ref[...]
    valid = _valid_rows(pl.program_id(0), tm, tp)
    subs = [slice(r0, r0 + MERGE_SUB) for r0 in range(0, tm, MERGE_SUB)]
    ys = [y_ref[s, :].astype(_F32) for s in subs]
    yms = [_head_sum(y, head_sum) * (1.0 / HEAD_DIM) for y in ys]
    ycs = [y - ym for y, ym in zip(ys, yms)]
    yvs = [_head_sum(yc * yc, head_sum) * (1.0 / HEAD_DIM) for yc in ycs]
    y_rwkv = [((yc * lax.rsqrt(yv + GN_EPS) * lnx_ref[0:1, :] + lnx_ref[1:2, :]
                + bonus_ref[s, :].astype(_F32)) * g_ref[s, :].astype(_F32)).astype(_BF16)
              for s, yc, yv in zip(subs, ycs, yvs)]
    pr = [_dot(yr, wpr_ref[...]) for yr in y_rwkv]
    pm = [_dot(ymla_ref[s, :], wpm_ref[...]) for s in subs]
    mixed = []
    for s, a, b in zip(subs, pr, pm):
        gates = _sigmoid(gate_ref[s, :].astype(_F32))
        mixed.append((gates[:, :d] * a + gates[:, d:] * b).astype(_BF16))
    outs = [_dot(m, wout_ref[...]) for m in mixed]
    for s, out in zip(subs, outs):
        hn = _layer_norm(alpha * h_ref[s, :] + out, ln_ref[0:1, :], ln_ref[1:2, :])
        o_ref[s, :] = jnp.where(valid[s], hn, 0.0)


def _merge(y, bonus, g, ymla, pgate, h, lnx, head_mean, wpr, wpm, wout, ln, tp, alpha):
    r, d = h.shape
    tm = ROW_TM
    w = RWKV_WIDTH
    full = lambda a: pl.BlockSpec(a.shape, lambda i: (0,) * a.ndim)
    rows = lambda n: pl.BlockSpec((tm, n), lambda i: (i, 0))
    return pl.pallas_call(
        functools.partial(_merge_kernel, tp=tp, alpha=alpha),
        out_shape=jax.ShapeDtypeStruct((r, d), _F32),
        grid=(r // tm,),
        in_specs=[rows(w), rows(w), rows(w), rows(MLA_WIDTH),
                  rows(2 * d),
                  rows(d), full(lnx), full(head_mean), full(wpr), full(wpm), full(wout), full(ln)],
        out_specs=rows(d),
        compiler_params=_cparams("parallel"),
        name="merge",
    )(y, bonus, g, ymla, pgate, h, lnx, head_mean, wpr, wpm, wout, ln)


def _ffn_kernel(h_ref, wup_ref, cw_ref, cb_ref, wdown_ref, ln_ref, o_ref, carry_ref, act_ref,
                pre_ref, *, tp, alpha):
    i = pl.program_id(0)
    tm, d = h_ref.shape
    dff = wdown_ref.shape[0]

    @pl.when(i == 0)
    def _():
        carry_ref[0:8, :] = jnp.zeros((8, carry_ref.shape[1]), _F32)
        pre_ref[...] = jnp.zeros_like(pre_ref)

    h = h_ref[...]
    hb = h.astype(_BF16)
    def conv(u, c0):
        cs = slice(c0, c0 + FF_CHUNK)
        carry_ref[8:, cs] = u
        u1 = carry_ref[7:7 + tm, cs]
        u2 = carry_ref[6:6 + tm, cs]
        out = (cw_ref[2:3, cs] * u + cw_ref[1:2, cs] * u1 + cw_ref[0:1, cs] * u2 + cb_ref[:, cs])
        carry_ref[0:8, cs] = u[tm - 8:, :]
        return out

    def up(c0):
        return (_dot(hb, wup_ref[:, c0:c0 + FF_CHUNK]),
                _dot(hb, wup_ref[:, dff + c0:dff + c0 + FF_CHUNK]))

    chunks = list(range(0, dff, FF_CHUNK))
    ahead = up(chunks[0])
    hn = _layer_norm(pre_ref[...], ln_ref[0:1, :], ln_ref[1:2, :])
    o_ref[...] = jnp.where(_valid_rows(i - 1, tm, tp), hn, 0.0).reshape(o_ref.shape)
    for n, c0 in enumerate(chunks):
        gate_raw, val_raw = ahead
        if n + 1 < len(chunks):
            ahead = up(chunks[n + 1])
        gate = conv(gate_raw, c0)
        val = conv(val_raw, dff + c0)
        act_ref[:, c0:c0 + FF_CHUNK] = (gate * _sigmoid(gate) * val).astype(_BF16)
    pre_ref[...] = alpha * h + _dot(act_ref[...], wdown_ref[...])


def _ffn(h, wup, cw, cb, wdown, ln, tp, alpha, bsz, final):
    r, d = h.shape
    tm = FFN_TM
    full = lambda a: pl.BlockSpec(a.shape, lambda i: (0,) * a.ndim, pipeline_mode=pl.Buffered(1))
    nt = r // tm
    done = lambda i: jnp.maximum(i - 1, 0)
    if final:
        per = tp // tm
        assert FRONT == tm
        out_shape = jax.ShapeDtypeStruct((bsz, tp - FRONT, d), _F32)
        out_spec = pl.BlockSpec(
            (1, tm, d), lambda i: (done(i) // per, jnp.maximum(done(i) % per - 1, 0), 0))
    else:
        out_shape = jax.ShapeDtypeStruct((r, d), _F32)
        out_spec = pl.BlockSpec((tm, d), lambda i: (done(i), 0))
    return pl.pallas_call(
        functools.partial(_ffn_kernel, tp=tp, alpha=alpha),
        out_shape=out_shape,
        grid=(nt + 1,),
        in_specs=[pl.BlockSpec((tm, d), lambda i: (jnp.minimum(i, nt - 1), 0)),
                  full(wup), full(cw), full(cb), full(wdown), full(ln)],
        out_specs=out_spec,
        scratch_shapes=[pltpu.VMEM((8 + tm, wup.shape[1]), _F32),
                        pltpu.VMEM((tm, wdown.shape[0]), _BF16),
                        pltpu.VMEM((tm, d), _F32)],
        compiler_params=_cparams("arbitrary"),
        name="conv_ffn",
    )(h, wup, cw, cb, wdown, ln)


def _pad_cols(a, n):
    return jnp.pad(a, ((0, 0), (0, n - a.shape[1])))


def _rearrange_in_cols(a):
    w = RWKV_WIDTH
    o_mla = 3 * w + DECAY_LORA + ICLR_LORA + GATE_LORA
    o_gate = o_mla + Q_LORA_RANK + KV_LORA_RANK + QK_ROPE_DIM
    rows = a.shape[0]
    z = lambda n: jnp.zeros((rows, n), a.dtype)
    kr = a[:, o_mla + Q_LORA_RANK + KV_LORA_RANK:o_gate]
    parts = [
        a[:, :3 * w + DECAY_LORA + ICLR_LORA],
        a[:, 3 * w + DECAY_LORA + ICLR_LORA:o_mla], z(GATE_LORA_PAD - GATE_LORA),
        a[:, o_mla:o_mla + Q_LORA_RANK + KV_LORA_RANK],
        z(QK_NOPE_DIM), kr, z(LANES - QK_NOPE_DIM - QK_ROPE_DIM),
        a[:, o_gate:],
    ]
    return jnp.concatenate(parts, axis=1)


def _pad_heads(wmat, per_head, lead):
    kdim = wmat.shape[0]
    wh = wmat.reshape(kdim, MLA_HEADS, per_head)
    wh = jnp.pad(wh, ((0, 0), (0, 0), (lead, QK_PAD - per_head - lead)))
    return wh.reshape(kdim, MLA_HEADS * QK_PAD)


def _rope_tables(tp, reps):
    half = QK_ROPE_DIM // 2
    pos = (jnp.arange(tp, dtype=jnp.int32) - PADF).astype(_F32)
    inv_freq = ROPE_THETA ** (-jnp.arange(half, dtype=_F32) / half)
    ang = pos[:, None] * inv_freq[None, :]
    cos, sin = jnp.cos(ang), jnp.sin(ang)
    zeros = lambda n: jnp.zeros((tp, n), _F32)
    tail = LANES - QK_NOPE_DIM - QK_ROPE_DIM
    cos_t = jnp.concatenate([jnp.ones((tp, QK_NOPE_DIM), _F32), cos, cos, zeros(tail)], axis=1)
    sin_a = jnp.concatenate([zeros(QK_NOPE_DIM), -sin, zeros(half + tail)], axis=1)
    sin_b = jnp.concatenate([zeros(QK_NOPE_DIM + half), sin, zeros(tail)], axis=1)
    tab = jnp.stack([cos_t, sin_a, sin_b])
    return jnp.tile(tab, (1, reps, 1))


def kernel(x, meta_tokens, ln_in_g, ln_in_b, w_in, mu_shift, w0, w_lora_up, a0, a_lora_up,
           g_lora_up, k_k, k_a, r_k, lnx_g, lnx_b, q_norm_g, w_uq, kv_norm_g, w_uk, w_uv,
           w_proj_rwkv, w_proj_mla, w_out, ln1_g, ln1_b, w_up, conv_w, conv_b, w_down,
           ln2_g, ln2_b):
    bsz, seq, d = x.shape
    depth = w_in.shape[0]
    tp = seq + FRONT
    r = bsz * tp
    assert tp % ATT_TILE == 0 and r % ROW_TM == 0 and r % PREP_TM == 0
    alpha = float((2 * depth) ** 0.25)
    w = RWKV_WIDTH

    meta_pad = jnp.pad(meta_tokens.astype(x.dtype), ((PADF, 0), (0, 0)))
    h = (x, meta_pad, ln_in_g[None], ln_in_b[None])

    hid = jnp.arange(MXU_DIM) // HEAD_DIM
    head_sum = (hid[:, None] == hid[None, :]).astype(_BF16)
    tt = jnp.arange(PREP_TM)
    tri = ((tt[:, None] >= tt[None, :])
           & (tt[:, None] // WKV_CHUNK == tt[None, :] // WKV_CHUNK)).astype(_BF16)

    def rope_for(tm):
        reps = 1
        while (reps * tp) % tm:
            reps += 1
        return _rope_tables(tp, reps)

    for l in range(depth):
        w_in_l = _rearrange_in_cols(w_in[l]).astype(_BF16)
        mu = _rearrange_in_cols(jnp.pad(mu_shift[l][None], ((0, 0), (0, w_in.shape[2] - mu_shift.shape[1]))))
        mu = mu[:, :RW_END]
        vec = jnp.stack([w0[l], a0[l], k_k[l], k_a[l]])
        wa_up = jnp.concatenate([
            jnp.concatenate([w_lora_up[l], jnp.zeros((DECAY_LORA, w), _F32)], axis=1),
            jnp.concatenate([jnp.zeros((ICLR_LORA, w), _F32), a_lora_up[l]], axis=1)], axis=0)
        g_up = jnp.pad(g_lora_up[l], ((0, GATE_LORA_PAD - GATE_LORA), (0, 0)))
        wuq = _pad_heads(w_uq[l], QK_NOPE_DIM + QK_ROPE_DIM, 0)
        wuk = _pad_heads(w_uk[l], QK_NOPE_DIM, 0)

        params = (mu, vec, r_k[l].reshape(1, w), wa_up.astype(_BF16), g_up.astype(_BF16),
                  head_sum, tri, q_norm_g[l][None], kv_norm_g[l][None],
                  wuq.astype(_BF16), wuk.astype(_BF16), w_uv[l].T.astype(_BF16))
        tm = PREP_TM if l == 0 else PREP_TM_LATER
        outs = _mixin(h, w_in_l, rope_for(tm * MIX_SUBTILES), params, tp, bsz, tm)
        if l == 0:
            h, outs = outs[0], outs[1:]
        pgate, kt, rt, kd, bd, v, pc, bonus, g, q, kk, vv = outs
        y = _wkv(kt, rt, kd, bd, v, pc, bsz)
        ymla = _attention(q, kk, vv, bsz)
        h = _merge(y, bonus, g, ymla, pgate, h, jnp.stack([lnx_g[l], lnx_b[l]]), head_sum,
                   w_proj_rwkv[l].astype(_BF16), w_proj_mla[l].astype(_BF16),
                   w_out[l].astype(_BF16), jnp.stack([ln1_g[l], ln1_b[l]]), tp, alpha)
        h = _ffn(h, w_up[l].astype(_BF16), conv_w[l], conv_b[l][None], w_down[l].astype(_BF16),
                 jnp.stack([ln2_g[l], ln2_b[l]]), tp, alpha, bsz, final=(l == depth - 1))
    return h
```
